```python
import math
import jax, jax.numpy as jnp
from jax import lax
import numpy as np

D_MODEL = 2048
BATCH = 8
SEQ = 2048
DEPTH = 2

CHUNK = 128
POOL_WIDTH = D_MODEL
POOL_GROUPS = 4
POOL_WINDOWS = (2, 4, 8, 16)
POOL_GDIM = POOL_WIDTH // POOL_GROUPS
SSM_WIDTH = D_MODEL
SSM_HEAD_DIM = 64
SSM_HEADS = SSM_WIDTH // SSM_HEAD_DIM
SSM_GROUPS = 4
SSM_STATE = 128
SSM_CONV = 4
SSM_BC = SSM_GROUPS * SSM_STATE
SSM_CONV_DIM = SSM_WIDTH + 2 * SSM_BC
MLSTM_WIDTH = D_MODEL
MLSTM_HEADS = 8
MLSTM_HEAD_DIM = MLSTM_WIDTH // MLSTM_HEADS
N_BRANCH = 3
ALPHA = (2 * DEPTH) ** 0.25
BETA = (8 * DEPTH) ** -0.25
EPS = 1e-5

IN_SIZES = (POOL_WIDTH, POOL_WIDTH,
            SSM_WIDTH, SSM_WIDTH, SSM_BC, SSM_BC, SSM_HEADS,
            MLSTM_WIDTH, MLSTM_WIDTH, MLSTM_WIDTH, MLSTM_WIDTH, MLSTM_WIDTH,
            MLSTM_HEADS, MLSTM_HEADS,
            N_BRANCH * D_MODEL)
IN_DIM = sum(IN_SIZES)

kernel_name = "hybrid_pool_ssd_mlstm_gated_deepnorm"

F32 = jnp.float32


def _split_points():
    pts, acc = [], 0
    for n in IN_SIZES[:-1]:
        acc += n
        pts.append(acc)
    return pts


def layer_norm(h, g, b):
    h = h.astype(F32)
    mu = jnp.mean(h, -1, keepdims=True)
    var = jnp.mean(jnp.square(h - mu), -1, keepdims=True)
    return (h - mu) * lax.rsqrt(var + EPS) * g + b


def pool_mixer(u, w_pool, pool_scale):
    b, s, _ = u.shape
    uf = u.astype(F32).reshape(b, s, POOL_GROUPS, POOL_GDIM)
    cs = jnp.cumsum(uf, axis=1)
    cs = jnp.concatenate([jnp.zeros_like(cs[:, :1]), cs], axis=1)
    t = jnp.arange(s)
    outs = []
    for g, w in enumerate(POOL_WINDOWS):
        start = jnp.maximum(t + 1 - w, 0)
        win_sum = cs[:, 1:, g] - cs[:, start, g]
        cnt = jnp.minimum(t + 1, w).astype(F32)
        outs.append(win_sum / cnt[None, :, None] - uf[:, :, g])
    pooled = jnp.stack(outs, axis=2)
    mixed = jnp.einsum('bsgc,gcd->bsgd', pooled, w_pool.astype(F32))
    return mixed.reshape(b, s, POOL_WIDTH) * pool_scale.astype(F32)


def causal_depthwise_conv(x, w, bias):
    k = w.shape[0]
    y = lax.conv_general_dilated(x, w[:, None, :], window_strides=(1,), padding=[(k - 1, 0)],
                                 dimension_numbers=('NWC', 'WIO', 'NWC'),
                                 feature_group_count=x.shape[-1])
    return y + bias


def segsum(a):
    T = a.shape[-1]
    cs = jnp.cumsum(a, axis=-1)
    diff = cs[..., :, None] - cs[..., None, :]
    mask = jnp.tril(jnp.ones((T, T), dtype=bool))
    return jnp.where(mask, diff, -jnp.inf)


def ssd_scan(xh, a, bm, cm):
    b, s, h, p = xh.shape
    g, n = bm.shape[2], bm.shape[3]
    r = h // g
    c = s // CHUNK
    X = xh.reshape(b, c, CHUNK, g, r, p)
    A = a.reshape(b, c, CHUNK, g, r).transpose(0, 3, 4, 1, 2)
    Bc = bm.reshape(b, c, CHUNK, g, n)
    Cc = cm.reshape(b, c, CHUNK, g, n)
    A_cs = jnp.cumsum(A, axis=-1)
    Lmat = jnp.exp(segsum(A))
    CB = jnp.einsum('bclgn,bcsgn->bcgls', Cc, Bc)
    y_diag = jnp.einsum('bcgls,bgrcls,bcsgrp->bclgrp', CB, Lmat, X)
    decay_states = jnp.exp(A_cs[..., -1:] - A_cs)
    states = jnp.einsum('bclgn,bgrcl,bclgrp->bcgrpn', Bc, decay_states, X)
    chunk_decay = jnp.exp(A_cs[..., -1])

    def step(carry, inp):
        st, dec = inp
        return carry * dec[..., None, None] + st, carry

    init = jnp.zeros((b, g, r, p, n), F32)
    _, prev = lax.scan(step, init, (states.transpose(1, 0, 2, 3, 4, 5),
                                    chunk_decay.transpose(3, 0, 1, 2)))
    prev = prev.transpose(1, 0, 2, 3, 4, 5)
    y_off = jnp.einsum('bclgn,bcgrpn,bgrcl->bclgrp', Cc, prev, jnp.exp(A_cs))
    return (y_diag + y_off).reshape(b, s, h * p)


def mamba2_branch(xs, z, bm, cm, dt_raw, conv_w, conv_b, dt_bias, a_log, d_skip, norm_w):
    b, s, _ = xs.shape
    xbc = jnp.concatenate([xs, bm, cm], axis=-1).astype(F32)
    xbc = jax.nn.silu(causal_depthwise_conv(xbc, conv_w.astype(F32), conv_b.astype(F32)))
    xs, bm, cm = jnp.split(xbc, [SSM_WIDTH, SSM_WIDTH + SSM_BC], axis=-1)
    dt = jax.nn.softplus(dt_raw.astype(F32) + dt_bias.astype(F32))
    A = -jnp.exp(a_log.astype(F32))
    xh = xs.reshape(b, s, SSM_HEADS, SSM_HEAD_DIM)
    y = ssd_scan(xh * dt[..., None], dt * A,
                 bm.reshape(b, s, SSM_GROUPS, SSM_STATE), cm.reshape(b, s, SSM_GROUPS, SSM_STATE))
    y = y + (xh * d_skip.astype(F32)[:, None]).reshape(b, s, SSM_WIDTH)
    y = y * jax.nn.silu(z.astype(F32))
    yg = y.reshape(b, s, SSM_GROUPS, -1)
    yg = yg * lax.rsqrt(jnp.mean(yg * yg, -1, keepdims=True) + EPS)
    return yg.reshape(b, s, SSM_WIDTH) * norm_w.astype(F32)


def mlstm_chunkwise(q, k, v, ig, lf):
    b, h, s, d = q.shape
    c = s // CHUNK
    q = q.reshape(b, h, c, CHUNK, d)
    k = k.reshape(b, h, c, CHUNK, d)
    v = v.reshape(b, h, c, CHUNK, d)
    ig = ig.reshape(b, h, c, CHUNK)
    lf = lf.reshape(b, h, c, CHUNK)
    bcum = jnp.cumsum(lf, axis=-1)
    b_last = bcum[..., -1]
    w_log = b_last[..., None] - bcum + ig
    m_loc = jnp.max(w_log, axis=-1)
    wgt = jnp.exp(w_log - m_loc[..., None])
    C_loc = jnp.einsum('bhcl,bhcld,bhcle->bhcde', wgt, k, v)
    n_loc = jnp.einsum('bhcl,bhcld->bhcd', wgt, k)

    def step(carry, inp):
        Cp, npv, mp = carry
        Cl, nl, ml, bl = inp
        m_new = jnp.maximum(bl + mp, ml)
        s_old = jnp.exp(bl + mp - m_new)
        s_loc = jnp.exp(ml - m_new)
        Cn = Cp * s_old[..., None, None] + Cl * s_loc[..., None, None]
        nn = npv * s_old[..., None] + nl * s_loc[..., None]
        return (Cn, nn, m_new), (Cp, npv, mp)

    init = (jnp.zeros((b, h, d, d), F32), jnp.zeros((b, h, d), F32), jnp.zeros((b, h), F32))
    _, (C_prev, n_prev, m_prev) = lax.scan(
        step, init, (C_loc.transpose(2, 0, 1, 3, 4), n_loc.transpose(2, 0, 1, 3),
                     m_loc.transpose(2, 0, 1), b_last.transpose(2, 0, 1)))
    C_prev = C_prev.transpose(1, 2, 0, 3, 4)
    n_prev = n_prev.transpose(1, 2, 0, 3)
    m_prev = m_prev.transpose(1, 2, 0)
    Dlog = bcum[..., :, None] - bcum[..., None, :] + ig[..., None, :]
    mask = jnp.tril(jnp.ones((CHUNK, CHUNK), dtype=bool))
    Dlog = jnp.where(mask, Dlog, -jnp.inf)
    inter_log = bcum + m_prev[..., None]
    m_t = jnp.maximum(jnp.max(Dlog, axis=-1), inter_log)
    Dw = jnp.exp(Dlog - m_t[..., None])
    inter_w = jnp.exp(inter_log - m_t)
    qk = jnp.einsum('bhcld,bhcsd->bhcls', q, k) * Dw
    num = (jnp.einsum('bhcls,bhcse->bhcle', qk, v)
           + inter_w[..., None] * jnp.einsum('bhcld,bhcde->bhcle', q, C_prev))
    den = jnp.sum(qk, axis=-1) + inter_w * jnp.einsum('bhcld,bhcd->bhcl', q, n_prev)
    hcell = num / jnp.maximum(jnp.abs(den), jnp.exp(-m_t))[..., None]
    return hcell.reshape(b, h, s, d)


def mlstm_branch(q, k, v, o, i_pre, f_pre, z, i_bias, f_bias, norm_w):
    b, s, _ = q.shape

    def heads(t):
        return t.astype(F32).reshape(b, s, MLSTM_HEADS, MLSTM_HEAD_DIM).transpose(0, 2, 1, 3)

    qh, kh, vh = heads(q), heads(k) * (MLSTM_HEAD_DIM ** -0.5), heads(v)
    ig = (i_pre.astype(F32) + i_bias.astype(F32)).transpose(0, 2, 1)
    lf = jax.nn.log_sigmoid(f_pre.astype(F32) + f_bias.astype(F32)).transpose(0, 2, 1)
    hc = mlstm_chunkwise(qh, kh, vh, ig, lf).transpose(0, 2, 1, 3)
    hc = jax.nn.sigmoid(o.astype(F32)).reshape(b, s, MLSTM_HEADS, MLSTM_HEAD_DIM) * hc
    mu = jnp.mean(hc, -1, keepdims=True)
    var = jnp.mean(jnp.square(hc - mu), -1, keepdims=True)
    hc = (hc - mu) * lax.rsqrt(var + EPS)
    hc = hc.reshape(b, s, MLSTM_WIDTH) * norm_w.astype(F32)
    return hc * jax.nn.silu(z.astype(F32))


def hybrid_layer(x, w_in, b_gate, w_pool, pool_scale, conv_w, conv_b, dt_bias, a_log, d_skip,
                 ssm_norm_w, i_bias, f_bias, mlstm_norm_w, w_branch, w_out, ln_g, ln_b):
    b, s, d = x.shape
    proj = jnp.einsum('bsd,de->bse', x, w_in)
    (pu, pz, sx, sz, sB, sC, sdt, mq, mk, mv, mo, mz, mi, mf, gates) = jnp.split(
        proj, _split_points(), axis=-1)
    y_pool = pool_mixer(pu, w_pool, pool_scale) * jax.nn.silu(pz.astype(F32))
    y_ssm = mamba2_branch(sx, sz, sB, sC, sdt, conv_w, conv_b, dt_bias, a_log, d_skip, ssm_norm_w)
    y_mlstm = mlstm_branch(mq, mk, mv, mo, mi, mf, mz, i_bias, f_bias, mlstm_norm_w)
    g = jax.nn.sigmoid(gates.astype(F32).reshape(b, s, N_BRANCH, d) + b_gate.astype(F32))
    wb = w_branch.astype(F32)
    merged = (g[:, :, 0] * jnp.einsum('bsw,wd->bsd', y_pool, wb[0])
              + g[:, :, 1] * jnp.einsum('bsw,wd->bsd', y_ssm, wb[1])
              + g[:, :, 2] * jnp.einsum('bsw,wd->bsd', y_mlstm, wb[2]))
    out = jnp.einsum('bsd,de->bse', merged, w_out.astype(F32))
    h = ALPHA * x.astype(F32) + out
    return layer_norm(h, ln_g.astype(F32), ln_b.astype(F32)).astype(x.dtype)


def setup_inputs(seed: int = 0) -> dict:
    key = jax.random.key(seed)
    ks = jax.random.split(key, 20)
    L = DEPTH
    nrm = jax.random.normal
    x = nrm(ks[0], (BATCH, SEQ, D_MODEL), F32)
    w_in = nrm(ks[1], (L, D_MODEL, IN_DIM), F32) * D_MODEL ** -0.5
    b_gate = 0.01 * nrm(ks[2], (L, N_BRANCH, D_MODEL), F32)
    w_pool = nrm(ks[3], (L, POOL_GROUPS, POOL_GDIM, POOL_GDIM), F32) * POOL_GDIM ** -0.5
    pool_scale = 1.0 + 0.02 * nrm(ks[4], (L, POOL_WIDTH), F32)
    conv_w = nrm(ks[5], (L, SSM_CONV, SSM_CONV_DIM), F32) * SSM_CONV ** -0.5
    conv_b = 0.01 * nrm(ks[6], (L, SSM_CONV_DIM), F32)
    dt0 = jnp.exp(jax.random.uniform(ks[7], (L, SSM_HEADS), F32, math.log(1e-3), math.log(1e-1)))
    dt_bias = dt0 + jnp.log(-jnp.expm1(-dt0))
    a_log = jnp.log(jax.random.uniform(ks[8], (L, SSM_HEADS), F32, 1.0, 16.0))
    d_skip = 1.0 + 0.01 * nrm(ks[9], (L, SSM_HEADS), F32)
    ssm_norm_w = 1.0 + 0.02 * nrm(ks[10], (L, SSM_WIDTH), F32)
    i_bias = 0.1 * nrm(ks[11], (L, MLSTM_HEADS), F32)
    f_bias = jnp.linspace(3.0, 6.0, MLSTM_HEADS, dtype=F32)[None, :] + 0.01 * nrm(ks[12], (L, MLSTM_HEADS), F32)
    mlstm_norm_w = 1.0 + 0.02 * nrm(ks[13], (L, MLSTM_WIDTH), F32)
    w_branch = nrm(ks[14], (L, N_BRANCH, D_MODEL, D_MODEL), F32) * (D_MODEL ** -0.5) * BETA
    w_out = nrm(ks[15], (L, D_MODEL, D_MODEL), F32) * (D_MODEL ** -0.5) * BETA
    ln_g = 1.0 + 0.02 * nrm(ks[16], (L, D_MODEL), F32)
    ln_b = 0.01 * nrm(ks[17], (L, D_MODEL), F32)
    return {"x": x, "w_in": w_in, "b_gate": b_gate, "w_pool": w_pool, "pool_scale": pool_scale,
            "conv_w": conv_w, "conv_b": conv_b, "dt_bias": dt_bias, "a_log": a_log,
            "d_skip": d_skip, "ssm_norm_w": ssm_norm_w, "i_bias": i_bias, "f_bias": f_bias,
            "mlstm_norm_w": mlstm_norm_w, "w_branch": w_branch, "w_out": w_out,
            "ln_g": ln_g, "ln_b": ln_b}


def reference(x, w_in, b_gate, w_pool, pool_scale, conv_w, conv_b, dt_bias, a_log, d_skip,
              ssm_norm_w, i_bias, f_bias, mlstm_norm_w, w_branch, w_out, ln_g, ln_b):
    h = x
    for l in range(DEPTH):
        h = hybrid_layer(h, w_in[l], b_gate[l], w_pool[l], pool_scale[l], conv_w[l], conv_b[l],
                         dt_bias[l], a_log[l], d_skip[l], ssm_norm_w[l], i_bias[l], f_bias[l],
                         mlstm_norm_w[l], w_branch[l], w_out[l], ln_g[l], ln_b[l])
    return h
```

```python
import functools
import math

import jax
import jax.numpy as jnp
from jax import lax
from jax.experimental import pallas as pl
from jax.experimental.pallas import tpu as pltpu

F32 = jnp.float32
BF16 = jnp.bfloat16

D_MODEL = 2048
BATCH = 8
SEQ = 2048
DEPTH = 2
TOKENS = BATCH * SEQ
CHUNK = 128
N_CHUNKS = SEQ // CHUNK

POOL_GROUPS = 4
POOL_WINDOWS = (2, 4, 8, 16)
POOL_GDIM = D_MODEL // POOL_GROUPS
POOL_HALO = 16

SSM_HEAD_DIM = 64
SSM_HEADS = D_MODEL // SSM_HEAD_DIM
SSM_GROUPS = 4
SSM_HEADS_PER_GROUP = SSM_HEADS // SSM_GROUPS
SSM_STATE = 128
SSM_CONV = 4
SSM_BC = SSM_GROUPS * SSM_STATE
SSM_GROUP_WIDTH = D_MODEL // SSM_GROUPS
CONV_HALO = 8

MLSTM_HEADS = 8
MLSTM_HEAD_DIM = D_MODEL // MLSTM_HEADS

N_BRANCH = 3
ALPHA = (2 * DEPTH) ** 0.25
EPS = 1e-5

IN_SIZES = (D_MODEL, D_MODEL, D_MODEL, D_MODEL, SSM_BC, SSM_BC, SSM_HEADS,
            D_MODEL, D_MODEL, D_MODEL, D_MODEL, D_MODEL, MLSTM_HEADS, MLSTM_HEADS,
            N_BRANCH * D_MODEL)

SEG_PU, SEG_PZ, SEG_SX, SEG_SZ, SEG_MQ, SEG_MK, SEG_MV, SEG_MO, SEG_MZ, SEG_G0 = range(10)
N_WIDE_SEGS = 12
BIG_WIDTH = N_WIDE_SEGS * D_MODEL + 2 * SSM_BC
SEG_SB = N_WIDE_SEGS * D_MODEL // SSM_BC
SEG_SC = SEG_SB + 1
SMALL_WIDTH = 128
LANE_DT = 0
LANE_IG = SSM_HEADS
LANE_FG = SSM_HEADS + MLSTM_HEADS

VMEM_LIMIT = 56 * 1024 * 1024


def _params(n_axes):
    return pltpu.CompilerParams(dimension_semantics=("arbitrary",) * n_axes,
                                vmem_limit_bytes=VMEM_LIMIT)


def _sigmoid(v):
    return jax.nn.sigmoid(v)


def _silu(v):
    return v * jax.nn.sigmoid(v)


def _split3(v):
    hi = v.astype(BF16)
    r1 = v - hi.astype(F32)
    mid = r1.astype(BF16)
    lo = (r1 - mid.astype(F32)).astype(BF16)
    return hi, mid, lo


def _dot(a, b):
    return jnp.dot(a, b, preferred_element_type=F32)


def _exact_dot_right01(v, mat01):
    hi, mid, lo = _split3(v)
    return _dot(hi, mat01) + _dot(mid, mat01) + _dot(lo, mat01)


def _exact_dot_left01(mat01, v):
    hi, mid, lo = _split3(v)
    return _dot(mat01, hi) + _dot(mat01, mid) + _dot(mat01, lo)


def _tril_mask(n):
    r = lax.broadcasted_iota(jnp.int32, (n, n), 0)
    c = lax.broadcasted_iota(jnp.int32, (n, n), 1)
    return c <= r


def _matmul_kernel(x_ref, w_ref, o_ref):
    o_ref[...] = _dot(x_ref[...], w_ref[...]).astype(o_ref.dtype)


def _matmul(x, w, out_dtype, tm, tn, name):
    m, k = x.shape
    n = w.shape[1]
    return pl.pallas_call(
        _matmul_kernel,
        grid=(m // tm, n // tn),
        in_specs=[pl.BlockSpec((tm, k), lambda i, j: (i, 0)),
                  pl.BlockSpec((k, tn), lambda i, j: (0, j))],
        out_specs=pl.BlockSpec((tm, tn), lambda i, j: (i, j)),
        out_shape=jax.ShapeDtypeStruct((m, n), out_dtype),
        compiler_params=_params(2),
        name=name,
    )(x, w)


POOL_TM = 512


def _pool_kernel(u_ref, z_ref, w_ref, scale_ref, o_ref, ext_ref):
    t = pl.program_id(1)
    tm = POOL_TM

    @pl.when(t == 0)
    def _():
        ext_ref[0:POOL_HALO, :] = jnp.zeros((POOL_HALO, D_MODEL), F32)

    @pl.when(t != 0)
    def _():
        ext_ref[0:POOL_HALO, :] = ext_ref[tm:tm + POOL_HALO, :]

    ext_ref[POOL_HALO:POOL_HALO + tm, :] = u_ref[...].astype(F32)

    pos = t * tm + lax.broadcasted_iota(jnp.int32, (tm, 1), 0)
    for g, win in enumerate(POOL_WINDOWS):
        cols = slice(g * POOL_GDIM, (g + 1) * POOL_GDIM)
        cur = ext_ref[POOL_HALO:POOL_HALO + tm, cols]
        win_sum = cur
        for j in range(1, win):
            win_sum = win_sum + ext_ref[POOL_HALO - j:POOL_HALO - j + tm, cols]
        cnt = jnp.minimum(pos + 1, win).astype(F32)
        pooled = win_sum / cnt - cur
        mixed = _dot(pooled.astype(BF16), w_ref[g])
        y = mixed * scale_ref[:, cols] * _silu(z_ref[:, cols].astype(F32))
        o_ref[:, cols] = y.astype(o_ref.dtype)


def _pool_branch(proj, w_pool, pool_scale):
    nt = SEQ // POOL_TM
    return pl.pallas_call(
        _pool_kernel,
        grid=(BATCH, nt),
        in_specs=[pl.BlockSpec((POOL_TM, D_MODEL), lambda b, t: (b * nt + t, SEG_PU)),
                  pl.BlockSpec((POOL_TM, D_MODEL), lambda b, t: (b * nt + t, SEG_PZ)),
                  pl.BlockSpec((POOL_GROUPS, POOL_GDIM, POOL_GDIM), lambda b, t: (0, 0, 0)),
                  pl.BlockSpec((1, D_MODEL), lambda b, t: (0, 0))],
        out_specs=pl.BlockSpec((POOL_TM, D_MODEL), lambda b, t: (b * nt + t, 0)),
        out_shape=jax.ShapeDtypeStruct((TOKENS, D_MODEL), BF16),
        scratch_shapes=[pltpu.VMEM((POOL_HALO + POOL_TM, D_MODEL), F32)],
        compiler_params=_params(2),
        name="pool_mixer",
    )(proj, proj, w_pool, pool_scale)


def _conv_silu(ext_ref, new_ref, w_ref, b_ref, c):
    width = ext_ref.shape[1]

    @pl.when(c == 0)
    def _():
        ext_ref[0:CONV_HALO, :] = jnp.zeros((CONV_HALO, width), F32)

    @pl.when(c != 0)
    def _():
        ext_ref[0:CONV_HALO, :] = ext_ref[CHUNK:CHUNK + CONV_HALO, :]

    ext_ref[CONV_HALO:CONV_HALO + CHUNK, :] = new_ref[...].astype(F32)
    acc = b_ref[...]
    for j in range(SSM_CONV):
        off = CONV_HALO - (SSM_CONV - 1) + j
        acc = acc + w_ref[j:j + 1, :] * ext_ref[off:off + CHUNK, :]
    return _silu(acc)


def _ssd_kernel(xs_ref, z_ref, b_ref, c_ref, small_ref,
                cwx_ref, cwb_ref, cwc_ref, cbx_ref, cbb_ref, cbc_ref,
                dtb_ref, alog_ref, dskip_ref, normw_ref, expand_ref,
                o_ref, extx_ref, extb_ref, extc_ref, state_ref):
    c = pl.program_id(1)

    @pl.when(c == 0)
    def _():
        state_ref[...] = jnp.zeros(state_ref.shape, F32)

    xc = _conv_silu(extx_ref, xs_ref, cwx_ref, cbx_ref, c)
    bc = _conv_silu(extb_ref, b_ref, cwb_ref, cbb_ref, c)
    cc = _conv_silu(extc_ref, c_ref, cwc_ref, cbc_ref, c)

    lane = lax.broadcasted_iota(jnp.int32, (CHUNK, SMALL_WIDTH), 1)
    head_lane = lane < SSM_HEADS
    dt = jnp.where(head_lane, jax.nn.softplus(small_ref[...] + dtb_ref[...]), 0.0)
    a = dt * (-jnp.exp(alog_ref[...]))
    tril = _tril_mask(CHUNK)
    tri01 = tril.astype(BF16)
    a_cs = _exact_dot_left01(tri01, a)
    a_cs_t = a_cs.T
    a_last = a_cs[CHUNK - 1:CHUNK, :]
    decay = jnp.where(head_lane, jnp.exp(a_last - a_cs), 0.0)
    exp_cs = jnp.where(head_lane, jnp.exp(a_cs), 0.0)

    expand01 = expand_ref[...]
    dt_e = _exact_dot_right01(dt, expand01)
    decay_e = _exact_dot_right01(decay, expand01)
    exp_cs_e = _exact_dot_right01(exp_cs, expand01)
    chunk_decay_e = exp_cs_e[CHUNK - 1:CHUNK, :]

    xdt = xc * dt_e
    xdt_bf = xdt.astype(BF16)
    xdecay_bf = (xdt * decay_e).astype(BF16)
    bc_bf = bc.astype(BF16)
    cc_bf = cc.astype(BF16)

    for g in range(SSM_GROUPS):
        ncols = slice(g * SSM_STATE, (g + 1) * SSM_STATE)
        wcols = slice(g * SSM_GROUP_WIDTH, (g + 1) * SSM_GROUP_WIDTH)
        bg = bc_bf[:, ncols]
        cg = cc_bf[:, ncols]
        cb = lax.dot_general(cg, bg, (((1,), (1,)), ((), ())), preferred_element_type=F32)
        prev = state_ref[g]
        y_off = _dot(cg, prev.astype(BF16)) * exp_cs_e[:, wcols]
        st_new = lax.dot_general(bg, xdecay_bf[:, wcols], (((0,), (0,)), ((), ())),
                                 preferred_element_type=F32)
        state_ref[g] = prev * chunk_decay_e[:, wcols] + st_new
        y_heads = []
        for r in range(SSM_HEADS_PER_GROUP):
            h = g * SSM_HEADS_PER_GROUP + r
            seg = a_cs[:, h:h + 1] - a_cs_t[h:h + 1, :]
            lmat = jnp.exp(jnp.where(tril, seg, -jnp.inf))
            hcols = slice(h * SSM_HEAD_DIM, (h + 1) * SSM_HEAD_DIM)
            y_heads.append(_dot((cb * lmat).astype(BF16), xdt_bf[:, hcols]))
        y = jnp.concatenate(y_heads, axis=1) + y_off
        y = y + xc[:, wcols] * dskip_ref[:, wcols]
        y = y * _silu(z_ref[:, wcols].astype(F32))
        y = y * lax.rsqrt(jnp.mean(y * y, axis=-1, keepdims=True) + EPS)
        o_ref[:, wcols] = (y * normw_ref[:, wcols]).astype(o_ref.dtype)


def _ssd_branch(proj, small, conv_w, conv_b, dt_bias, a_log, d_skip, norm_w, expand01):
    nc = N_CHUNKS
    row = lambda b, c: b * nc + c
    const2 = lambda b, c: (0, 0)
    cwx, cwb, cwc = conv_w[:, :D_MODEL], conv_w[:, D_MODEL:D_MODEL + SSM_BC], conv_w[:, D_MODEL + SSM_BC:]
    cbx, cbb, cbc = conv_b[:, :D_MODEL], conv_b[:, D_MODEL:D_MODEL + SSM_BC], conv_b[:, D_MODEL + SSM_BC:]
    return pl.pallas_call(
        _ssd_kernel,
        grid=(BATCH, nc),
        in_specs=[pl.BlockSpec((CHUNK, D_MODEL), lambda b, c: (row(b, c), SEG_SX)),
                  pl.BlockSpec((CHUNK, D_MODEL), lambda b, c: (row(b, c), SEG_SZ)),
                  pl.BlockSpec((CHUNK, SSM_BC), lambda b, c: (row(b, c), SEG_SB)),
                  pl.BlockSpec((CHUNK, SSM_BC), lambda b, c: (row(b, c), SEG_SC)),
                  pl.BlockSpec((CHUNK, SMALL_WIDTH), lambda b, c: (row(b, c), 0)),
                  pl.BlockSpec((SSM_CONV, D_MODEL), const2),
                  pl.BlockSpec((SSM_CONV, SSM_BC), const2),
                  pl.BlockSpec((SSM_CONV, SSM_BC), const2),
                  pl.BlockSpec((1, D_MODEL), const2),
                  pl.BlockSpec((1, SSM_BC), const2),
                  pl.BlockSpec((1, SSM_BC), const2),
                  pl.BlockSpec((1, SMALL_WIDTH), const2),
                  pl.BlockSpec((1, SMALL_WIDTH), const2),
                  pl.BlockSpec((1, D_MODEL), const2),
                  pl.BlockSpec((1, D_MODEL), const2),
                  pl.BlockSpec((SMALL_WIDTH, D_MODEL), const2)],
        out_specs=pl.BlockSpec((CHUNK, D_MODEL), lambda b, c: (row(b, c), 0)),
        out_shape=jax.ShapeDtypeStruct((TOKENS, D_MODEL), BF16),
        scratch_shapes=[pltpu.VMEM((CONV_HALO + CHUNK, D_MODEL), F32),
                        pltpu.VMEM((CONV_HALO + CHUNK, SSM_BC), F32),
                        pltpu.VMEM((CONV_HALO + CHUNK, SSM_BC), F32),
                        pltpu.VMEM((SSM_GROUPS, SSM_STATE, SSM_GROUP_WIDTH), F32)],
        compiler_params=_params(2),
        name="ssd_mixer",
    )(proj, proj, proj, proj, small, cwx, cwb, cwc, cbx, cbb, cbc,
      dt_bias, a_log, d_skip, norm_w, expand01)


def _mlstm_kernel(q_ref, k_ref, v_ref, og_ref, z_ref, small_ref, ib_ref, fb_ref, normw_ref,
                  o_ref, cstate_ref, nstate_ref, mstate_ref):
    c = pl.program_id(1)

    @pl.when(c == 0)
    def _():
        cstate_ref[...] = jnp.zeros(cstate_ref.shape, F32)
        nstate_ref[...] = jnp.zeros(nstate_ref.shape, F32)
        mstate_ref[...] = jnp.zeros(mstate_ref.shape, F32)

    lane = lax.broadcasted_iota(jnp.int32, (CHUNK, SMALL_WIDTH), 1)
    fg_lane = (lane >= LANE_FG) & (lane < LANE_FG + MLSTM_HEADS)
    small = small_ref[...]
    ig_all = small + ib_ref[...]
    lf_all = jnp.where(fg_lane, jax.nn.log_sigmoid(small + fb_ref[...]), 0.0)
    tril = _tril_mask(CHUNK)
    bcum_all = _exact_dot_left01(tril.astype(BF16), lf_all)
    bcum_t = bcum_all.T
    ig_t = ig_all.T
    k_scale = MLSTM_HEAD_DIM ** -0.5

    for h in range(MLSTM_HEADS):
        cols = slice(h * MLSTM_HEAD_DIM, (h + 1) * MLSTM_HEAD_DIM)
        b_col = bcum_all[:, LANE_FG + h:LANE_FG + h + 1]
        i_col = ig_all[:, LANE_IG + h:LANE_IG + h + 1]
        b_row = bcum_t[LANE_FG + h:LANE_FG + h + 1, :]
        i_row = ig_t[LANE_IG + h:LANE_IG + h + 1, :]
        b_last = b_col[CHUNK - 1:CHUNK, :]
        m_prev = mstate_ref[h:h + 1, 0:1]
        c_prev = cstate_ref[h]
        n_prev = nstate_ref[h:h + 1, :]

        qh = q_ref[:, cols]
        kh = k_ref[:, cols]
        vh = v_ref[:, cols]

        dlog = jnp.where(tril, b_col - b_row + i_row, -jnp.inf)
        inter_log = b_col + m_prev
        m_t = jnp.maximum(jnp.max(dlog, axis=-1, keepdims=True), inter_log)
        dw = jnp.exp(dlog - m_t)
        inter_w = jnp.exp(inter_log - m_t)
        s = lax.dot_general(qh, kh, (((1,), (1,)), ((), ())), preferred_element_type=F32) * k_scale
        qk = s * dw
        num = _dot(qk.astype(BF16), vh) + inter_w * _dot(qh, c_prev.astype(BF16))
        den = (jnp.sum(qk, axis=-1, keepdims=True)
               + inter_w * jnp.sum(qh.astype(F32) * n_prev, axis=-1, keepdims=True))
        hcell = num / jnp.maximum(jnp.abs(den), jnp.exp(-m_t))

        w_log = b_last - b_col + i_col
        m_loc = jnp.max(w_log, axis=0, keepdims=True)
        wk = jnp.exp(w_log - m_loc) * (kh.astype(F32) * k_scale)
        c_loc = lax.dot_general(wk.astype(BF16), vh, (((0,), (0,)), ((), ())),
                                preferred_element_type=F32)
        n_loc = jnp.sum(wk, axis=0, keepdims=True)
        m_new = jnp.maximum(b_last + m_prev, m_loc)
        s_old = jnp.exp(b_last + m_prev - m_new)
        s_loc = jnp.exp(m_loc - m_new)
        cstate_ref[h] = c_prev * s_old + c_loc * s_loc
        nstate_ref[h:h + 1, :] = n_prev * s_old + n_loc * s_loc
        mstate_ref[h:h + 1, :] = jnp.broadcast_to(m_new, (1, SMALL_WIDTH))

        hc = _sigmoid(og_ref[:, cols].astype(F32)) * hcell
        mu = jnp.mean(hc, axis=-1, keepdims=True)
        var = jnp.mean(jnp.square(hc - mu), axis=-1, keepdims=True)
        hc = (hc - mu) * lax.rsqrt(var + EPS)
        hc = hc * normw_ref[:, cols] * _silu(z_ref[:, cols].astype(F32))
        o_ref[:, cols] = hc.astype(o_ref.dtype)


def _mlstm_branch(proj, small, i_bias, f_bias, norm_w):
    nc = N_CHUNKS
    row = lambda b, c: b * nc + c
    const2 = lambda b, c: (0, 0)
    seg = lambda s: pl.BlockSpec((CHUNK, D_MODEL), lambda b, c: (row(b, c), s))
    return pl.pallas_call(
        _mlstm_kernel,
        grid=(BATCH, nc),
        in_specs=[seg(SEG_MQ), seg(SEG_MK), seg(SEG_MV), seg(SEG_MO), seg(SEG_MZ),
                  pl.BlockSpec((CHUNK, SMALL_WIDTH), lambda b, c: (row(b, c), 0)),
                  pl.BlockSpec((1, SMALL_WIDTH), const2),
                  pl.BlockSpec((1, SMALL_WIDTH), const2),
                  pl.BlockSpec((1, D_MODEL), const2)],
        out_specs=pl.BlockSpec((CHUNK, D_MODEL), lambda b, c: (row(b, c), 0)),
        out_shape=jax.ShapeDtypeStruct((TOKENS, D_MODEL), BF16),
        scratch_shapes=[pltpu.VMEM((MLSTM_HEADS, MLSTM_HEAD_DIM, MLSTM_HEAD_DIM), F32),
                        pltpu.VMEM((MLSTM_HEADS, MLSTM_HEAD_DIM), F32),
                        pltpu.VMEM((MLSTM_HEADS, SMALL_WIDTH), F32)],
        compiler_params=_params(2),
        name="mlstm_mixer",
    )(proj, proj, proj, proj, proj, small, i_bias, f_bias, norm_w)


MERGE_TM = 512
MERGE_TN = 1024


def _merge_kernel(y0_ref, y1_ref, y2_ref, g0_ref, g1_ref, g2_ref, bg_ref, w_ref, o_ref):
    acc = None
    for b, (y_ref, g_ref) in enumerate(((y0_ref, g0_ref), (y1_ref, g1_ref), (y2_ref, g2_ref))):
        gate = _sigmoid(g_ref[...].astype(F32) + bg_ref[b:b + 1, :])
        term = gate * _dot(y_ref[...], w_ref[b])
        acc = term if acc is None else acc + term
    o_ref[...] = acc.astype(o_ref.dtype)


def _merge(y_pool, y_ssm, y_mlstm, proj, b_gate, w_branch):
    tm, tn = MERGE_TM, MERGE_TN
    per_seg = D_MODEL // tn
    y_spec = pl.BlockSpec((tm, D_MODEL), lambda i, j: (i, 0))
    gate = lambda b: pl.BlockSpec((tm, tn), lambda i, j: (i, (SEG_G0 + b) * per_seg + j))
    return pl.pallas_call(
        _merge_kernel,
        grid=(TOKENS // tm, D_MODEL // tn),
        in_specs=[y_spec, y_spec, y_spec, gate(0), gate(1), gate(2),
                  pl.BlockSpec((N_BRANCH, tn), lambda i, j: (0, j)),
                  pl.BlockSpec((N_BRANCH, D_MODEL, tn), lambda i, j: (0, 0, j))],
        out_specs=pl.BlockSpec((tm, tn), lambda i, j: (i, j)),
        out_shape=jax.ShapeDtypeStruct((TOKENS, D_MODEL), BF16),
        compiler_params=_params(2),
        name="branch_merge",
    )(y_pool, y_ssm, y_mlstm, proj, proj, proj, b_gate, w_branch)


OUT_TM = 512


def _out_kernel(m_ref, x_ref, w_ref, g_ref, b_ref, o_ref, obf_ref):
    h = ALPHA * x_ref[...] + _dot(m_ref[...], w_ref[...])
    mu = jnp.mean(h, axis=-1, keepdims=True)
    var = jnp.mean(jnp.square(h - mu), axis=-1, keepdims=True)
    y = (h - mu) * lax.rsqrt(var + EPS) * g_ref[...] + b_ref[...]
    o_ref[...] = y
    obf_ref[...] = y.astype(BF16)


def _out_proj(merged, x, w_out, ln_g, ln_b):
    tm = OUT_TM
    row = pl.BlockSpec((tm, D_MODEL), lambda i: (i, 0))
    vec = pl.BlockSpec((1, D_MODEL), lambda i: (0, 0))
    return pl.pallas_call(
        _out_kernel,
        grid=(TOKENS // tm,),
        in_specs=[row, row, pl.BlockSpec((D_MODEL, D_MODEL), lambda i: (0, 0)), vec, vec],
        out_specs=[row, row],
        out_shape=[jax.ShapeDtypeStruct((TOKENS, D_MODEL), F32),
                   jax.ShapeDtypeStruct((TOKENS, D_MODEL), BF16)],
        compiler_params=_params(1),
        name="out_proj_ln",
    )(merged, x, w_out, ln_g, ln_b)


def _pack_in_weights(w_in):
    pieces, off = [], 0
    for n in IN_SIZES:
        pieces.append(w_in[:, off:off + n])
        off += n
    (pu, pz, sx, sz, sb, sc, sdt, mq, mk, mv, mo, mz, mi, mf, gates) = pieces
    big = jnp.concatenate([pu, pz, sx, sz, mq, mk, mv, mo, mz, gates, sb, sc], axis=1)
    pad = jnp.zeros((D_MODEL, SMALL_WIDTH - SSM_HEADS - 2 * MLSTM_HEADS), w_in.dtype)
    small = jnp.concatenate([sdt, mi, mf, pad], axis=1)
    return big.astype(BF16), small.astype(BF16)


def _lane_row(vec, start):
    return jnp.zeros((1, SMALL_WIDTH), F32).at[0, start:start + vec.shape[0]].set(vec.astype(F32))


def _layer(x, x_bf, w_in, b_gate, w_pool, pool_scale, conv_w, conv_b, dt_bias, a_log, d_skip,
           ssm_norm_w, i_bias, f_bias, mlstm_norm_w, w_branch, w_out, ln_g, ln_b, expand01):
    w_big, w_small = _pack_in_weights(w_in)
    proj = _matmul(x_bf, w_big, BF16, 1024, 1024, "in_proj")
    small = _matmul(x_bf, w_small, F32, 1024, SMALL_WIDTH, "in_proj_gates")

    y_pool = _pool_branch(proj, w_pool.astype(BF16), pool_scale.reshape(1, D_MODEL))
    y_ssm = _ssd_branch(proj, small, conv_w, conv_b.reshape(1, -1),
                        _lane_row(dt_bias, LANE_DT), _lane_row(a_log, LANE_DT),
                        jnp.repeat(d_skip, SSM_HEAD_DIM).reshape(1, D_MODEL),
                        ssm_norm_w.reshape(1, D_MODEL), expand01)
    y_mlstm = _mlstm_branch(proj, small, _lane_row(i_bias, LANE_IG), _lane_row(f_bias, LANE_FG),
                            mlstm_norm_w.reshape(1, D_MODEL))
    merged = _merge(y_pool, y_ssm, y_mlstm, proj, b_gate, w_branch.astype(BF16))
    return _out_proj(merged, x, w_out.astype(BF16), ln_g.reshape(1, D_MODEL), ln_b.reshape(1, D_MODEL))


def kernel(x, w_in, b_gate, w_pool, pool_scale, conv_w, conv_b, dt_bias, a_log, d_skip,
           ssm_norm_w, i_bias, f_bias, mlstm_norm_w, w_branch, w_out, ln_g, ln_b):
    head = lax.broadcasted_iota(jnp.int32, (SMALL_WIDTH, D_MODEL), 0)
    chan = lax.broadcasted_iota(jnp.int32, (SMALL_WIDTH, D_MODEL), 1)
    expand01 = (chan // SSM_HEAD_DIM == head).astype(BF16)

    h = x.reshape(TOKENS, D_MODEL)
    h_bf = h.astype(BF16)
    for l in range(DEPTH):
        h, h_bf = _layer(h, h_bf, w_in[l], b_gate[l], w_pool[l], pool_scale[l], conv_w[l],
                         conv_b[l], dt_bias[l], a_log[l], d_skip[l], ssm_norm_w[l], i_bias[l],
                         f_bias[l], mlstm_norm_w[l], w_branch[l], w_out[l], ln_g[l], ln_b[l],
                         expand01)
    return h.reshape(BATCH, SEQ, D_MODEL)
```

```python
import jax
import jax.numpy as jnp
from jax import lax
from jax.experimental import pallas as pl
from jax.experimental.pallas import tpu as pltpu

F32 = jnp.float32
BF16 = jnp.bfloat16

D_MODEL = 2048
BATCH = 8
SEQ = 2048
DEPTH = 2
TOKENS = BATCH * SEQ
CHUNK = 128
N_CHUNKS = SEQ // CHUNK
LANES = 128

POOL_GROUPS = 4
POOL_WINDOWS = (2, 4, 8, 16)
POOL_GDIM = D_MODEL // POOL_GROUPS
POOL_HALO = 16

SSM_HEAD_DIM = 64
SSM_HEADS = D_MODEL // SSM_HEAD_DIM
SSM_GROUPS = 4
SSM_HEADS_PER_GROUP = SSM_HEADS // SSM_GROUPS
SSM_STATE = 128
SSM_CONV = 4
SSM_BC = SSM_GROUPS * SSM_STATE
SSM_GROUP_WIDTH = D_MODEL // SSM_GROUPS
CONV_HALO = 8

MLSTM_HEADS = 8
MLSTM_HEAD_DIM = D_MODEL // MLSTM_HEADS
MLSTM_AUG = MLSTM_HEAD_DIM + LANES

N_BRANCH = 3
ALPHA = (2 * DEPTH) ** 0.25
EPS = 1e-5

OFF_SSM_DT = 4 * D_MODEL + 2 * SSM_BC
OFF_MLSTM = OFF_SSM_DT + SSM_HEADS
OFF_MLSTM_GATES = OFF_MLSTM + 5 * D_MODEL
OFF_MERGE_GATES = OFF_MLSTM_GATES + 2 * MLSTM_HEADS
IN_DIM = OFF_MERGE_GATES + N_BRANCH * D_MODEL

SEG_PU, SEG_PZ, SEG_SX, SEG_SZ = range(4)
SEG_SB = 4 * D_MODEL // SSM_BC
SEG_SC = SEG_SB + 1
SEG_MQ, SEG_MK, SEG_MV, SEG_MO, SEG_MZ = range(5)
SMALL_WIDTH = LANES
LANE_DT = 0
LANE_IG = SSM_HEADS
LANE_FG = SSM_HEADS + MLSTM_HEADS

VMEM_LIMIT = 56 * 1024 * 1024


def _params(n_axes):
    return pltpu.CompilerParams(dimension_semantics=("arbitrary",) * n_axes,
                                vmem_limit_bytes=VMEM_LIMIT)


def _sigmoid(v):
    return 0.5 * jnp.tanh(0.5 * v) + 0.5


def _silu(v):
    h = 0.5 * v
    return h * jnp.tanh(h) + h


def _split3(v):
    hi = v.astype(BF16)
    r1 = v - hi.astype(F32)
    mid = r1.astype(BF16)
    lo = (r1 - mid.astype(F32)).astype(BF16)
    return hi, mid, lo


def _dot(a, b):
    return jnp.dot(a, b, preferred_element_type=F32)


def _exact_dot_right01(v, mat01):
    hi, mid, lo = _split3(v)
    return _dot(hi, mat01) + _dot(mid, mat01) + _dot(lo, mat01)


def _tril_mask(n):
    r = lax.broadcasted_iota(jnp.int32, (n, n), 0)
    c = lax.broadcasted_iota(jnp.int32, (n, n), 1)
    return c <= r


def _matmul_kernel(x_ref, w_ref, o_ref):
    o_ref[...] = _dot(x_ref[...], w_ref[...]).astype(o_ref.dtype)


def _matmul_cast_kernel(x_ref, w_ref, o_ref, xbf_ref):
    @pl.when(pl.program_id(1) == 0)
    def _():
        xbf_ref[...] = x_ref[...].astype(BF16)

    o_ref[...] = _dot(xbf_ref[...], w_ref[...]).astype(o_ref.dtype)


def _matmul(x, w, out_dtype, tm, tn, name):
    m, k = x.shape
    n = w.shape[1]
    cast = x.dtype != BF16
    return pl.pallas_call(
        _matmul_cast_kernel if cast else _matmul_kernel,
        grid=(m // tm, n // tn),
        in_specs=[pl.BlockSpec((tm, k), lambda i, j: (i, 0)),
                  pl.BlockSpec((k, tn), lambda i, j: (0, j))],
        out_specs=pl.BlockSpec((tm, tn), lambda i, j: (i, j)),
        out_shape=jax.ShapeDtypeStruct((m, n), out_dtype),
        scratch_shapes=[pltpu.VMEM((tm, k), BF16)] if cast else [],
        compiler_params=_params(2),
        name=name,
    )(x, w)


POOL_TM = 512


def _pool_kernel(u_ref, z_ref, w_ref, scale_ref, o_ref, ext_ref):
    t = pl.program_id(1)
    tm = POOL_TM

    @pl.when(t == 0)
    def _():
        ext_ref[0:POOL_HALO, :] = jnp.zeros((POOL_HALO, D_MODEL), F32)

    @pl.when(t != 0)
    def _():
        ext_ref[0:POOL_HALO, :] = ext_ref[tm:tm + POOL_HALO, :]

    ext_ref[POOL_HALO:POOL_HALO + tm, :] = u_ref[...].astype(F32)

    pos = t * tm + lax.broadcasted_iota(jnp.int32, (tm, 1), 0)
    for g, win in enumerate(POOL_WINDOWS):
        cols = slice(g * POOL_GDIM, (g + 1) * POOL_GDIM)
        cur = ext_ref[POOL_HALO:POOL_HALO + tm, cols]
        win_sum = cur
        for j in range(1, win):
            win_sum = win_sum + ext_ref[POOL_HALO - j:POOL_HALO - j + tm, cols]
        cnt = jnp.minimum(pos + 1, win).astype(F32)
        pooled = win_sum / cnt - cur
        mixed = _dot(pooled.astype(BF16), w_ref[g])
        y = mixed * scale_ref[:, cols] * _silu(z_ref[:, cols].astype(F32))
        o_ref[:, cols] = y.astype(o_ref.dtype)


def _pool_branch(proj_a, w_pool, pool_scale):
    nt = SEQ // POOL_TM
    return pl.pallas_call(
        _pool_kernel,
        grid=(BATCH, nt),
        in_specs=[pl.BlockSpec((POOL_TM, D_MODEL), lambda b, t: (b * nt + t, SEG_PU)),
                  pl.BlockSpec((POOL_TM, D_MODEL), lambda b, t: (b * nt + t, SEG_PZ)),
                  pl.BlockSpec((POOL_GROUPS, POOL_GDIM, POOL_GDIM), lambda b, t: (0, 0, 0)),
                  pl.BlockSpec((1, D_MODEL), lambda b, t: (0, 0))],
        out_specs=pl.BlockSpec((POOL_TM, D_MODEL), lambda b, t: (b * nt + t, 0)),
        out_shape=jax.ShapeDtypeStruct((TOKENS, D_MODEL), BF16),
        scratch_shapes=[pltpu.VMEM((POOL_HALO + POOL_TM, D_MODEL), F32)],
        compiler_params=_params(2),
        name="pool_mixer",
    )(proj_a, proj_a, w_pool, pool_scale)


SSD_COL_DT, SSD_COL_DTDECAY, SSD_COL_EXPCS, SSD_COL_CS = (k * SSM_HEADS for k in range(4))
ML_COL_C1, ML_COL_CLAMP = 0, MLSTM_HEADS
ML_ROW_G, ML_ROW_WGT = 0, MLSTM_HEADS
ML_CHUNK_MPREV, ML_CHUNK_SOLD = 0, MLSTM_HEADS


def _gate_scan_kernel(small_ref, dtb_ref, alog_ref, gbias_ref,
                      ssd_cols_ref, ssd_rows_ref, ml_cols_ref, ml_rows_ref, ml_chunk_ref):
    nh = MLSTM_HEADS
    r = lax.broadcasted_iota(jnp.int32, (CHUNK, CHUNK), 0)
    cidx = lax.broadcasted_iota(jnp.int32, (CHUNK, CHUNK), 1)
    triu01 = (r <= cidx).astype(BF16)
    lane = lax.broadcasted_iota(jnp.int32, (nh, CHUNK), 1)
    a_coef = -jnp.exp(alog_ref[...])
    m_prev = jnp.zeros((nh, LANES), F32)
    ml_pad = jnp.zeros((CHUNK - 2 * nh, CHUNK), F32)

    for c in range(N_CHUNKS):
        tok = slice(c * CHUNK, (c + 1) * CHUNK)
        small_t = small_ref[tok, :].T

        dt = jax.nn.softplus(small_t[LANE_DT:LANE_DT + SSM_HEADS, :] + dtb_ref[...])
        a_cs = _exact_dot_right01(dt * a_coef, triu01)
        a_last = a_cs[:, CHUNK - 1:CHUNK]
        decay = jnp.exp(a_last - a_cs)
        ssd_rows_ref[0, :, tok] = a_cs
        ssd_cols_ref[tok, :] = jnp.concatenate([dt, dt * decay, jnp.exp(a_cs), a_cs], axis=0).T

        pre = small_t[LANE_IG:LANE_IG + 2 * nh, :] + gbias_ref[...]
        cum = _exact_dot_right01(jax.nn.log_sigmoid(pre), triu01)
        ig = pre[0:nh, :]
        bcum = cum[nh:2 * nh, :]
        g = ig - bcum
        b_last = bcum[:, CHUNK - 1:CHUNK]
        pmax = g
        shift = 1
        while shift < CHUNK:
            pmax = jnp.maximum(pmax, jnp.where(lane >= shift, pltpu.roll(pmax, shift, axis=1), -jnp.inf))
            shift *= 2
        m_t = jnp.maximum(bcum + pmax, bcum + m_prev)
        w_log = b_last + g
        m_loc = jnp.max(w_log, axis=1, keepdims=True)
        m_new = jnp.maximum(b_last + m_prev, m_loc)
        s_old = jnp.exp(b_last + m_prev - m_new)
        s_loc = jnp.exp(m_loc - m_new)
        wgt = jnp.exp(w_log - m_loc) * (s_loc * MLSTM_HEAD_DIM ** -0.5)
        ml_rows_ref[0, :, tok] = jnp.concatenate([g, wgt], axis=0)
        ml_chunk_ref[0, c] = jnp.concatenate([m_prev, s_old], axis=0)
        ml_cols_ref[tok, :] = jnp.concatenate([bcum - m_t, jnp.exp(-m_t), ml_pad], axis=0).T
        m_prev = m_new


def _gate_scan(small, dt_bias, a_log, gate_bias):
    const2 = lambda b: (0, 0)
    return pl.pallas_call(
        _gate_scan_kernel,
        grid=(BATCH,),
        in_specs=[pl.BlockSpec((SEQ, SMALL_WIDTH), lambda b: (b, 0)),
                  pl.BlockSpec((SSM_HEADS, LANES), const2),
                  pl.BlockSpec((SSM_HEADS, LANES), const2),
                  pl.BlockSpec((2 * MLSTM_HEADS, LANES), const2)],
        out_specs=[pl.BlockSpec((SEQ, LANES), lambda b: (b, 0)),
                   pl.BlockSpec((1, SSM_HEADS, SEQ), lambda b: (b, 0, 0)),
                   pl.BlockSpec((SEQ, LANES), lambda b: (b, 0)),
                   pl.BlockSpec((1, 2 * MLSTM_HEADS, SEQ), lambda b: (b, 0, 0)),
                   pl.BlockSpec((1, N_CHUNKS, 2 * MLSTM_HEADS, LANES), lambda b: (b, 0, 0, 0))],
        out_shape=[jax.ShapeDtypeStruct((TOKENS, LANES), F32),
                   jax.ShapeDtypeStruct((BATCH, SSM_HEADS, SEQ), F32),
                   jax.ShapeDtypeStruct((TOKENS, LANES), F32),
                   jax.ShapeDtypeStruct((BATCH, 2 * MLSTM_HEADS, SEQ), F32),
                   jax.ShapeDtypeStruct((BATCH, N_CHUNKS, 2 * MLSTM_HEADS, LANES), F32)],
        compiler_params=_params(1),
        name="gate_scan",
    )(small, dt_bias, a_log, gate_bias)


def _conv_silu(ext_ref, new_ref, w_ref, b_ref, c):
    width = ext_ref.shape[1]

    @pl.when(c == 0)
    def _():
        ext_ref[0:CONV_HALO, :] = jnp.zeros((CONV_HALO, width), F32)

    @pl.when(c != 0)
    def _():
        ext_ref[0:CONV_HALO, :] = ext_ref[CHUNK:CHUNK + CONV_HALO, :]

    ext_ref[CONV_HALO:CONV_HALO + CHUNK, :] = new_ref[...].astype(F32)
    acc = b_ref[...]
    for j in range(SSM_CONV):
        off = CONV_HALO - (SSM_CONV - 1) + j
        acc = acc + w_ref[j:j + 1, :] * ext_ref[off:off + CHUNK, :]
    return _silu(acc)


def _expand_heads(cols_hi, cols_mid, expand01):
    return _dot(cols_hi, expand01) + _dot(cols_mid, expand01)


def _ssd_kernel(xs_ref, z_ref, b_ref, c_ref, cols_ref, rows_ref,
                cwx_ref, cwb_ref, cwc_ref, cbx_ref, cbb_ref, cbc_ref,
                dskip_ref, normw_ref, expand_ref,
                o_ref, extx_ref, extb_ref, extc_ref, state_ref):
    c = pl.program_id(1)

    @pl.when(c == 0)
    def _():
        state_ref[...] = jnp.zeros(state_ref.shape, F32)

    xc = _conv_silu(extx_ref, xs_ref, cwx_ref, cbx_ref, c)
    bc = _conv_silu(extb_ref, b_ref, cwb_ref, cbb_ref, c)
    cc = _conv_silu(extc_ref, c_ref, cwc_ref, cbc_ref, c)

    cols = cols_ref[...]
    cols_hi = cols.astype(BF16)
    cols_mid = (cols - cols_hi.astype(F32)).astype(BF16)
    dt_e = _expand_heads(cols_hi, cols_mid, expand_ref[0])
    dtdecay_e = _expand_heads(cols_hi, cols_mid, expand_ref[1])
    exp_cs_e = _expand_heads(cols_hi, cols_mid, expand_ref[2])
    chunk_decay_e = exp_cs_e[CHUNK - 1:CHUNK, :]
    a_cs_rows = rows_ref[0]
    tril = _tril_mask(CHUNK)

    xdt_bf = (xc * dt_e).astype(BF16)
    xdecay_bf = (xc * dtdecay_e).astype(BF16)
    bc_bf = bc.astype(BF16)
    cc_bf = cc.astype(BF16)

    for g in range(SSM_GROUPS):
        ncols = slice(g * SSM_STATE, (g + 1) * SSM_STATE)
        wcols = slice(g * SSM_GROUP_WIDTH, (g + 1) * SSM_GROUP_WIDTH)
        bg = bc_bf[:, ncols]
        cg = cc_bf[:, ncols]
        cb = lax.dot_general(cg, bg, (((1,), (1,)), ((), ())), preferred_element_type=F32)
        prev = state_ref[g]
        y_off = _dot(cg, prev.astype(BF16)) * exp_cs_e[:, wcols]
        st_new = lax.dot_general(bg, xdecay_bf[:, wcols], (((0,), (0,)), ((), ())),
                                 preferred_element_type=F32)
        state_ref[g] = prev * chunk_decay_e[:, wcols] + st_new
        y_heads = []
        for r in range(SSM_HEADS_PER_GROUP):
            h = g * SSM_HEADS_PER_GROUP + r
            seg = cols[:, SSD_COL_CS + h:SSD_COL_CS + h + 1] - a_cs_rows[h:h + 1, :]
            lmat = jnp.exp(jnp.where(tril, seg, -jnp.inf))
            hcols = slice(h * SSM_HEAD_DIM, (h + 1) * SSM_HEAD_DIM)
            y_heads.append(_dot((cb * lmat).astype(BF16), xdt_bf[:, hcols]))
        y = jnp.concatenate(y_heads, axis=1) + y_off
        y = y + xc[:, wcols] * dskip_ref[:, wcols]
        y = y * _silu(z_ref[:, wcols].astype(F32))
        y = y * lax.rsqrt(jnp.mean(y * y, axis=-1, keepdims=True) + EPS)
        o_ref[:, wcols] = (y * normw_ref[:, wcols]).astype(o_ref.dtype)


def _ssd_branch(proj_a, ssd_cols, ssd_rows, conv_w, conv_b, d_skip, norm_w, expand01):
    nc = N_CHUNKS
    row = lambda b, c: b * nc + c
    const2 = lambda b, c: (0, 0)
    cwx, cwb, cwc = conv_w[:, :D_MODEL], conv_w[:, D_MODEL:D_MODEL + SSM_BC], conv_w[:, D_MODEL + SSM_BC:]
    cbx, cbb, cbc = conv_b[:, :D_MODEL], conv_b[:, D_MODEL:D_MODEL + SSM_BC], conv_b[:, D_MODEL + SSM_BC:]
    return pl.pallas_call(
        _ssd_kernel,
        grid=(BATCH, nc),
        in_specs=[pl.BlockSpec((CHUNK, D_MODEL), lambda b, c: (row(b, c), SEG_SX)),
                  pl.BlockSpec((CHUNK, D_MODEL), lambda b, c: (row(b, c), SEG_SZ)),
                  pl.BlockSpec((CHUNK, SSM_BC), lambda b, c: (row(b, c), SEG_SB)),
                  pl.BlockSpec((CHUNK, SSM_BC), lambda b, c: (row(b, c), SEG_SC)),
                  pl.BlockSpec((CHUNK, LANES), lambda b, c: (row(b, c), 0)),
                  pl.BlockSpec((1, SSM_HEADS, CHUNK), lambda b, c: (b, 0, c)),
                  pl.BlockSpec((SSM_CONV, D_MODEL), const2),
                  pl.BlockSpec((SSM_CONV, SSM_BC), const2),
                  pl.BlockSpec((SSM_CONV, SSM_BC), const2),
                  pl.BlockSpec((1, D_MODEL), const2),
                  pl.BlockSpec((1, SSM_BC), const2),
                  pl.BlockSpec((1, SSM_BC), const2),
                  pl.BlockSpec((1, D_MODEL), const2),
                  pl.BlockSpec((1, D_MODEL), const2),
                  pl.BlockSpec((3, LANES, D_MODEL), lambda b, c: (0, 0, 0))],
        out_specs=pl.BlockSpec((CHUNK, D_MODEL), lambda b, c: (row(b, c), 0)),
        out_shape=jax.ShapeDtypeStruct((TOKENS, D_MODEL), BF16),
        scratch_shapes=[pltpu.VMEM((CONV_HALO + CHUNK, D_MODEL), F32),
                        pltpu.VMEM((CONV_HALO + CHUNK, SSM_BC), F32),
                        pltpu.VMEM((CONV_HALO + CHUNK, SSM_BC), F32),
                        pltpu.VMEM((SSM_GROUPS, SSM_STATE, SSM_GROUP_WIDTH), F32)],
        compiler_params=_params(2),
        name="ssd_mixer",
    )(proj_a, proj_a, proj_a, proj_a, ssd_cols, ssd_rows, cwx, cwb, cwc, cbx, cbb, cbc,
      d_skip, norm_w, expand01)


def _mlstm_kernel(q_ref, k_ref, v_ref, og_ref, z_ref, cols_ref, rows_ref, chunk_ref, normw_ref,
                  o_ref, cstate_ref):
    c = pl.program_id(1)
    nh = MLSTM_HEADS
    n_tiles = MLSTM_AUG // LANES
    v_tiles = MLSTM_HEAD_DIM // LANES

    @pl.when(c == 0)
    def _():
        cstate_ref[...] = jnp.zeros(cstate_ref.shape, F32)

    cols_t = cols_ref[...]
    rows = rows_ref[0]
    chunk = chunk_ref[0, 0]
    tril = _tril_mask(CHUNK)
    ones_tile = jnp.ones((CHUNK, LANES), BF16)
    mean_mat = jnp.full((MLSTM_HEAD_DIM, LANES), 1.0 / MLSTM_HEAD_DIM, BF16)
    k_scale = MLSTM_HEAD_DIM ** -0.5

    for h in range(nh):
        cols = slice(h * MLSTM_HEAD_DIM, (h + 1) * MLSTM_HEAD_DIM)
        qh = q_ref[:, cols]
        k_t = k_ref[:, cols].T
        v_aug = jnp.concatenate([v_ref[:, cols], ones_tile], axis=1)
        c_prev = cstate_ref[h]
        g_row = rows[ML_ROW_G + h:ML_ROW_G + h + 1, :]
        wgt_row = rows[ML_ROW_WGT + h:ML_ROW_WGT + h + 1, :]
        m_prev_row = chunk[ML_CHUNK_MPREV + h:ML_CHUNK_MPREV + h + 1, :]
        s_old_row = chunk[ML_CHUNK_SOLD + h:ML_CHUNK_SOLD + h + 1, :]

        c1_d = jnp.broadcast_to(cols_t[:, ML_COL_C1 + h:ML_COL_C1 + h + 1], (CHUNK, CHUNK))
        clamp_d = jnp.broadcast_to(cols_t[:, ML_COL_CLAMP + h:ML_COL_CLAMP + h + 1], (CHUNK, CHUNK))
        dw = jnp.exp(jnp.where(tril, c1_d + g_row, -jnp.inf))
        inter_w = jnp.exp(c1_d + m_prev_row)

        s = _dot(qh, k_t) * k_scale
        intra = _dot((s * dw).astype(BF16), v_aug)
        inter = _dot(qh, c_prev.astype(BF16))
        nd = [intra[:, j * LANES:(j + 1) * LANES] + inter_w * inter[:, j * LANES:(j + 1) * LANES]
              for j in range(n_tiles)]
        rden = 1.0 / jnp.maximum(jnp.abs(nd[-1]), clamp_d)

        wk_t = (k_t.astype(F32) * wgt_row).astype(BF16)
        cstate_ref[h] = (c_prev * jnp.concatenate([s_old_row] * n_tiles, axis=1)
                         + _dot(wk_t, v_aug))

        og = _sigmoid(og_ref[:, cols].astype(F32))
        hc = jnp.concatenate([nd[j] * rden for j in range(v_tiles)], axis=1) * og
        mu = _dot(hc.astype(BF16), mean_mat)
        dev = hc - jnp.concatenate([mu] * v_tiles, axis=1)
        var = _dot((dev * dev).astype(BF16), mean_mat)
        y = dev * jnp.concatenate([lax.rsqrt(var + EPS)] * v_tiles, axis=1)
        y = y * normw_ref[:, cols] * _silu(z_ref[:, cols].astype(F32))
        o_ref[:, cols] = y.astype(o_ref.dtype)


def _mlstm_branch(proj_m, ml_cols, ml_rows, ml_chunk, norm_w):
    nc = N_CHUNKS
    row = lambda b, c: b * nc + c
    seg = lambda s: pl.BlockSpec((CHUNK, D_MODEL), lambda b, c: (row(b, c), s))
    return pl.pallas_call(
        _mlstm_kernel,
        grid=(BATCH, nc),
        in_specs=[seg(SEG_MQ), seg(SEG_MK), seg(SEG_MV), seg(SEG_MO), seg(SEG_MZ),
                  pl.BlockSpec((CHUNK, LANES), lambda b, c: (row(b, c), 0)),
                  pl.BlockSpec((1, 2 * MLSTM_HEADS, CHUNK), lambda b, c: (b, 0, c)),
                  pl.BlockSpec((1, 1, 2 * MLSTM_HEADS, LANES), lambda b, c: (b, c, 0, 0)),
                  pl.BlockSpec((1, D_MODEL), lambda b, c: (0, 0))],
        out_specs=pl.BlockSpec((CHUNK, D_MODEL), lambda b, c: (row(b, c), 0)),
        out_shape=jax.ShapeDtypeStruct((TOKENS, D_MODEL), BF16),
        scratch_shapes=[pltpu.VMEM((MLSTM_HEADS, MLSTM_HEAD_DIM, MLSTM_AUG), F32)],
        compiler_params=_params(2),
        name="mlstm_mixer",
    )(proj_m, proj_m, proj_m, proj_m, proj_m, ml_cols, ml_rows, ml_chunk, norm_w)


MERGE_TM = 512
MERGE_TN = 1024


def _merge_kernel(y0_ref, y1_ref, y2_ref, g0_ref, g1_ref, g2_ref, bg_ref, w_ref, o_ref):
    acc = None
    for b, (y_ref, g_ref) in enumerate(((y0_ref, g0_ref), (y1_ref, g1_ref), (y2_ref, g2_ref))):
        gate = _sigmoid(g_ref[...].astype(F32) + bg_ref[b:b + 1, :])
        term = gate * _dot(y_ref[...], w_ref[b])
        acc = term if acc is None else acc + term
    o_ref[...] = acc.astype(o_ref.dtype)


def _merge(y_pool, y_ssm, y_mlstm, proj_g, b_gate, w_branch):
    tm, tn = MERGE_TM, MERGE_TN
    per_seg = D_MODEL // tn
    y_spec = pl.BlockSpec((tm, D_MODEL), lambda i, j: (i, 0))
    gate = lambda b: pl.BlockSpec((tm, tn), lambda i, j: (i, b * per_seg + j))
    return pl.pallas_call(
        _merge_kernel,
        grid=(TOKENS // tm, D_MODEL // tn),
        in_specs=[y_spec, y_spec, y_spec, gate(0), gate(1), gate(2),
                  pl.BlockSpec((N_BRANCH, tn), lambda i, j: (0, j)),
                  pl.BlockSpec((N_BRANCH, D_MODEL, tn), lambda i, j: (0, 0, j))],
        out_specs=pl.BlockSpec((tm, tn), lambda i, j: (i, j)),
        out_shape=jax.ShapeDtypeStruct((TOKENS, D_MODEL), BF16),
        compiler_params=_params(2),
        name="branch_merge",
    )(y_pool, y_ssm, y_mlstm, proj_g, proj_g, proj_g, b_gate, w_branch)


OUT_TM = 512


def _out_kernel(m_ref, x_ref, w_ref, g_ref, b_ref, o_ref, obf_ref):
    h = ALPHA * x_ref[...] + _dot(m_ref[...], w_ref[...])
    mu = jnp.mean(h, axis=-1, keepdims=True)
    var = jnp.mean(jnp.square(h - mu), axis=-1, keepdims=True)
    y = (h - mu) * lax.rsqrt(var + EPS) * g_ref[...] + b_ref[...]
    o_ref[...] = y
    obf_ref[...] = y.astype(BF16)


def _out_proj(merged, x, w_out, ln_g, ln_b):
    tm = OUT_TM
    row = pl.BlockSpec((tm, D_MODEL), lambda i: (i, 0))
    vec = pl.BlockSpec((1, D_MODEL), lambda i: (0, 0))
    return pl.pallas_call(
        _out_kernel,
        grid=(TOKENS // tm,),
        in_specs=[row, row, pl.BlockSpec((D_MODEL, D_MODEL), lambda i: (0, 0)), vec, vec],
        out_specs=[row, row],
        out_shape=[jax.ShapeDtypeStruct((TOKENS, D_MODEL), F32),
                   jax.ShapeDtypeStruct((TOKENS, D_MODEL), BF16)],
        compiler_params=_params(1),
        name="out_proj_ln",
    )(merged, x, w_out, ln_g, ln_b)


def _lane_bcast(vec):
    return jnp.broadcast_to(vec.astype(F32)[:, None], (vec.shape[0], LANES))


def _layer(x, x_mm, w_in, b_gate, w_pool, pool_scale, conv_w, conv_b, dt_bias, a_log, d_skip,
           ssm_norm_w, i_bias, f_bias, mlstm_norm_w, w_branch, w_out, ln_g, ln_b, expand01):
    w_a = w_in[:, :OFF_SSM_DT].astype(BF16)
    w_m = w_in[:, OFF_MLSTM:OFF_MLSTM_GATES].astype(BF16)
    w_g = w_in[:, OFF_MERGE_GATES:].astype(BF16)
    pad = jnp.zeros((D_MODEL, SMALL_WIDTH - SSM_HEADS - 2 * MLSTM_HEADS), w_in.dtype)
    w_s = jnp.concatenate([w_in[:, OFF_SSM_DT:OFF_MLSTM],
                           w_in[:, OFF_MLSTM_GATES:OFF_MERGE_GATES], pad], axis=1).astype(BF16)
    proj_a = _matmul(x_mm, w_a, BF16, 1024, 1024, "in_proj_a")
    proj_m = _matmul(x_mm, w_m, BF16, 1024, 1024, "in_proj_m")
    proj_g = _matmul(x_mm, w_g, BF16, 1024, 1024, "in_proj_g")
    small = _matmul(x_mm, w_s, F32, 1024, SMALL_WIDTH, "in_proj_gates")

    ssd_cols, ssd_rows, ml_cols, ml_rows, ml_chunk = _gate_scan(
        small, _lane_bcast(dt_bias), _lane_bcast(a_log),
        _lane_bcast(jnp.concatenate([i_bias, f_bias])))

    y_pool = _pool_branch(proj_a, w_pool.astype(BF16), pool_scale.reshape(1, D_MODEL))
    y_ssm = _ssd_branch(proj_a, ssd_cols, ssd_rows, conv_w, conv_b.reshape(1, -1),
                        jnp.repeat(d_skip, SSM_HEAD_DIM).reshape(1, D_MODEL),
                        ssm_norm_w.reshape(1, D_MODEL), expand01)
    y_mlstm = _mlstm_branch(proj_m, ml_cols, ml_rows, ml_chunk, mlstm_norm_w.reshape(1, D_MODEL))
    merged = _merge(y_pool, y_ssm, y_mlstm, proj_g, b_gate, w_branch.astype(BF16))
    return _out_proj(merged, x, w_out.astype(BF16), ln_g.reshape(1, D_MODEL), ln_b.reshape(1, D_MODEL))


def kernel(x, w_in, b_gate, w_pool, pool_scale, conv_w, conv_b, dt_bias, a_log, d_skip,
           ssm_norm_w, i_bias, f_bias, mlstm_norm_w, w_branch, w_out, ln_g, ln_b):
    lane = lax.broadcasted_iota(jnp.int32, (3, LANES, D_MODEL), 1)
    chan = lax.broadcasted_iota(jnp.int32, (3, LANES, D_MODEL), 2)
    which = lax.broadcasted_iota(jnp.int32, (3, LANES, D_MODEL), 0)
    expand01 = (lane == which * SSM_HEADS + chan // SSM_HEAD_DIM).astype(BF16)

    h = x.reshape(TOKENS, D_MODEL)
    h_mm = h
    for l in range(DEPTH):
        h, h_mm = _layer(h, h_mm, w_in[l], b_gate[l], w_pool[l], pool_scale[l], conv_w[l],
                         conv_b[l], dt_bias[l], a_log[l], d_skip[l], ssm_norm_w[l], i_bias[l],
                         f_bias[l], mlstm_norm_w[l], w_branch[l], w_out[l], ln_g[l], ln_b[l],
                         expand01)
    return h.reshape(BATCH, SEQ, D_MODEL)
```

```python
import functools

import jax
import jax.numpy as jnp
from jax import lax
from jax.experimental import pallas as pl
from jax.experimental.pallas import tpu as pltpu

F32 = jnp.float32
BF16 = jnp.bfloat16

D_MODEL = 2048
BATCH = 8
SEQ = 2048
DEPTH = 2
TOKENS = BATCH * SEQ
CHUNK = 128
N_CHUNKS = SEQ // CHUNK
LANES = 128

POOL_GROUPS = 4
POOL_WINDOWS = (2, 4, 8, 16)
POOL_GDIM = D_MODEL // POOL_GROUPS
POOL_HALO = 16

SSM_HEAD_DIM = 64
SSM_HEADS = D_MODEL // SSM_HEAD_DIM
SSM_GROUPS = 4
SSM_HEADS_PER_GROUP = SSM_HEADS // SSM_GROUPS
SSM_STATE = 128
SSM_CONV = 4
SSM_BC = SSM_GROUPS * SSM_STATE
SSM_GROUP_WIDTH = D_MODEL // SSM_GROUPS
CONV_HALO = 16

MLSTM_HEADS = 8
MLSTM_HEAD_DIM = D_MODEL // MLSTM_HEADS
MLSTM_AUG = MLSTM_HEAD_DIM + LANES

N_BRANCH = 3
ALPHA = (2 * DEPTH) ** 0.25
EPS = 1e-5

OFF_SSM_DT = 4 * D_MODEL + 2 * SSM_BC
OFF_MLSTM = OFF_SSM_DT + SSM_HEADS
OFF_MLSTM_GATES = OFF_MLSTM + 5 * D_MODEL
OFF_MERGE_GATES = OFF_MLSTM_GATES + 2 * MLSTM_HEADS
IN_DIM = OFF_MERGE_GATES + N_BRANCH * D_MODEL

SEG_PU, SEG_PZ, SEG_SX, SEG_SZ = range(4)
SEG_SB = 4 * D_MODEL // SSM_BC
SEG_SC = SEG_SB + 1
SEG_MQ, SEG_MK, SEG_MV, SEG_MO, SEG_MZ = range(5)
SMALL_WIDTH = LANES
LANE_DT = 0
LANE_IG = SSM_HEADS
LANE_FG = SSM_HEADS + MLSTM_HEADS

VMEM_LIMIT = 56 * 1024 * 1024


def _params(n_axes):
    return pltpu.CompilerParams(dimension_semantics=("arbitrary",) * n_axes,
                                vmem_limit_bytes=VMEM_LIMIT)


def _sigmoid(v):
    return 0.5 * jnp.tanh(0.5 * v) + 0.5


def _silu(v):
    h = 0.5 * v
    return h * jnp.tanh(h) + h


def _split3(v):
    hi = v.astype(BF16)
    r1 = v - hi.astype(F32)
    mid = r1.astype(BF16)
    lo = (r1 - mid.astype(F32)).astype(BF16)
    return hi, mid, lo


def _dot(a, b):
    return jnp.dot(a, b, preferred_element_type=F32)


def _exact_dot_right01(v, mat01):
    hi, mid, lo = _split3(v)
    return _dot(hi, mat01) + _dot(mid, mat01) + _dot(lo, mat01)


def _tril_mask(n):
    r = lax.broadcasted_iota(jnp.int32, (n, n), 0)
    c = lax.broadcasted_iota(jnp.int32, (n, n), 1)
    return c <= r


def _matmul_kernel(x_ref, w_ref, o_ref):
    o_ref[...] = _dot(x_ref[...], w_ref[...]).astype(o_ref.dtype)


def _matmul_cast_kernel(x_ref, w_ref, o_ref, xbf_ref):
    @pl.when(pl.program_id(1) == 0)
    def _():
        xbf_ref[...] = x_ref[...].astype(BF16)

    o_ref[...] = _dot(xbf_ref[...], w_ref[...]).astype(o_ref.dtype)


def _matmul(x, w, out_dtype, tm, tn, name):
    m, k = x.shape
    n = w.shape[1]
    cast = x.dtype != BF16
    return pl.pallas_call(
        _matmul_cast_kernel if cast else _matmul_kernel,
        grid=(m // tm, n // tn),
        in_specs=[pl.BlockSpec((tm, k), lambda i, j: (i, 0)),
                  pl.BlockSpec((k, tn), lambda i, j: (0, j))],
        out_specs=pl.BlockSpec((tm, tn), lambda i, j: (i, j)),
        out_shape=jax.ShapeDtypeStruct((m, n), out_dtype),
        scratch_shapes=[pltpu.VMEM((tm, k), BF16)] if cast else [],
        compiler_params=_params(2),
        name=name,
    )(x, w)


PACK_TN = 1024


def _pack_kernel(lane_shift, *refs):
    if lane_shift == 0:
        a_ref, o_ref = refs
        o_ref[...] = a_ref[0].astype(BF16)
    else:
        a_ref, b_ref, o_ref = refs
        a = a_ref[0]
        nxt = b_ref[0]
        o_ref[...] = jnp.concatenate([a[:, lane_shift:], nxt[:, :lane_shift]], axis=1).astype(BF16)


def _pack_region(w_in, layer, col_start, width, name):
    first_blk, lane_shift = divmod(col_start, PACK_TN)
    assert lane_shift < LANES and width % PACK_TN == 0
    per_blk = PACK_TN // LANES
    in_specs = [pl.BlockSpec((1, D_MODEL, PACK_TN), lambda j: (layer, 0, first_blk + j))]
    operands = [w_in]
    if lane_shift:
        in_specs.append(pl.BlockSpec((1, D_MODEL, LANES),
                                     lambda j: (layer, 0, (first_blk + j + 1) * per_blk)))
        operands.append(w_in)
    return pl.pallas_call(
        functools.partial(_pack_kernel, lane_shift),
        grid=(width // PACK_TN,),
        in_specs=in_specs,
        out_specs=pl.BlockSpec((D_MODEL, PACK_TN), lambda j: (0, j)),
        out_shape=jax.ShapeDtypeStruct((D_MODEL, width), BF16),
        compiler_params=_params(1),
        name=name,
    )(*operands)


POOL_TM = 512


def _pool_kernel(u_ref, z_ref, w_ref, scale_ref, o_ref, ext_ref):
    t = pl.program_id(1)
    tm = POOL_TM

    @pl.when(t == 0)
    def _():
        ext_ref[0:POOL_HALO, :] = jnp.zeros((POOL_HALO, D_MODEL), F32)

    @pl.when(t != 0)
    def _():
        ext_ref[0:POOL_HALO, :] = ext_ref[tm:tm + POOL_HALO, :]

    ext_ref[POOL_HALO:POOL_HALO + tm, :] = u_ref[...].astype(F32)

    pos = t * tm + lax.broadcasted_iota(jnp.int32, (tm, 1), 0)
    for g, win in enumerate(POOL_WINDOWS):
        cols = slice(g * POOL_GDIM, (g + 1) * POOL_GDIM)
        cur = ext_ref[POOL_HALO:POOL_HALO + tm, cols]
        win_sum = cur
        for j in range(1, win):
            win_sum = win_sum + ext_ref[POOL_HALO - j:POOL_HALO - j + tm, cols]
        cnt = jnp.minimum(pos + 1, win).astype(F32)
        pooled = win_sum / cnt - cur
        mixed = _dot(pooled.astype(BF16), w_ref[g])
        y = mixed * scale_ref[:, cols] * _silu(z_ref[:, cols].astype(F32))
        o_ref[:, cols] = y.astype(o_ref.dtype)


def _pool_branch(proj_a, w_pool, pool_scale):
    nt = SEQ // POOL_TM
    return pl.pallas_call(
        _pool_kernel,
        grid=(BATCH, nt),
        in_specs=[pl.BlockSpec((POOL_TM, D_MODEL), lambda b, t: (b * nt + t, SEG_PU)),
                  pl.BlockSpec((POOL_TM, D_MODEL), lambda b, t: (b * nt + t, SEG_PZ)),
                  pl.BlockSpec((POOL_GROUPS, POOL_GDIM, POOL_GDIM), lambda b, t: (0, 0, 0)),
                  pl.BlockSpec((1, D_MODEL), lambda b, t: (0, 0))],
        out_specs=pl.BlockSpec((POOL_TM, D_MODEL), lambda b, t: (b * nt + t, 0)),
        out_shape=jax.ShapeDtypeStruct((TOKENS, D_MODEL), BF16),
        scratch_shapes=[pltpu.VMEM((POOL_HALO + POOL_TM, D_MODEL), F32)],
        compiler_params=_params(2),
        name="pool_mixer",
    )(proj_a, proj_a, w_pool, pool_scale)


SSD_COL_DT, SSD_COL_DTDECAY, SSD_COL_EXPCS, SSD_COL_CS = (k * SSM_HEADS for k in range(4))
ML_COL_C1, ML_COL_CLAMP = 0, MLSTM_HEADS
ML_ROW_G, ML_ROW_WGT = 0, MLSTM_HEADS
ML_CHUNK_MPREV, ML_CHUNK_SOLD = 0, MLSTM_HEADS


def _gate_scan_kernel(small_ref, dtb_ref, alog_ref, gbias_ref,
                      ssd_cols_ref, ssd_rows_ref, ml_cols_ref, ml_rows_ref, ml_chunk_ref):
    nh = MLSTM_HEADS
    r = lax.broadcasted_iota(jnp.int32, (CHUNK, CHUNK), 0)
    cidx = lax.broadcasted_iota(jnp.int32, (CHUNK, CHUNK), 1)
    triu01 = (r <= cidx).astype(BF16)
    lane = lax.broadcasted_iota(jnp.int32, (nh, CHUNK), 1)
    a_coef = -jnp.exp(alog_ref[...])
    m_prev = jnp.zeros((nh, LANES), F32)
    ml_pad = jnp.zeros((CHUNK - 2 * nh, CHUNK), F32)

    for c in range(N_CHUNKS):
        tok = slice(c * CHUNK, (c + 1) * CHUNK)
        small_t = small_ref[tok, :].T

        dt = jax.nn.softplus(small_t[LANE_DT:LANE_DT + SSM_HEADS, :] + dtb_ref[...])
        a_cs = _exact_dot_right01(dt * a_coef, triu01)
        a_last = a_cs[:, CHUNK - 1:CHUNK]
        decay = jnp.exp(a_last - a_cs)
        ssd_rows_ref[0, :, tok] = a_cs
        ssd_cols_ref[tok, :] = jnp.concatenate([dt, dt * decay, jnp.exp(a_cs), a_cs], axis=0).T

        pre = small_t[LANE_IG:LANE_IG + 2 * nh, :] + gbias_ref[...]
        cum = _exact_dot_right01(jax.nn.log_sigmoid(pre), triu01)
        ig = pre[0:nh, :]
        bcum = cum[nh:2 * nh, :]
        g = ig - bcum
        b_last = bcum[:, CHUNK - 1:CHUNK]
        pmax = g
        shift = 1
        while shift < CHUNK:
            pmax = jnp.maximum(pmax, jnp.where(lane >= shift, pltpu.roll(pmax, shift, axis=1), -jnp.inf))
            shift *= 2
        m_t = jnp.maximum(bcum + pmax, bcum + m_prev)
        w_log = b_last + g
        m_loc = jnp.max(w_log, axis=1, keepdims=True)
        m_new = jnp.maximum(b_last + m_prev, m_loc)
        s_old = jnp.exp(b_last + m_prev - m_new)
        s_loc = jnp.exp(m_loc - m_new)
        wgt = jnp.exp(w_log - m_loc) * (s_loc * MLSTM_HEAD_DIM ** -0.5)
        ml_rows_ref[0, :, tok] = jnp.concatenate([g, wgt], axis=0)
        ml_chunk_ref[0, c] = jnp.concatenate([m_prev, s_old], axis=0)
        ml_cols_ref[tok, :] = jnp.concatenate([bcum - m_t, jnp.exp(-m_t), ml_pad], axis=0).T
        m_prev = m_new


def _gate_scan(small, dt_bias, a_log, gate_bias):
    const2 = lambda b: (0, 0)
    return pl.pallas_call(
        _gate_scan_kernel,
        grid=(BATCH,),
        in_specs=[pl.BlockSpec((SEQ, SMALL_WIDTH), lambda b: (b, 0)),
                  pl.BlockSpec((SSM_HEADS, LANES), const2),
                  pl.BlockSpec((SSM_HEADS, LANES), const2),
                  pl.BlockSpec((2 * MLSTM_HEADS, LANES), const2)],
        out_specs=[pl.BlockSpec((SEQ, LANES), lambda b: (b, 0)),
                   pl.BlockSpec((1, SSM_HEADS, SEQ), lambda b: (b, 0, 0)),
                   pl.BlockSpec((SEQ, LANES), lambda b: (b, 0)),
                   pl.BlockSpec((1, 2 * MLSTM_HEADS, SEQ), lambda b: (b, 0, 0)),
                   pl.BlockSpec((1, N_CHUNKS, 2 * MLSTM_HEADS, LANES), lambda b: (b, 0, 0, 0))],
        out_shape=[jax.ShapeDtypeStruct((TOKENS, LANES), F32),
                   jax.ShapeDtypeStruct((BATCH, SSM_HEADS, SEQ), F32),
                   jax.ShapeDtypeStruct((TOKENS, LANES), F32),
                   jax.ShapeDtypeStruct((BATCH, 2 * MLSTM_HEADS, SEQ), F32),
                   jax.ShapeDtypeStruct((BATCH, N_CHUNKS, 2 * MLSTM_HEADS, LANES), F32)],
        compiler_params=_params(1),
        name="gate_scan",
    )(small, dt_bias, a_log, gate_bias)


def _conv_silu(ext_ref, new_ref, w_ref, b_ref, shift_ref, c):
    width = ext_ref.shape[1]

    @pl.when(c == 0)
    def _():
        ext_ref[0:CONV_HALO, :] = jnp.zeros((CONV_HALO, width), BF16)

    @pl.when(c != 0)
    def _():
        ext_ref[0:CONV_HALO, :] = ext_ref[CHUNK:CHUNK + CONV_HALO, :]

    ext_ref[CONV_HALO:CONV_HALO + CHUNK, :] = new_ref[...]
    taps = _dot(shift_ref[...], ext_ref[...])
    acc = b_ref[...] + w_ref[SSM_CONV - 1:SSM_CONV, :] * new_ref[...].astype(F32)
    for j in range(1, SSM_CONV):
        acc = acc + w_ref[SSM_CONV - 1 - j:SSM_CONV - j, :] * taps[(j - 1) * CHUNK:j * CHUNK, :]
    return _silu(acc)


def _ssd_kernel(xs_ref, z_ref, b_ref, c_ref, cols_ref, rows_ref,
                cwx_ref, cwb_ref, cwc_ref, cbx_ref, cbb_ref, cbc_ref,
                dskip_ref, normw_ref, expand_ref, shift_ref,
                o_ref, extx_ref, extb_ref, extc_ref, state_ref):
    c = pl.program_id(1)

    @pl.when(c == 0)
    def _():
        state_ref[...] = jnp.zeros(state_ref.shape, F32)

    xc = _conv_silu(extx_ref, xs_ref, cwx_ref, cbx_ref, shift_ref, c)
    bc = _conv_silu(extb_ref, b_ref, cwb_ref, cbb_ref, shift_ref, c)
    cc = _conv_silu(extc_ref, c_ref, cwc_ref, cbc_ref, shift_ref, c)

    cols = cols_ref[...]
    cols_hi = cols.astype(BF16)
    cols_mid = (cols - cols_hi.astype(F32)).astype(BF16)
    cols_2 = jnp.concatenate([cols_hi, cols_mid], axis=1)
    dt_e = _dot(cols_2, expand_ref[0])
    dtdecay_e = _dot(cols_2, expand_ref[1])
    exp_cs_e = _dot(cols_2, expand_ref[2])
    chunk_decay_e = exp_cs_e[CHUNK - 1:CHUNK, :]
    a_cs_rows = rows_ref[0]
    tril = _tril_mask(CHUNK)

    xdt_bf = (xc * dt_e).astype(BF16)
    xdecay_bf = (xc * dtdecay_e).astype(BF16)
    cc_bf = cc.astype(BF16)

    for g in range(SSM_GROUPS):
        ncols = slice(g * SSM_STATE, (g + 1) * SSM_STATE)
        wcols = slice(g * SSM_GROUP_WIDTH, (g + 1) * SSM_GROUP_WIDTH)
        bg_t = bc[:, ncols].T.astype(BF16)
        cg = cc_bf[:, ncols]
        cb = _dot(cg, bg_t)
        prev = state_ref[g]
        y_off = _dot(cg, prev.astype(BF16)) * exp_cs_e[:, wcols]
        state_ref[g] = prev * chunk_decay_e[:, wcols] + _dot(bg_t, xdecay_bf[:, wcols])
        y_heads = []
        for r in range(SSM_HEADS_PER_GROUP):
            h = g * SSM_HEADS_PER_GROUP + r
            seg = cols[:, SSD_COL_CS + h:SSD_COL_CS + h + 1] - a_cs_rows[h:h + 1, :]
            lmat = jnp.exp(jnp.where(tril, seg, -jnp.inf))
            hcols = slice(h * SSM_HEAD_DIM, (h + 1) * SSM_HEAD_DIM)
            y_heads.append(_dot((cb * lmat).astype(BF16), xdt_bf[:, hcols]))
        y = jnp.concatenate(y_heads, axis=1) + y_off
        y = y + xc[:, wcols] * dskip_ref[:, wcols]
        y = y * _silu(z_ref[:, wcols].astype(F32))
        y = y * lax.rsqrt(jnp.mean(y * y, axis=-1, keepdims=True) + EPS)
        o_ref[:, wcols] = (y * normw_ref[:, wcols]).astype(o_ref.dtype)


def _ssd_branch(proj_a, ssd_cols, ssd_rows, conv_w, conv_b, d_skip, norm_w, expand01):
    nc = N_CHUNKS
    row = lambda b, c: b * nc + c
    const2 = lambda b, c: (0, 0)
    n_shift = (SSM_CONV - 1) * CHUNK
    out_row = lax.broadcasted_iota(jnp.int32, (n_shift, CONV_HALO + CHUNK), 0)
    src_row = lax.broadcasted_iota(jnp.int32, (n_shift, CONV_HALO + CHUNK), 1)
    shift01 = (src_row == CONV_HALO + out_row % CHUNK - out_row // CHUNK - 1).astype(BF16)
    cwx, cwb, cwc = conv_w[:, :D_MODEL], conv_w[:, D_MODEL:D_MODEL + SSM_BC], conv_w[:, D_MODEL + SSM_BC:]
    cbx, cbb, cbc = conv_b[:, :D_MODEL], conv_b[:, D_MODEL:D_MODEL + SSM_BC], conv_b[:, D_MODEL + SSM_BC:]
    return pl.pallas_call(
        _ssd_kernel,
        grid=(BATCH, nc),
        in_specs=[pl.BlockSpec((CHUNK, D_MODEL), lambda b, c: (row(b, c), SEG_SX)),
                  pl.BlockSpec((CHUNK, D_MODEL), lambda b, c: (row(b, c), SEG_SZ)),
                  pl.BlockSpec((CHUNK, SSM_BC), lambda b, c: (row(b, c), SEG_SB)),
                  pl.BlockSpec((CHUNK, SSM_BC), lambda b, c: (row(b, c), SEG_SC)),
                  pl.BlockSpec((CHUNK, LANES), lambda b, c: (row(b, c), 0)),
                  pl.BlockSpec((1, SSM_HEADS, CHUNK), lambda b, c: (b, 0, c)),
                  pl.BlockSpec((SSM_CONV, D_MODEL), const2),
                  pl.BlockSpec((SSM_CONV, SSM_BC), const2),
                  pl.BlockSpec((SSM_CONV, SSM_BC), const2),
                  pl.BlockSpec((1, D_MODEL), const2),
                  pl.BlockSpec((1, SSM_BC), const2),
                  pl.BlockSpec((1, SSM_BC), const2),
                  pl.BlockSpec((1, D_MODEL), const2),
                  pl.BlockSpec((1, D_MODEL), const2),
                  pl.BlockSpec((3, 2 * LANES, D_MODEL), lambda b, c: (0, 0, 0)),
                  pl.BlockSpec((n_shift, CONV_HALO + CHUNK), const2)],
        out_specs=pl.BlockSpec((CHUNK, D_MODEL), lambda b, c: (row(b, c), 0)),
        out_shape=jax.ShapeDtypeStruct((TOKENS, D_MODEL), BF16),
        scratch_shapes=[pltpu.VMEM((CONV_HALO + CHUNK, D_MODEL), BF16),
                        pltpu.VMEM((CONV_HALO + CHUNK, SSM_BC), BF16),
                        pltpu.VMEM((CONV_HALO + CHUNK, SSM_BC), BF16),
                        pltpu.VMEM((SSM_GROUPS, SSM_STATE, SSM_GROUP_WIDTH), F32)],
        compiler_params=_params(2),
        name="ssd_mixer",
    )(proj_a, proj_a, proj_a, proj_a, ssd_cols, ssd_rows, cwx, cwb, cwc, cbx, cbb, cbc,
      d_skip, norm_w, expand01, shift01)


def _mlstm_kernel(q_ref, k_ref, v_ref, og_ref, z_ref, cols_ref, rows_ref, chunk_ref, normw_ref,
                  o_ref, cstate_ref):
    c = pl.program_id(1)
    nh = MLSTM_HEADS
    n_tiles = MLSTM_AUG // LANES
    v_tiles = MLSTM_HEAD_DIM // LANES

    @pl.when(c == 0)
    def _():
        cstate_ref[...] = jnp.zeros(cstate_ref.shape, F32)

    cols_t = cols_ref[...]
    rows = rows_ref[0]
    chunk = chunk_ref[0, 0]
    tril = _tril_mask(CHUNK)
    ones_tile = jnp.ones((CHUNK, LANES), BF16)
    mean_mat = jnp.full((MLSTM_HEAD_DIM, LANES), 1.0 / MLSTM_HEAD_DIM, BF16)
    k_scale = MLSTM_HEAD_DIM ** -0.5

    for h in range(nh):
        cols = slice(h * MLSTM_HEAD_DIM, (h + 1) * MLSTM_HEAD_DIM)
        qh = q_ref[:, cols]
        k_t = k_ref[:, cols].T
        v_aug = jnp.concatenate([v_ref[:, cols], ones_tile], axis=1)
        c_prev = cstate_ref[h]
        g_row = rows[ML_ROW_G + h:ML_ROW_G + h + 1, :]
        wgt_row = rows[ML_ROW_WGT + h:ML_ROW_WGT + h + 1, :]
        m_prev_row = chunk[ML_CHUNK_MPREV + h:ML_CHUNK_MPREV + h + 1, :]
        s_old_row = chunk[ML_CHUNK_SOLD + h:ML_CHUNK_SOLD + h + 1, :]

        c1_d = jnp.broadcast_to(cols_t[:, ML_COL_C1 + h:ML_COL_C1 + h + 1], (CHUNK, CHUNK))
        clamp_d = jnp.broadcast_to(cols_t[:, ML_COL_CLAMP + h:ML_COL_CLAMP + h + 1], (CHUNK, CHUNK))
        dw = jnp.exp(jnp.where(tril, c1_d + g_row, -jnp.inf))
        inter_w = jnp.exp(c1_d + m_prev_row)

        s = _dot(qh, k_t) * k_scale
        intra = _dot((s * dw).astype(BF16), v_aug)
        inter = _dot(qh, c_prev.astype(BF16))
        nd = [intra[:, j * LANES:(j + 1) * LANES] + inter_w * inter[:, j * LANES:(j + 1) * LANES]
              for j in range(n_tiles)]
        rden = 1.0 / jnp.maximum(jnp.abs(nd[-1]), clamp_d)

        wk_t = (k_t.astype(F32) * wgt_row).astype(BF16)
        cstate_ref[h] = (c_prev * jnp.concatenate([s_old_row] * n_tiles, axis=1)
                         + _dot(wk_t, v_aug))

        og = _sigmoid(og_ref[:, cols].astype(F32))
        hc = jnp.concatenate([nd[j] * rden for j in range(v_tiles)], axis=1) * og
        mu = _dot(hc.astype(BF16), mean_mat)
        dev = hc - jnp.concatenate([mu] * v_tiles, axis=1)
        var = _dot((dev * dev).astype(BF16), mean_mat)
        y = dev * jnp.concatenate([lax.rsqrt(var + EPS)] * v_tiles, axis=1)
        y = y * normw_ref[:, cols] * _silu(z_ref[:, cols].astype(F32))
        o_ref[:, cols] = y.astype(o_ref.dtype)


def _mlstm_branch(proj_m, ml_cols, ml_rows, ml_chunk, norm_w):
    nc = N_CHUNKS
    row = lambda b, c: b * nc + c
    seg = lambda s: pl.BlockSpec((CHUNK, D_MODEL), lambda b, c: (row(b, c), s))
    return pl.pallas_call(
        _mlstm_kernel,
        grid=(BATCH, nc),
        in_specs=[seg(SEG_MQ), seg(SEG_MK), seg(SEG_MV), seg(SEG_MO), seg(SEG_MZ),
                  pl.BlockSpec((CHUNK, LANES), lambda b, c: (row(b, c), 0)),
                  pl.BlockSpec((1, 2 * MLSTM_HEADS, CHUNK), lambda b, c: (b, 0, c)),
                  pl.BlockSpec((1, 1, 2 * MLSTM_HEADS, LANES), lambda b, c: (b, c, 0, 0)),
                  pl.BlockSpec((1, D_MODEL), lambda b, c: (0, 0))],
        out_specs=pl.BlockSpec((CHUNK, D_MODEL), lambda b, c: (row(b, c), 0)),
        out_shape=jax.ShapeDtypeStruct((TOKENS, D_MODEL), BF16),
        scratch_shapes=[pltpu.VMEM((MLSTM_HEADS, MLSTM_HEAD_DIM, MLSTM_AUG), F32)],
        compiler_params=_params(2),
        name="mlstm_mixer",
    )(proj_m, proj_m, proj_m, proj_m, proj_m, ml_cols, ml_rows, ml_chunk, norm_w)


MERGE_TM = 512
MERGE_TN = 1024


def _merge_kernel(y0_ref, y1_ref, y2_ref, g0_ref, g1_ref, g2_ref, bg_ref, w_ref, o_ref):
    acc = None
    for b, (y_ref, g_ref) in enumerate(((y0_ref, g0_ref), (y1_ref, g1_ref), (y2_ref, g2_ref))):
        gate = _sigmoid(g_ref[...].astype(F32) + bg_ref[b:b + 1, :])
        term = gate * _dot(y_ref[...], w_ref[b])
        acc = term if acc is None else acc + term
    o_ref[...] = acc.astype(o_ref.dtype)


def _merge(y_pool, y_ssm, y_mlstm, proj_g, b_gate, w_branch):
    tm, tn = MERGE_TM, MERGE_TN
    per_seg = D_MODEL // tn
    y_spec = pl.BlockSpec((tm, D_MODEL), lambda i, j: (i, 0))
    gate = lambda b: pl.BlockSpec((tm, tn), lambda i, j: (i, b * per_seg + j))
    return pl.pallas_call(
        _merge_kernel,
        grid=(TOKENS // tm, D_MODEL // tn),
        in_specs=[y_spec, y_spec, y_spec, gate(0), gate(1), gate(2),
                  pl.BlockSpec((N_BRANCH, tn), lambda i, j: (0, j)),
                  pl.BlockSpec((N_BRANCH, D_MODEL, tn), lambda i, j: (0, 0, j))],
        out_specs=pl.BlockSpec((tm, tn), lambda i, j: (i, j)),
        out_shape=jax.ShapeDtypeStruct((TOKENS, D_MODEL), BF16),
        compiler_params=_params(2),
        name="branch_merge",
    )(y_pool, y_ssm, y_mlstm, proj_g, proj_g, proj_g, b_gate, w_branch)


OUT_TM = 512


def _out_kernel(m_ref, x_ref, w_ref, g_ref, b_ref, o_ref, obf_ref):
    h = ALPHA * x_ref[...] + _dot(m_ref[...], w_ref[...])
    mu = jnp.mean(h, axis=-1, keepdims=True)
    var = jnp.mean(jnp.square(h - mu), axis=-1, keepdims=True)
    y = (h - mu) * lax.rsqrt(var + EPS) * g_ref[...] + b_ref[...]
    o_ref[...] = y
    obf_ref[...] = y.astype(BF16)


def _out_proj(merged, x, w_out, ln_g, ln_b):
    tm = OUT_TM
    row = pl.BlockSpec((tm, D_MODEL), lambda i: (i, 0))
    vec = pl.BlockSpec((1, D_MODEL), lambda i: (0, 0))
    return pl.pallas_call(
        _out_kernel,
        grid=(TOKENS // tm,),
        in_specs=[row, row, pl.BlockSpec((D_MODEL, D_MODEL), lambda i: (0, 0)), vec, vec],
        out_specs=[row, row],
        out_shape=[jax.ShapeDtypeStruct((TOKENS, D_MODEL), F32),
                   jax.ShapeDtypeStruct((TOKENS, D_MODEL), BF16)],
        compiler_params=_params(1),
        name="out_proj_ln",
    )(merged, x, w_out, ln_g, ln_b)


def _lane_bcast(vec):
    return jnp.broadcast_to(vec.astype(F32)[:, None], (vec.shape[0], LANES))


def _layer(layer, x, x_mm, w_in_all, b_gate, w_pool, pool_scale, conv_w, conv_b, dt_bias, a_log,
           d_skip, ssm_norm_w, i_bias, f_bias, mlstm_norm_w, w_branch, w_out, ln_g, ln_b, expand01):
    w_a = _pack_region(w_in_all, layer, 0, OFF_SSM_DT, "pack_w_a")
    w_m = _pack_region(w_in_all, layer, OFF_MLSTM, 5 * D_MODEL, "pack_w_m")
    w_g = _pack_region(w_in_all, layer, OFF_MERGE_GATES, N_BRANCH * D_MODEL, "pack_w_g")
    pad = jnp.zeros((D_MODEL, SMALL_WIDTH - SSM_HEADS - 2 * MLSTM_HEADS), w_in_all.dtype)
    w_s = jnp.concatenate([w_in_all[layer, :, OFF_SSM_DT:OFF_MLSTM],
                           w_in_all[layer, :, OFF_MLSTM_GATES:OFF_MERGE_GATES], pad],
                          axis=1).astype(BF16)
    proj_a = _matmul(x_mm, w_a, BF16, 1024, 1024, "in_proj_a")
    proj_m = _matmul(x_mm, w_m, BF16, 1024, 1024, "in_proj_m")
    proj_g = _matmul(x_mm, w_g, BF16, 1024, 1024, "in_proj_g")
    small = _matmul(x_mm, w_s, F32, 1024, SMALL_WIDTH, "in_proj_gates")

    ssd_cols, ssd_rows, ml_cols, ml_rows, ml_chunk = _gate_scan(
        small, _lane_bcast(dt_bias), _lane_bcast(a_log),
        _lane_bcast(jnp.concatenate([i_bias, f_bias])))

    y_pool = _pool_branch(proj_a, w_pool.astype(BF16), pool_scale.reshape(1, D_MODEL))
    y_ssm = _ssd_branch(proj_a, ssd_cols, ssd_rows, conv_w, conv_b.reshape(1, -1),
                        jnp.repeat(d_skip, SSM_HEAD_DIM).reshape(1, D_MODEL),
                        ssm_norm_w.reshape(1, D_MODEL), expand01)
    y_mlstm = _mlstm_branch(proj_m, ml_cols, ml_rows, ml_chunk, mlstm_norm_w.reshape(1, D_MODEL))
    merged = _merge(y_pool, y_ssm, y_mlstm, proj_g, b_gate, w_branch.astype(BF16))
    return _out_proj(merged, x, w_out.astype(BF16), ln_g.reshape(1, D_MODEL), ln_b.reshape(1, D_MODEL))


def kernel(x, w_in, b_gate, w_pool, pool_scale, conv_w, conv_b, dt_bias, a_log, d_skip,
           ssm_norm_w, i_bias, f_bias, mlstm_norm_w, w_branch, w_out, ln_g, ln_b):
    lane = lax.broadcasted_iota(jnp.int32, (3, 2 * LANES, D_MODEL), 1) % LANES
    chan = lax.broadcasted_iota(jnp.int32, (3, 2 * LANES, D_MODEL), 2)
    which = lax.broadcasted_iota(jnp.int32, (3, 2 * LANES, D_MODEL), 0)
    expand01 = (lane == which * SSM_HEADS + chan // SSM_HEAD_DIM).astype(BF16)

    h = x.reshape(TOKENS, D_MODEL)
    h_mm = h
    for l in range(DEPTH):
        h, h_mm = _layer(l, h, h_mm, w_in, b_gate[l], w_pool[l], pool_scale[l], conv_w[l],
                         conv_b[l], dt_bias[l], a_log[l], d_skip[l], ssm_norm_w[l], i_bias[l],
                         f_bias[l], mlstm_norm_w[l], w_branch[l], w_out[l], ln_g[l], ln_b[l],
                         expand01)
    return h.reshape(BATCH, SEQ, D_MODEL)
```

```python
import functools

import jax
import jax.numpy as jnp
from jax import lax
from jax.experimental import pallas as pl
from jax.experimental.pallas import tpu as pltpu

F32 = jnp.float32
BF16 = jnp.bfloat16

D_MODEL = 2048
BATCH = 8
SEQ = 2048
DEPTH = 2
TOKENS = BATCH * SEQ
CHUNK = 128
N_CHUNKS = SEQ // CHUNK
LANES = 128

POOL_GROUPS = 4
POOL_WINDOWS = (2, 4, 8, 16)
POOL_GDIM = D_MODEL // POOL_GROUPS
POOL_HALO = 16

SSM_HEAD_DIM = 64
SSM_HEADS = D_MODEL // SSM_HEAD_DIM
SSM_GROUPS = 4
SSM_HEADS_PER_GROUP = SSM_HEADS // SSM_GROUPS
SSM_STATE = 128
SSM_CONV = 4
SSM_BC = SSM_GROUPS * SSM_STATE
SSM_GROUP_WIDTH = D_MODEL // SSM_GROUPS
HEADS_PER_TILE = LANES // SSM_HEAD_DIM
CONV_HALO = 16

MLSTM_HEADS = 8
MLSTM_HEAD_DIM = D_MODEL // MLSTM_HEADS
MLSTM_AUG = MLSTM_HEAD_DIM + LANES

N_BRANCH = 3
ALPHA = (2 * DEPTH) ** 0.25
EPS = 1e-5

OFF_SSM_DT = 4 * D_MODEL + 2 * SSM_BC
OFF_MLSTM = OFF_SSM_DT + SSM_HEADS
OFF_MLSTM_GATES = OFF_MLSTM + 5 * D_MODEL
OFF_MERGE_GATES = OFF_MLSTM_GATES + 2 * MLSTM_HEADS
IN_DIM = OFF_MERGE_GATES + N_BRANCH * D_MODEL

SEG_PU, SEG_PZ, SEG_SX, SEG_SZ = range(4)
SEG_SB = 4 * D_MODEL // SSM_BC
SEG_SC = SEG_SB + 1
SEG_MQ, SEG_MK, SEG_MV, SEG_MO, SEG_MZ = range(5)
SMALL_WIDTH = LANES
LANE_DT = 0
LANE_IG = SSM_HEADS
LANE_FG = SSM_HEADS + MLSTM_HEADS

VMEM_LIMIT = 56 * 1024 * 1024


def _params(n_axes):
    return pltpu.CompilerParams(dimension_semantics=("arbitrary",) * n_axes,
                                vmem_limit_bytes=VMEM_LIMIT)


def _sigmoid(v):
    return 0.5 * jnp.tanh(0.5 * v) + 0.5


def _silu(v):
    h = 0.5 * v
    return h * jnp.tanh(h) + h


def _split3(v):
    hi = v.astype(BF16)
    r1 = v - hi.astype(F32)
    mid = r1.astype(BF16)
    lo = (r1 - mid.astype(F32)).astype(BF16)
    return hi, mid, lo


def _dot(a, b):
    return jnp.dot(a, b, preferred_element_type=F32)


def _exact_dot_right01(v, mat01):
    hi, mid, lo = _split3(v)
    return _dot(hi, mat01) + _dot(mid, mat01) + _dot(lo, mat01)


def _tril_mask(n):
    r = lax.broadcasted_iota(jnp.int32, (n, n), 0)
    c = lax.broadcasted_iota(jnp.int32, (n, n), 1)
    return c <= r


def _matmul_kernel(x_ref, w_ref, o_ref):
    o_ref[...] = _dot(x_ref[...], w_ref[...]).astype(o_ref.dtype)


def _matmul_cast_kernel(x_ref, w_ref, o_ref, xbf_ref):
    @pl.when(pl.program_id(1) == 0)
    def _():
        xbf_ref[...] = x_ref[...].astype(BF16)

    o_ref[...] = _dot(xbf_ref[...], w_ref[...]).astype(o_ref.dtype)


def _matmul(x, w, out_dtype, tm, tn, name):
    m, k = x.shape
    n = w.shape[1]
    cast = x.dtype != BF16
    return pl.pallas_call(
        _matmul_cast_kernel if cast else _matmul_kernel,
        grid=(m // tm, n // tn),
        in_specs=[pl.BlockSpec((tm, k), lambda i, j: (i, 0)),
                  pl.BlockSpec((k, tn), lambda i, j: (0, j))],
        out_specs=pl.BlockSpec((tm, tn), lambda i, j: (i, j)),
        out_shape=jax.ShapeDtypeStruct((m, n), out_dtype),
        scratch_shapes=[pltpu.VMEM((tm, k), BF16)] if cast else [],
        compiler_params=_params(2),
        name=name,
    )(x, w)


PACK_TN = 1024


def _pack_kernel(a_ref, o_ref):
    o_ref[...] = a_ref[0].T.astype(BF16)


def _pack_region(w_in_t, layer, col_start, width, name):
    return pl.pallas_call(
        _pack_kernel,
        grid=(width // PACK_TN,),
        in_specs=[pl.BlockSpec((pl.Element(1), pl.Element(PACK_TN), pl.Element(D_MODEL)),
                               lambda j: (layer, pl.multiple_of(col_start + j * PACK_TN, 16), 0))],
        out_specs=pl.BlockSpec((D_MODEL, PACK_TN), lambda j: (0, j)),
        out_shape=jax.ShapeDtypeStruct((D_MODEL, width), BF16),
        compiler_params=_params(1),
        name=name,
    )(w_in_t)


def _pack_gates_kernel(dt_ref, gate_ref, o_ref):
    pad = jnp.zeros((SMALL_WIDTH - SSM_HEADS - 2 * MLSTM_HEADS, D_MODEL), F32)
    rows = jnp.concatenate([dt_ref[0], gate_ref[0], pad], axis=0)
    o_ref[...] = rows.T.astype(BF16)


def _pack_gates(w_in_t, layer):
    rows = lambda n, start: pl.BlockSpec((pl.Element(1), pl.Element(n), pl.Element(D_MODEL)),
                                         lambda i: (layer, start, 0))
    return pl.pallas_call(
        _pack_gates_kernel,
        grid=(1,),
        in_specs=[rows(SSM_HEADS, OFF_SSM_DT), rows(2 * MLSTM_HEADS, OFF_MLSTM_GATES)],
        out_specs=pl.BlockSpec((D_MODEL, LANES), lambda i: (0, 0)),
        out_shape=jax.ShapeDtypeStruct((D_MODEL, SMALL_WIDTH), BF16),
        compiler_params=_params(1),
        name="pack_w_gates",
    )(w_in_t, w_in_t)


POOL_TM = 512


def _pool_kernel(u_ref, z_ref, w_ref, scale_ref, o_ref, ext_ref):
    t = pl.program_id(1)
    tm = POOL_TM

    @pl.when(t == 0)
    def _():
        ext_ref[0:POOL_HALO, :] = jnp.zeros((POOL_HALO, D_MODEL), F32)

    @pl.when(t != 0)
    def _():
        ext_ref[0:POOL_HALO, :] = ext_ref[tm:tm + POOL_HALO, :]

    ext_ref[POOL_HALO:POOL_HALO + tm, :] = u_ref[...].astype(F32)

    pos = t * tm + lax.broadcasted_iota(jnp.int32, (tm, 1), 0)
    for g, win in enumerate(POOL_WINDOWS):
        cols = slice(g * POOL_GDIM, (g + 1) * POOL_GDIM)
        cur = ext_ref[POOL_HALO:POOL_HALO + tm, cols]
        win_sum = cur
        for j in range(1, win):
            win_sum = win_sum + ext_ref[POOL_HALO - j:POOL_HALO - j + tm, cols]
        cnt = jnp.minimum(pos + 1, win).astype(F32)
        pooled = win_sum / cnt - cur
        mixed = _dot(pooled.astype(BF16), w_ref[g])
        y = mixed * scale_ref[:, cols] * _silu(z_ref[:, cols].astype(F32))
        o_ref[:, cols] = y.astype(o_ref.dtype)


def _pool_branch(proj_a, w_pool, pool_scale):
    nt = SEQ // POOL_TM
    return pl.pallas_call(
        _pool_kernel,
        grid=(BATCH, nt),
        in_specs=[pl.BlockSpec((POOL_TM, D_MODEL), lambda b, t: (b * nt + t, SEG_PU)),
                  pl.BlockSpec((POOL_TM, D_MODEL), lambda b, t: (b * nt + t, SEG_PZ)),
                  pl.BlockSpec((POOL_GROUPS, POOL_GDIM, POOL_GDIM), lambda b, t: (0, 0, 0)),
                  pl.BlockSpec((1, D_MODEL), lambda b, t: (0, 0))],
        out_specs=pl.BlockSpec((POOL_TM, D_MODEL), lambda b, t: (b * nt + t, 0)),
        out_shape=jax.ShapeDtypeStruct((TOKENS, D_MODEL), BF16),
        scratch_shapes=[pltpu.VMEM((POOL_HALO + POOL_TM, D_MODEL), F32)],
        compiler_params=_params(2),
        name="pool_mixer",
    )(proj_a, proj_a, w_pool, pool_scale)


SSD_COL_DT, SSD_COL_DTDECAY, SSD_COL_EXPCS, SSD_COL_CS = (k * SSM_HEADS for k in range(4))
ML_COL_C1, ML_COL_CLAMP = 0, MLSTM_HEADS
ML_ROW_G, ML_ROW_WGT = 0, MLSTM_HEADS
ML_CHUNK_MPREV, ML_CHUNK_SOLD = 0, MLSTM_HEADS


def _gate_scan_kernel(small_ref, dtb_ref, alog_ref, gbias_ref,
                      ssd_cols_ref, ssd_rows_ref, ml_cols_ref, ml_rows_ref, ml_chunk_ref):
    nh = MLSTM_HEADS
    r = lax.broadcasted_iota(jnp.int32, (CHUNK, CHUNK), 0)
    cidx = lax.broadcasted_iota(jnp.int32, (CHUNK, CHUNK), 1)
    triu01 = (r <= cidx).astype(BF16)
    lane = lax.broadcasted_iota(jnp.int32, (nh, CHUNK), 1)
    a_coef = -jnp.exp(alog_ref[...])
    m_prev = jnp.zeros((nh, LANES), F32)
    ml_pad = jnp.zeros((CHUNK - 2 * nh, CHUNK), F32)

    for c in range(N_CHUNKS):
        tok = slice(c * CHUNK, (c + 1) * CHUNK)
        small_t = small_ref[tok, :].T

        dt = jax.nn.softplus(small_t[LANE_DT:LANE_DT + SSM_HEADS, :] + dtb_ref[...])
        a_cs = _exact_dot_right01(dt * a_coef, triu01)
        a_last = a_cs[:, CHUNK - 1:CHUNK]
        decay = jnp.exp(a_last - a_cs)
        ssd_rows_ref[0, :, tok] = a_cs
        ssd_cols_ref[tok, :] = jnp.concatenate([dt, dt * decay, jnp.exp(a_cs), a_cs], axis=0).T

        pre = small_t[LANE_IG:LANE_IG + 2 * nh, :] + gbias_ref[...]
        cum = _exact_dot_right01(jax.nn.log_sigmoid(pre), triu01)
        ig = pre[0:nh, :]
        bcum = cum[nh:2 * nh, :]
        g = ig - bcum
        b_last = bcum[:, CHUNK - 1:CHUNK]
        pmax = g
        shift = 1
        while shift < CHUNK:
            pmax = jnp.maximum(pmax, jnp.where(lane >= shift, pltpu.roll(pmax, shift, axis=1), -jnp.inf))
            shift *= 2
        m_t = jnp.maximum(bcum + pmax, bcum + m_prev)
        w_log = b_last + g
        m_loc = jnp.max(w_log, axis=1, keepdims=True)
        m_new = jnp.maximum(b_last + m_prev, m_loc)
        s_old = jnp.exp(b_last + m_prev - m_new)
        s_loc = jnp.exp(m_loc - m_new)
        wgt = jnp.exp(w_log - m_loc) * (s_loc * MLSTM_HEAD_DIM ** -0.5)
        ml_rows_ref[0, :, tok] = jnp.concatenate([g, wgt], axis=0)
        ml_chunk_ref[0, c] = jnp.concatenate([m_prev, s_old], axis=0)
        ml_cols_ref[tok, :] = jnp.concatenate([bcum - m_t, jnp.exp(-m_t), ml_pad], axis=0).T
        m_prev = m_new


def _gate_scan(small, dt_bias, a_log, gate_bias):
    const2 = lambda b: (0, 0)
    return pl.pallas_call(
        _gate_scan_kernel,
        grid=(BATCH,),
        in_specs=[pl.BlockSpec((SEQ, SMALL_WIDTH), lambda b: (b, 0)),
                  pl.BlockSpec((SSM_HEADS, LANES), const2),
                  pl.BlockSpec((SSM_HEADS, LANES), const2),
                  pl.BlockSpec((2 * MLSTM_HEADS, LANES), const2)],
        out_specs=[pl.BlockSpec((SEQ, LANES), lambda b: (b, 0)),
                   pl.BlockSpec((1, SSM_HEADS, SEQ), lambda b: (b, 0, 0)),
                   pl.BlockSpec((SEQ, LANES), lambda b: (b, 0)),
                   pl.BlockSpec((1, 2 * MLSTM_HEADS, SEQ), lambda b: (b, 0, 0)),
                   pl.BlockSpec((1, N_CHUNKS, 2 * MLSTM_HEADS, LANES), lambda b: (b, 0, 0, 0))],
        out_shape=[jax.ShapeDtypeStruct((TOKENS, LANES), F32),
                   jax.ShapeDtypeStruct((BATCH, SSM_HEADS, SEQ), F32),
                   jax.ShapeDtypeStruct((TOKENS, LANES), F32),
                   jax.ShapeDtypeStruct((BATCH, 2 * MLSTM_HEADS, SEQ), F32),
                   jax.ShapeDtypeStruct((BATCH, N_CHUNKS, 2 * MLSTM_HEADS, LANES), F32)],
        compiler_params=_params(1),
        name="gate_scan",
    )(small, dt_bias, a_log, gate_bias)


def _ssd_init(c, extx_ref, extb_ref, extc_ref, state_ref):
    @pl.when(c == 0)
    def _():
        state_ref[...] = jnp.zeros(state_ref.shape, F32)
        for ext_ref in (extx_ref, extb_ref, extc_ref):
            ext_ref[0:CONV_HALO, :] = jnp.zeros((CONV_HALO, ext_ref.shape[1]), BF16)

    @pl.when(c != 0)
    def _():
        for ext_ref in (extx_ref, extb_ref, extc_ref):
            ext_ref[0:CONV_HALO, :] = ext_ref[CHUNK:CHUNK + CONV_HALO, :]


def _conv_silu(ext_ref, new_ref, w_ref, b_ref, shift_ref):
    ext_ref[CONV_HALO:CONV_HALO + CHUNK, :] = new_ref[...]
    taps = _dot(shift_ref[...], ext_ref[...])
    acc = b_ref[...] + w_ref[SSM_CONV - 1:SSM_CONV, :] * new_ref[...].astype(F32)
    for j in range(1, SSM_CONV):
        acc = acc + w_ref[SSM_CONV - 1 - j:SSM_CONV - j, :] * taps[(j - 1) * CHUNK:j * CHUNK, :]
    return _silu(acc)


N_SSD_IN = 16
N_MLSTM_IN = 9


def _ssd_body(xs_ref, z_ref, b_ref, c_ref, cols_ref, rows_ref,
              cwx_ref, cwb_ref, cwc_ref, cbx_ref, cbb_ref, cbc_ref,
              dskip_ref, normw_ref, expand_ref, shift_ref,
              o_ref, extx_ref, extb_ref, extc_ref, state_ref):
    xc = _conv_silu(extx_ref, xs_ref, cwx_ref, cbx_ref, shift_ref)
    bc = _conv_silu(extb_ref, b_ref, cwb_ref, cbb_ref, shift_ref)
    cc = _conv_silu(extc_ref, c_ref, cwc_ref, cbc_ref, shift_ref)

    cols = cols_ref[...]
    cols_hi = cols.astype(BF16)
    cols_mid = (cols - cols_hi.astype(F32)).astype(BF16)
    cols_2 = jnp.concatenate([cols_hi, cols_mid], axis=1)
    dt_e = _dot(cols_2, expand_ref[0])
    dtdecay_e = _dot(cols_2, expand_ref[1])
    exp_cs_e = _dot(cols_2, expand_ref[2])
    chunk_decay_e = exp_cs_e[CHUNK - 1:CHUNK, :]
    a_cs_rows = rows_ref[0]
    tril = _tril_mask(CHUNK)
    tile_head = lax.broadcasted_iota(jnp.int32, (CHUNK, LANES), 1) // SSM_HEAD_DIM

    xdt_bf = (xc * dt_e).astype(BF16)
    xdecay_bf = (xc * dtdecay_e).astype(BF16)
    cc_bf = cc.astype(BF16)

    for g in range(SSM_GROUPS):
        ncols = slice(g * SSM_STATE, (g + 1) * SSM_STATE)
        wcols = slice(g * SSM_GROUP_WIDTH, (g + 1) * SSM_GROUP_WIDTH)
        bg_t = bc[:, ncols].T.astype(BF16)
        cg = cc_bf[:, ncols]
        cb = _dot(cg, bg_t)
        prev = state_ref[g]
        y_off = _dot(cg, prev.astype(BF16)) * exp_cs_e[:, wcols]
        state_ref[g] = prev * chunk_decay_e[:, wcols] + _dot(bg_t, xdecay_bf[:, wcols])
        y_pairs = []
        for pair in range(SSM_HEADS_PER_GROUP // HEADS_PER_TILE):
            wts = []
            for r in range(HEADS_PER_TILE):
                h = g * SSM_HEADS_PER_GROUP + pair * HEADS_PER_TILE + r
                seg = cols[:, SSD_COL_CS + h:SSD_COL_CS + h + 1] - a_cs_rows[h:h + 1, :]
                lmat = jnp.exp(jnp.where(tril, seg, -jnp.inf))
                wts.append((cb * lmat).astype(BF16))
            tile0 = g * SSM_GROUP_WIDTH + pair * LANES
            x_tile = xdt_bf[:, tile0:tile0 + LANES]
            x_diag = jnp.concatenate(
                [jnp.where(tile_head == r, x_tile, jnp.zeros_like(x_tile)) for r in range(HEADS_PER_TILE)],
                axis=0)
            y_pairs.append(_dot(jnp.concatenate(wts, axis=1), x_diag))
        y = jnp.concatenate(y_pairs, axis=1) + y_off
        y = y + xc[:, wcols] * dskip_ref[:, wcols]
        y = y * _silu(z_ref[:, wcols].astype(F32))
        y = y * lax.rsqrt(jnp.mean(y * y, axis=-1, keepdims=True) + EPS)
        o_ref[:, wcols] = (y * normw_ref[:, wcols]).astype(o_ref.dtype)


def _ssd_operands(proj_a, ssd_cols, ssd_rows, conv_w, conv_b, d_skip, norm_w, expand01):
    nc = N_CHUNKS
    row = lambda b, c: b * nc + c
    const2 = lambda b, c: (0, 0)
    n_shift = (SSM_CONV - 1) * CHUNK
    out_row = lax.broadcasted_iota(jnp.int32, (n_shift, CONV_HALO + CHUNK), 0)
    src_row = lax.broadcasted_iota(jnp.int32, (n_shift, CONV_HALO + CHUNK), 1)
    shift01 = (src_row == CONV_HALO + out_row % CHUNK - out_row // CHUNK - 1).astype(BF16)
    cwx, cwb, cwc = conv_w[:, :D_MODEL], conv_w[:, D_MODEL:D_MODEL + SSM_BC], conv_w[:, D_MODEL + SSM_BC:]
    cbx, cbb, cbc = conv_b[:, :D_MODEL], conv_b[:, D_MODEL:D_MODEL + SSM_BC], conv_b[:, D_MODEL + SSM_BC:]
    in_specs = [pl.BlockSpec((CHUNK, D_MODEL), lambda b, c: (row(b, c), SEG_SX)),
                pl.BlockSpec((CHUNK, D_MODEL), lambda b, c: (row(b, c), SEG_SZ)),
                pl.BlockSpec((CHUNK, SSM_BC), lambda b, c: (row(b, c), SEG_SB)),
                pl.BlockSpec((CHUNK, SSM_BC), lambda b, c: (row(b, c), SEG_SC)),
                pl.BlockSpec((CHUNK, LANES), lambda b, c: (row(b, c), 0)),
                pl.BlockSpec((1, SSM_HEADS, CHUNK), lambda b, c: (b, 0, c)),
                pl.BlockSpec((SSM_CONV, D_MODEL), const2),
                pl.BlockSpec((SSM_CONV, SSM_BC), const2),
                pl.BlockSpec((SSM_CONV, SSM_BC), const2),
                pl.BlockSpec((1, D_MODEL), const2),
                pl.BlockSpec((1, SSM_BC), const2),
                pl.BlockSpec((1, SSM_BC), const2),
                pl.BlockSpec((1, D_MODEL), const2),
                pl.BlockSpec((1, D_MODEL), const2),
                pl.BlockSpec((3, 2 * LANES, D_MODEL), lambda b, c: (0, 0, 0)),
                pl.BlockSpec((n_shift, CONV_HALO + CHUNK), const2)]
    operands = [proj_a, proj_a, proj_a, proj_a, ssd_cols, ssd_rows, cwx, cwb, cwc, cbx, cbb, cbc,
                d_skip, norm_w, expand01, shift01]
    scratch = [pltpu.VMEM((CONV_HALO + CHUNK, D_MODEL), BF16),
               pltpu.VMEM((CONV_HALO + CHUNK, SSM_BC), BF16),
               pltpu.VMEM((CONV_HALO + CHUNK, SSM_BC), BF16),
               pltpu.VMEM((SSM_GROUPS, SSM_STATE, SSM_GROUP_WIDTH), F32)]
    assert len(in_specs) == len(operands) == N_SSD_IN
    return in_specs, operands, scratch


def _mlstm_body(q_ref, k_ref, v_ref, og_ref, z_ref, cols_ref, rows_ref, chunk_ref, normw_ref,
                o_ref, cstate_ref):
    nh = MLSTM_HEADS
    n_tiles = MLSTM_AUG // LANES
    v_tiles = MLSTM_HEAD_DIM // LANES

    cols_t = cols_ref[...]
    rows = rows_ref[0]
    chunk = chunk_ref[0, 0]
    tril = _tril_mask(CHUNK)
    ones_tile = jnp.ones((CHUNK, LANES), BF16)
    mean_mat = jnp.full((MLSTM_HEAD_DIM, LANES), 1.0 / MLSTM_HEAD_DIM, BF16)
    k_scale = MLSTM_HEAD_DIM ** -0.5

    for h in range(nh):
        cols = slice(h * MLSTM_HEAD_DIM, (h + 1) * MLSTM_HEAD_DIM)
        qh = q_ref[:, cols]
        k_t = k_ref[:, cols].T
        v_aug = jnp.concatenate([v_ref[:, cols], ones_tile], axis=1)
        c_prev = cstate_ref[h]
        g_row = rows[ML_ROW_G + h:ML_ROW_G + h + 1, :]
        wgt_row = rows[ML_ROW_WGT + h:ML_ROW_WGT + h + 1, :]
        m_prev_row = chunk[ML_CHUNK_MPREV + h:ML_CHUNK_MPREV + h + 1, :]
        s_old_row = chunk[ML_CHUNK_SOLD + h:ML_CHUNK_SOLD + h + 1, :]

        c1_d = jnp.broadcast_to(cols_t[:, ML_COL_C1 + h:ML_COL_C1 + h + 1], (CHUNK, CHUNK))
        clamp_d = jnp.broadcast_to(cols_t[:, ML_COL_CLAMP + h:ML_COL_CLAMP + h + 1], (CHUNK, CHUNK))
        dw = jnp.exp(jnp.where(tril, c1_d + g_row, -jnp.inf))
        inter_w = jnp.exp(c1_d + m_prev_row)

        s = _dot(qh, k_t) * k_scale
        intra = _dot((s * dw).astype(BF16), v_aug)
        inter = _dot(qh, c_prev.astype(BF16))
        nd = [intra[:, j * LANES:(j + 1) * LANES] + inter_w * inter[:, j * LANES:(j + 1) * LANES]
              for j in range(n_tiles)]
        rden = 1.0 / jnp.maximum(jnp.abs(nd[-1]), clamp_d)

        wk_t = (k_t.astype(F32) * wgt_row).astype(BF16)
        cstate_ref[h] = (c_prev * jnp.concatenate([s_old_row] * n_tiles, axis=1)
                         + _dot(wk_t, v_aug))

        og = _sigmoid(og_ref[:, cols].astype(F32))
        hc = jnp.concatenate([nd[j] * rden for j in range(v_tiles)], axis=1) * og
        mu = _dot(hc.astype(BF16), mean_mat)
        dev = hc - jnp.concatenate([mu] * v_tiles, axis=1)
        var = _dot((dev * dev).astype(BF16), mean_mat)
        y = dev * jnp.concatenate([lax.rsqrt(var + EPS)] * v_tiles, axis=1)
        y = y * normw_ref[:, cols] * _silu(z_ref[:, cols].astype(F32))
        o_ref[:, cols] = y.astype(o_ref.dtype)


def _mlstm_operands(proj_m, ml_cols, ml_rows, ml_chunk, norm_w):
    nc = N_CHUNKS
    row = lambda b, c: b * nc + c
    seg = lambda s: pl.BlockSpec((CHUNK, D_MODEL), lambda b, c: (row(b, c), s))
    in_specs = [seg(SEG_MQ), seg(SEG_MK), seg(SEG_MV), seg(SEG_MO), seg(SEG_MZ),
                pl.BlockSpec((CHUNK, LANES), lambda b, c: (row(b, c), 0)),
                pl.BlockSpec((1, 2 * MLSTM_HEADS, CHUNK), lambda b, c: (b, 0, c)),
                pl.BlockSpec((1, 1, 2 * MLSTM_HEADS, LANES), lambda b, c: (b, c, 0, 0)),
                pl.BlockSpec((1, D_MODEL), lambda b, c: (0, 0))]
    operands = [proj_m, proj_m, proj_m, proj_m, proj_m, ml_cols, ml_rows, ml_chunk, norm_w]
    scratch = [pltpu.VMEM((MLSTM_HEADS, MLSTM_HEAD_DIM, MLSTM_AUG), F32)]
    assert len(in_specs) == len(operands) == N_MLSTM_IN
    return in_specs, operands, scratch


def _recurrent_mixers_kernel(*refs):
    ssd_in = refs[:N_SSD_IN]
    ml_in = refs[N_SSD_IN:N_SSD_IN + N_MLSTM_IN]
    o_ssd_ref, o_ml_ref = refs[N_SSD_IN + N_MLSTM_IN:N_SSD_IN + N_MLSTM_IN + 2]
    extx_ref, extb_ref, extc_ref, state_ref, cstate_ref = refs[N_SSD_IN + N_MLSTM_IN + 2:]
    c = pl.program_id(1)
    _ssd_init(c, extx_ref, extb_ref, extc_ref, state_ref)

    @pl.when(c == 0)
    def _():
        cstate_ref[...] = jnp.zeros(cstate_ref.shape, F32)

    _ssd_body(*ssd_in, o_ssd_ref, extx_ref, extb_ref, extc_ref, state_ref)
    _mlstm_body(*ml_in, o_ml_ref, cstate_ref)


def _recurrent_mixers(ssd_args, mlstm_args):
    s_specs, s_ops, s_scratch = _ssd_operands(*ssd_args)
    m_specs, m_ops, m_scratch = _mlstm_operands(*mlstm_args)
    out_spec = pl.BlockSpec((CHUNK, D_MODEL), lambda b, c: (b * N_CHUNKS + c, 0))
    out_shape = jax.ShapeDtypeStruct((TOKENS, D_MODEL), BF16)
    return pl.pallas_call(
        _recurrent_mixers_kernel,
        grid=(BATCH, N_CHUNKS),
        in_specs=s_specs + m_specs,
        out_specs=[out_spec, out_spec],
        out_shape=[out_shape, out_shape],
        scratch_shapes=s_scratch + m_scratch,
        compiler_params=_params(2),
        name="recurrent_mixers",
    )(*s_ops, *m_ops)


MERGE_TM = 512
MERGE_TN = 1024


def _merge_kernel(y0_ref, y1_ref, y2_ref, g0_ref, g1_ref, g2_ref, bg_ref, w_ref, o_ref):
    acc = None
    for b, (y_ref, g_ref) in enumerate(((y0_ref, g0_ref), (y1_ref, g1_ref), (y2_ref, g2_ref))):
        gate = _sigmoid(g_ref[...].astype(F32) + bg_ref[b:b + 1, :])
        term = gate * _dot(y_ref[...], w_ref[b])
        acc = term if acc is None else acc + term
    o_ref[...] = acc.astype(o_ref.dtype)


def _merge(y_pool, y_ssm, y_mlstm, proj_g, b_gate, w_branch):
    tm, tn = MERGE_TM, MERGE_TN
    per_seg = D_MODEL // tn
    y_spec = pl.BlockSpec((tm, D_MODEL), lambda i, j: (i, 0))
    gate = lambda b: pl.BlockSpec((tm, tn), lambda i, j: (i, b * per_seg + j))
    return pl.pallas_call(
        _merge_kernel,
        grid=(TOKENS // tm, D_MODEL // tn),
        in_specs=[y_spec, y_spec, y_spec, gate(0), gate(1), gate(2),
                  pl.BlockSpec((N_BRANCH, tn), lambda i, j: (0, j)),
                  pl.BlockSpec((N_BRANCH, D_MODEL, tn), lambda i, j: (0, 0, j))],
        out_specs=pl.BlockSpec((tm, tn), lambda i, j: (i, j)),
        out_shape=jax.ShapeDtypeStruct((TOKENS, D_MODEL), BF16),
        compiler_params=_params(2),
        name="branch_merge",
    )(y_pool, y_ssm, y_mlstm, proj_g, proj_g, proj_g, b_gate, w_branch)


OUT_TM = 512


def _out_kernel(m_ref, x_ref, w_ref, g_ref, b_ref, o_ref, obf_ref):
    h = ALPHA * x_ref[...] + _dot(m_ref[...], w_ref[...])
    mu = jnp.mean(h, axis=-1, keepdims=True)
    var = jnp.mean(jnp.square(h - mu), axis=-1, keepdims=True)
    y = (h - mu) * lax.rsqrt(var + EPS) * g_ref[...] + b_ref[...]
    o_ref[...] = y
    obf_ref[...] = y.astype(BF16)


def _out_proj(merged, x, w_out, ln_g, ln_b):
    tm = OUT_TM
    row = pl.BlockSpec((tm, D_MODEL), lambda i: (i, 0))
    vec = pl.BlockSpec((1, D_MODEL), lambda i: (0, 0))
    return pl.pallas_call(
        _out_kernel,
        grid=(TOKENS // tm,),
        in_specs=[row, row, pl.BlockSpec((D_MODEL, D_MODEL), lambda i: (0, 0)), vec, vec],
        out_specs=[row, row],
        out_shape=[jax.ShapeDtypeStruct((TOKENS, D_MODEL), F32),
                   jax.ShapeDtypeStruct((TOKENS, D_MODEL), BF16)],
        compiler_params=_params(1),
        name="out_proj_ln",
    )(merged, x, w_out, ln_g, ln_b)


def _lane_bcast(vec):
    return jnp.broadcast_to(vec.astype(F32)[:, None], (vec.shape[0], LANES))


def _layer(layer, x, x_mm, w_in_all, b_gate, w_pool, pool_scale, conv_w, conv_b, dt_bias, a_log,
           d_skip, ssm_norm_w, i_bias, f_bias, mlstm_norm_w, w_branch, w_out, ln_g, ln_b, expand01):
    w_a = _pack_region(w_in_all, layer, 0, OFF_SSM_DT, "pack_w_a")
    w_m = _pack_region(w_in_all, layer, OFF_MLSTM, 5 * D_MODEL, "pack_w_m")
    w_g = _pack_region(w_in_all, layer, OFF_MERGE_GATES, N_BRANCH * D_MODEL, "pack_w_g")
    w_s = _pack_gates(w_in_all, layer)
    proj_a = _matmul(x_mm, w_a, BF16, 1024, 1024, "in_proj_a")
    proj_m = _matmul(x_mm, w_m, BF16, 1024, 1024, "in_proj_m")
    proj_g = _matmul(x_mm, w_g, BF16, 1024, 1024, "in_proj_g")
    small = _matmul(x_mm, w_s, F32, 1024, SMALL_WIDTH, "in_proj_gates")

    ssd_cols, ssd_rows, ml_cols, ml_rows, ml_chunk = _gate_scan(
        small, _lane_bcast(dt_bias), _lane_bcast(a_log),
        _lane_bcast(jnp.concatenate([i_bias, f_bias])))

    y_pool = _pool_branch(proj_a, w_pool.astype(BF16), pool_scale.reshape(1, D_MODEL))
    y_ssm, y_mlstm = _recurrent_mixers(
        (proj_a, ssd_cols, ssd_rows, conv_w, conv_b.reshape(1, -1),
         jnp.repeat(d_skip, SSM_HEAD_DIM).reshape(1, D_MODEL),
         ssm_norm_w.reshape(1, D_MODEL), expand01),
        (proj_m, ml_cols, ml_rows, ml_chunk, mlstm_norm_w.reshape(1, D_MODEL)))
    merged = _merge(y_pool, y_ssm, y_mlstm, proj_g, b_gate, w_branch.astype(BF16))
    return _out_proj(merged, x, w_out.astype(BF16), ln_g.reshape(1, D_MODEL), ln_b.reshape(1, D_MODEL))


def kernel(x, w_in, b_gate, w_pool, pool_scale, conv_w, conv_b, dt_bias, a_log, d_skip,
           ssm_norm_w, i_bias, f_bias, mlstm_norm_w, w_branch, w_out, ln_g, ln_b):
    lane = lax.broadcasted_iota(jnp.int32, (3, 2 * LANES, D_MODEL), 1) % LANES
    chan = lax.broadcasted_iota(jnp.int32, (3, 2 * LANES, D_MODEL), 2)
    which = lax.broadcasted_iota(jnp.int32, (3, 2 * LANES, D_MODEL), 0)
    expand01 = (lane == which * SSM_HEADS + chan // SSM_HEAD_DIM).astype(BF16)

    w_in_t = jnp.swapaxes(w_in, 1, 2)
    h = x.reshape(TOKENS, D_MODEL)
    h_mm = h
    for l in range(DEPTH):
        h, h_mm = _layer(l, h, h_mm, w_in_t, b_gate[l], w_pool[l], pool_scale[l], conv_w[l],
                         conv_b[l], dt_bias[l], a_log[l], d_skip[l], ssm_norm_w[l], i_bias[l],
                         f_bias[l], mlstm_norm_w[l], w_branch[l], w_out[l], ln_g[l], ln_b[l],
                         expand01)
    return h.reshape(BATCH, SEQ, D_MODEL)
```

```python
import functools

import jax
import jax.numpy as jnp
from jax import lax
from jax.experimental import pallas as pl
from jax.experimental.pallas import tpu as pltpu

F32 = jnp.float32
BF16 = jnp.bfloat16

D_MODEL = 2048
BATCH = 8
SEQ = 2048
DEPTH = 2
TOKENS = BATCH * SEQ
CHUNK = 128
N_CHUNKS = SEQ // CHUNK
CHUNKS_PER_STEP = 2
STEP_ROWS = CHUNKS_PER_STEP * CHUNK
STEPS_PER_SEQ = N_CHUNKS // CHUNKS_PER_STEP
LANES = 128

POOL_GROUPS = 4
POOL_WINDOWS = (2, 4, 8, 16)
POOL_GDIM = D_MODEL // POOL_GROUPS
POOL_HALO = 16

SSM_HEAD_DIM = 64
SSM_HEADS = D_MODEL // SSM_HEAD_DIM
SSM_GROUPS = 4
SSM_HEADS_PER_GROUP = SSM_HEADS // SSM_GROUPS
SSM_STATE = 128
SSM_CONV = 4
SSM_BC = SSM_GROUPS * SSM_STATE
SSM_GROUP_WIDTH = D_MODEL // SSM_GROUPS
HEADS_PER_TILE = LANES // SSM_HEAD_DIM
CONV_HALO = 16

MLSTM_HEADS = 8
MLSTM_HEAD_DIM = D_MODEL // MLSTM_HEADS
MLSTM_AUG = MLSTM_HEAD_DIM + LANES

N_BRANCH = 3
ALPHA = (2 * DEPTH) ** 0.25
EPS = 1e-5

OFF_SSM_DT = 4 * D_MODEL + 2 * SSM_BC
OFF_MLSTM = OFF_SSM_DT + SSM_HEADS
OFF_MLSTM_GATES = OFF_MLSTM + 5 * D_MODEL
OFF_MERGE_GATES = OFF_MLSTM_GATES + 2 * MLSTM_HEADS
IN_DIM = OFF_MERGE_GATES + N_BRANCH * D_MODEL

SEG_PU, SEG_PZ, SEG_SX, SEG_SZ = range(4)
SEG_SB = 4 * D_MODEL // SSM_BC
SEG_SC = SEG_SB + 1
SEG_MQ, SEG_MK, SEG_MV, SEG_MO, SEG_MZ = range(5)
SMALL_WIDTH = LANES
LANE_DT = 0
LANE_IG = SSM_HEADS
LANE_FG = SSM_HEADS + MLSTM_HEADS

VMEM_LIMIT = 56 * 1024 * 1024


def _params(n_axes):
    return pltpu.CompilerParams(dimension_semantics=("arbitrary",) * n_axes,
                                vmem_limit_bytes=VMEM_LIMIT)


def _sigmoid(v):
    return 0.5 * jnp.tanh(0.5 * v) + 0.5


def _silu(v):
    h = 0.5 * v
    return h * jnp.tanh(h) + h


def _split3(v):
    hi = v.astype(BF16)
    r1 = v - hi.astype(F32)
    mid = r1.astype(BF16)
    lo = (r1 - mid.astype(F32)).astype(BF16)
    return hi, mid, lo


def _dot(a, b):
    return jnp.dot(a, b, preferred_element_type=F32)


def _exact_dot_right01(v, mat01):
    hi, mid, lo = _split3(v)
    return _dot(hi, mat01) + _dot(mid, mat01) + _dot(lo, mat01)


def _tril_mask(n):
    r = lax.broadcasted_iota(jnp.int32, (n, n), 0)
    c = lax.broadcasted_iota(jnp.int32, (n, n), 1)
    return c <= r


def _matmul_kernel(x_ref, w_ref, o_ref):
    o_ref[...] = _dot(x_ref[...], w_ref[...]).astype(o_ref.dtype)


def _matmul(x, w, out_dtype, tm, tn, name):
    m, k = x.shape
    n = w.shape[1]
    return pl.pallas_call(
        _matmul_kernel,
        grid=(m // tm, n // tn),
        in_specs=[pl.BlockSpec((tm, k), lambda i, j: (i, 0)),
                  pl.BlockSpec((k, tn), lambda i, j: (0, j))],
        out_specs=pl.BlockSpec((tm, tn), lambda i, j: (i, j)),
        out_shape=jax.ShapeDtypeStruct((m, n), out_dtype),
        compiler_params=_params(2),
        name=name,
    )(x, w)


PROJ_TM = 2048
PROJ_TN = 1024
GATES_TM = 1024


def _gates_cast_kernel(x_ref, w_ref, o_ref, xbf_ref):
    xb = x_ref[...].astype(BF16)
    xbf_ref[...] = xb
    o_ref[...] = _dot(xb, w_ref[...])


def _gates_matmul(x, w_s):
    if x.dtype == BF16:
        return _matmul(x, w_s, F32, GATES_TM, SMALL_WIDTH, "in_proj_gates"), x
    m, k = x.shape
    row = pl.BlockSpec((GATES_TM, k), lambda i: (i, 0))
    return pl.pallas_call(
        _gates_cast_kernel,
        grid=(m // GATES_TM,),
        in_specs=[row, pl.BlockSpec((k, SMALL_WIDTH), lambda i: (0, 0))],
        out_specs=[pl.BlockSpec((GATES_TM, SMALL_WIDTH), lambda i: (i, 0)), row],
        out_shape=[jax.ShapeDtypeStruct((m, SMALL_WIDTH), F32), jax.ShapeDtypeStruct((m, k), BF16)],
        compiler_params=_params(1),
        name="in_proj_gates_cast",
    )(x, w_s)


PACK_TN = 1024


def _pack_kernel(a_ref, o_ref):
    o_ref[...] = a_ref[0].T.astype(BF16)


def _pack_region(w_in_t, layer, col_start, width, name):
    return pl.pallas_call(
        _pack_kernel,
        grid=(width // PACK_TN,),
        in_specs=[pl.BlockSpec((pl.Element(1), pl.Element(PACK_TN), pl.Element(D_MODEL)),
                               lambda j: (layer, pl.multiple_of(col_start + j * PACK_TN, 16), 0))],
        out_specs=pl.BlockSpec((D_MODEL, PACK_TN), lambda j: (0, j)),
        out_shape=jax.ShapeDtypeStruct((D_MODEL, width), BF16),
        compiler_params=_params(1),
        name=name,
    )(w_in_t)


def _pack_gates_kernel(dt_ref, gate_ref, o_ref):
    pad = jnp.zeros((SMALL_WIDTH - SSM_HEADS - 2 * MLSTM_HEADS, D_MODEL), F32)
    rows = jnp.concatenate([dt_ref[0], gate_ref[0], pad], axis=0)
    o_ref[...] = rows.T.astype(BF16)


def _pack_gates(w_in_t, layer):
    rows = lambda n, start: pl.BlockSpec((pl.Element(1), pl.Element(n), pl.Element(D_MODEL)),
                                         lambda i: (layer, start, 0))
    return pl.pallas_call(
        _pack_gates_kernel,
        grid=(1,),
        in_specs=[rows(SSM_HEADS, OFF_SSM_DT), rows(2 * MLSTM_HEADS, OFF_MLSTM_GATES)],
        out_specs=pl.BlockSpec((D_MODEL, LANES), lambda i: (0, 0)),
        out_shape=jax.ShapeDtypeStruct((D_MODEL, SMALL_WIDTH), BF16),
        compiler_params=_params(1),
        name="pack_w_gates",
    )(w_in_t, w_in_t)


POOL_TM = 512


def _pool_kernel(u_ref, z_ref, w_ref, scale_ref, o_ref, ext_ref):
    t = pl.program_id(1)
    tm = POOL_TM

    @pl.when(t == 0)
    def _():
        ext_ref[0:POOL_HALO, :] = jnp.zeros((POOL_HALO, D_MODEL), F32)

    @pl.when(t != 0)
    def _():
        ext_ref[0:POOL_HALO, :] = ext_ref[tm:tm + POOL_HALO, :]

    ext_ref[POOL_HALO:POOL_HALO + tm, :] = u_ref[...].astype(F32)

    pos = t * tm + lax.broadcasted_iota(jnp.int32, (tm, 1), 0)
    for g, win in enumerate(POOL_WINDOWS):
        cols = slice(g * POOL_GDIM, (g + 1) * POOL_GDIM)
        cur = ext_ref[POOL_HALO:POOL_HALO + tm, cols]
        win_sum = cur
        for j in range(1, win):
            win_sum = win_sum + ext_ref[POOL_HALO - j:POOL_HALO - j + tm, cols]
        cnt = jnp.minimum(pos + 1, win).astype(F32)
        pooled = win_sum / cnt - cur
        mixed = _dot(pooled.astype(BF16), w_ref[g])
        y = mixed * scale_ref[:, cols] * _silu(z_ref[:, cols].astype(F32))
        o_ref[:, cols] = y.astype(o_ref.dtype)


def _pool_branch(proj_a, w_pool, pool_scale):
    nt = SEQ // POOL_TM
    return pl.pallas_call(
        _pool_kernel,
        grid=(BATCH, nt),
        in_specs=[pl.BlockSpec((POOL_TM, D_MODEL), lambda b, t: (b * nt + t, SEG_PU)),
                  pl.BlockSpec((POOL_TM, D_MODEL), lambda b, t: (b * nt + t, SEG_PZ)),
                  pl.BlockSpec((POOL_GROUPS, POOL_GDIM, POOL_GDIM), lambda b, t: (0, 0, 0)),
                  pl.BlockSpec((1, D_MODEL), lambda b, t: (0, 0))],
        out_specs=pl.BlockSpec((POOL_TM, D_MODEL), lambda b, t: (b * nt + t, 0)),
        out_shape=jax.ShapeDtypeStruct((TOKENS, D_MODEL), BF16),
        scratch_shapes=[pltpu.VMEM((POOL_HALO + POOL_TM, D_MODEL), F32)],
        compiler_params=_params(2),
        name="pool_mixer",
    )(proj_a, proj_a, w_pool, pool_scale)


SSD_COL_DT, SSD_COL_DTDECAY, SSD_COL_EXPCS, SSD_COL_CS = (k * SSM_HEADS for k in range(4))
ML_COL_C1, ML_COL_CLAMP = 0, MLSTM_HEADS
ML_ROW_G, ML_ROW_WGT = 0, MLSTM_HEADS
ML_CHUNK_MPREV, ML_CHUNK_SOLD = 0, MLSTM_HEADS


def _gate_scan_kernel(small_ref, dtb_ref, alog_ref, gbias_ref,
                      ssd_cols_ref, ssd_rows_ref, ml_cols_ref, ml_rows_ref, ml_chunk_ref):
    nh = MLSTM_HEADS
    r = lax.broadcasted_iota(jnp.int32, (CHUNK, CHUNK), 0)
    cidx = lax.broadcasted_iota(jnp.int32, (CHUNK, CHUNK), 1)
    triu01 = (r <= cidx).astype(BF16)
    lane = lax.broadcasted_iota(jnp.int32, (nh, CHUNK), 1)
    a_coef = -jnp.exp(alog_ref[...])
    m_prev = jnp.zeros((nh, LANES), F32)
    ml_pad = jnp.zeros((CHUNK - 2 * nh, CHUNK), F32)

    for c in range(N_CHUNKS):
        tok = slice(c * CHUNK, (c + 1) * CHUNK)
        small_t = small_ref[tok, :].T

        dt = jax.nn.softplus(small_t[LANE_DT:LANE_DT + SSM_HEADS, :] + dtb_ref[...])
        a_cs = _exact_dot_right01(dt * a_coef, triu01)
        a_last = a_cs[:, CHUNK - 1:CHUNK]
        decay = jnp.exp(a_last - a_cs)
        ssd_rows_ref[0, :, tok] = a_cs
        ssd_cols_ref[tok, :] = jnp.concatenate([dt, dt * decay, jnp.exp(a_cs), a_cs], axis=0).T

        pre = small_t[LANE_IG:LANE_IG + 2 * nh, :] + gbias_ref[...]
        cum = _exact_dot_right01(jax.nn.log_sigmoid(pre), triu01)
        ig = pre[0:nh, :]
        bcum = cum[nh:2 * nh, :]
        g = ig - bcum
        b_last = bcum[:, CHUNK - 1:CHUNK]
        pmax = g
        shift = 1
        while shift < CHUNK:
            pmax = jnp.maximum(pmax, jnp.where(lane >= shift, pltpu.roll(pmax, shift, axis=1), -jnp.inf))
            shift *= 2
        m_t = jnp.maximum(bcum + pmax, bcum + m_prev)
        w_log = b_last + g
        m_loc = jnp.max(w_log, axis=1, keepdims=True)
        m_new = jnp.maximum(b_last + m_prev, m_loc)
        s_old = jnp.exp(b_last + m_prev - m_new)
        s_loc = jnp.exp(m_loc - m_new)
        wgt = jnp.exp(w_log - m_loc) * (s_loc * MLSTM_HEAD_DIM ** -0.5)
        ml_rows_ref[0, :, tok] = jnp.concatenate([g, wgt], axis=0)
        ml_chunk_ref[0, c] = jnp.concatenate([m_prev, s_old], axis=0)
        ml_cols_ref[tok, :] = jnp.concatenate([bcum - m_t, jnp.exp(-m_t), ml_pad], axis=0).T
        m_prev = m_new


def _gate_scan(small, dt_bias, a_log, gate_bias):
    const2 = lambda b: (0, 0)
    return pl.pallas_call(
        _gate_scan_kernel,
        grid=(BATCH,),
        in_specs=[pl.BlockSpec((SEQ, SMALL_WIDTH), lambda b: (b, 0)),
                  pl.BlockSpec((SSM_HEADS, LANES), const2),
                  pl.BlockSpec((SSM_HEADS, LANES), const2),
                  pl.BlockSpec((2 * MLSTM_HEADS, LANES), const2)],
        out_specs=[pl.BlockSpec((SEQ, LANES), lambda b: (b, 0)),
                   pl.BlockSpec((1, SSM_HEADS, SEQ), lambda b: (b, 0, 0)),
                   pl.BlockSpec((SEQ, LANES), lambda b: (b, 0)),
                   pl.BlockSpec((1, 2 * MLSTM_HEADS, SEQ), lambda b: (b, 0, 0)),
                   pl.BlockSpec((1, N_CHUNKS, 2 * MLSTM_HEADS, LANES), lambda b: (b, 0, 0, 0))],
        out_shape=[jax.ShapeDtypeStruct((TOKENS, LANES), F32),
                   jax.ShapeDtypeStruct((BATCH, SSM_HEADS, SEQ), F32),
                   jax.ShapeDtypeStruct((TOKENS, LANES), F32),
                   jax.ShapeDtypeStruct((BATCH, 2 * MLSTM_HEADS, SEQ), F32),
                   jax.ShapeDtypeStruct((BATCH, N_CHUNKS, 2 * MLSTM_HEADS, LANES), F32)],
        compiler_params=_params(1),
        name="gate_scan",
    )(small, dt_bias, a_log, gate_bias)


def _ssd_init(c, extx_ref, extb_ref, extc_ref, state_ref):
    @pl.when(c == 0)
    def _():
        state_ref[...] = jnp.zeros(state_ref.shape, F32)
        for ext_ref in (extx_ref, extb_ref, extc_ref):
            ext_ref[0:CONV_HALO, :] = jnp.zeros((CONV_HALO, ext_ref.shape[1]), BF16)

    @pl.when(c != 0)
    def _():
        for ext_ref in (extx_ref, extb_ref, extc_ref):
            ext_ref[0:CONV_HALO, :] = ext_ref[STEP_ROWS:STEP_ROWS + CONV_HALO, :]


def _conv_silu(sub, ext_ref, new_ref, w_ref, b_ref, shift_ref):
    window = ext_ref[sub * CHUNK:sub * CHUNK + CONV_HALO + CHUNK, :]
    taps = _dot(shift_ref[...], window)
    cur = new_ref[sub * CHUNK:(sub + 1) * CHUNK, :].astype(F32)
    acc = b_ref[...] + w_ref[SSM_CONV - 1:SSM_CONV, :] * cur
    for j in range(1, SSM_CONV):
        acc = acc + w_ref[SSM_CONV - 1 - j:SSM_CONV - j, :] * taps[(j - 1) * CHUNK:j * CHUNK, :]
    return _silu(acc)


N_SSD_IN = 16
N_MLSTM_IN = 9


def _ssd_body(sub, xs_ref, z_ref, b_ref, c_ref, cols_ref, rows_ref,
              cwx_ref, cwb_ref, cwc_ref, cbx_ref, cbb_ref, cbc_ref,
              dskip_ref, normw_ref, expand_ref, shift_ref,
              o_ref, extx_ref, extb_ref, extc_ref, state_ref):
    tok = slice(sub * CHUNK, (sub + 1) * CHUNK)
    xc = _conv_silu(sub, extx_ref, xs_ref, cwx_ref, cbx_ref, shift_ref)
    bc = _conv_silu(sub, extb_ref, b_ref, cwb_ref, cbb_ref, shift_ref)
    cc = _conv_silu(sub, extc_ref, c_ref, cwc_ref, cbc_ref, shift_ref)

    cols = cols_ref[tok, :]
    cols_hi = cols.astype(BF16)
    cols_mid = (cols - cols_hi.astype(F32)).astype(BF16)
    cols_2 = jnp.concatenate([cols_hi, cols_mid], axis=1)
    dt_e = _dot(cols_2, expand_ref[0])
    dtdecay_e = _dot(cols_2, expand_ref[1])
    exp_cs_e = _dot(cols_2, expand_ref[2])
    chunk_decay_e = exp_cs_e[CHUNK - 1:CHUNK, :]
    a_cs_rows = rows_ref[0, :, tok]
    tril = _tril_mask(CHUNK)
    tile_head = lax.broadcasted_iota(jnp.int32, (CHUNK, LANES), 1) // SSM_HEAD_DIM

    xdt_bf = (xc * dt_e).astype(BF16)
    xdecay_bf = (xc * dtdecay_e).astype(BF16)
    cc_bf = cc.astype(BF16)

    for g in range(SSM_GROUPS):
        ncols = slice(g * SSM_STATE, (g + 1) * SSM_STATE)
        wcols = slice(g * SSM_GROUP_WIDTH, (g + 1) * SSM_GROUP_WIDTH)
        bg_t = bc[:, ncols].T.astype(BF16)
        cg = cc_bf[:, ncols]
        cb = _dot(cg, bg_t)
        prev = state_ref[g]
        y_off = _dot(cg, prev.astype(BF16)) * exp_cs_e[:, wcols]
        state_ref[g] = prev * chunk_decay_e[:, wcols] + _dot(bg_t, xdecay_bf[:, wcols])
        y_pairs = []
        for pair in range(SSM_HEADS_PER_GROUP // HEADS_PER_TILE):
            wts = []
            for r in range(HEADS_PER_TILE):
                h = g * SSM_HEADS_PER_GROUP + pair * HEADS_PER_TILE + r
                seg = cols[:, SSD_COL_CS + h:SSD_COL_CS + h + 1] - a_cs_rows[h:h + 1, :]
                lmat = jnp.exp(jnp.where(tril, seg, -jnp.inf))
                wts.append((cb * lmat).astype(BF16))
            tile0 = g * SSM_GROUP_WIDTH + pair * LANES
            x_tile = xdt_bf[:, tile0:tile0 + LANES]
            x_diag = jnp.concatenate(
                [jnp.where(tile_head == r, x_tile, jnp.zeros_like(x_tile)) for r in range(HEADS_PER_TILE)],
                axis=0)
            y_pairs.append(_dot(jnp.concatenate(wts, axis=1), x_diag))
        y = jnp.concatenate(y_pairs, axis=1) + y_off
        y = y + xc[:, wcols] * dskip_ref[:, wcols]
        y = y * _silu(z_ref[tok, wcols].astype(F32))
        y = y * lax.rsqrt(jnp.mean(y * y, axis=-1, keepdims=True) + EPS)
        o_ref[tok, wcols] = (y * normw_ref[:, wcols]).astype(o_ref.dtype)


def _ssd_operands(proj_a, ssd_cols, ssd_rows, conv_w, conv_b, d_skip, norm_w, expand01):
    row = lambda b, c: b * STEPS_PER_SEQ + c
    const2 = lambda b, c: (0, 0)
    n_shift = (SSM_CONV - 1) * CHUNK
    out_row = lax.broadcasted_iota(jnp.int32, (n_shift, CONV_HALO + CHUNK), 0)
    src_row = lax.broadcasted_iota(jnp.int32, (n_shift, CONV_HALO + CHUNK), 1)
    shift01 = (src_row == CONV_HALO + out_row % CHUNK - out_row // CHUNK - 1).astype(BF16)
    cwx, cwb, cwc = conv_w[:, :D_MODEL], conv_w[:, D_MODEL:D_MODEL + SSM_BC], conv_w[:, D_MODEL + SSM_BC:]
    cbx, cbb, cbc = conv_b[:, :D_MODEL], conv_b[:, D_MODEL:D_MODEL + SSM_BC], conv_b[:, D_MODEL + SSM_BC:]
    in_specs = [pl.BlockSpec((STEP_ROWS, D_MODEL), lambda b, c: (row(b, c), SEG_SX)),
                pl.BlockSpec((STEP_ROWS, D_MODEL), lambda b, c: (row(b, c), SEG_SZ)),
                pl.BlockSpec((STEP_ROWS, SSM_BC), lambda b, c: (row(b, c), SEG_SB)),
                pl.BlockSpec((STEP_ROWS, SSM_BC), lambda b, c: (row(b, c), SEG_SC)),
                pl.BlockSpec((STEP_ROWS, LANES), lambda b, c: (row(b, c), 0)),
                pl.BlockSpec((1, SSM_HEADS, STEP_ROWS), lambda b, c: (b, 0, c)),
                pl.BlockSpec((SSM_CONV, D_MODEL), const2),
                pl.BlockSpec((SSM_CONV, SSM_BC), const2),
                pl.BlockSpec((SSM_CONV, SSM_BC), const2),
                pl.BlockSpec((1, D_MODEL), const2),
                pl.BlockSpec((1, SSM_BC), const2),
                pl.BlockSpec((1, SSM_BC), const2),
                pl.BlockSpec((1, D_MODEL), const2),
                pl.BlockSpec((1, D_MODEL), const2),
                pl.BlockSpec((3, 2 * LANES, D_MODEL), lambda b, c: (0, 0, 0)),
                pl.BlockSpec((n_shift, CONV_HALO + CHUNK), const2)]
    operands = [proj_a, proj_a, proj_a, proj_a, ssd_cols, ssd_rows, cwx, cwb, cwc, cbx, cbb, cbc,
                d_skip, norm_w, expand01, shift01]
    scratch = [pltpu.VMEM((CONV_HALO + STEP_ROWS, D_MODEL), BF16),
               pltpu.VMEM((CONV_HALO + STEP_ROWS, SSM_BC), BF16),
               pltpu.VMEM((CONV_HALO + STEP_ROWS, SSM_BC), BF16),
               pltpu.VMEM((SSM_GROUPS, SSM_STATE, SSM_GROUP_WIDTH), F32)]
    assert len(in_specs) == len(operands) == N_SSD_IN
    return in_specs, operands, scratch


def _mlstm_body(sub, q_ref, k_ref, v_ref, og_ref, z_ref, cols_ref, rows_ref, chunk_ref, normw_ref,
                o_ref, cstate_ref):
    tok = slice(sub * CHUNK, (sub + 1) * CHUNK)
    nh = MLSTM_HEADS
    n_tiles = MLSTM_AUG // LANES
    v_tiles = MLSTM_HEAD_DIM // LANES

    cols_t = cols_ref[tok, :]
    rows = rows_ref[0, :, tok]
    chunk = chunk_ref[0, sub]
    tril = _tril_mask(CHUNK)
    ones_tile = jnp.ones((CHUNK, LANES), BF16)
    mean_mat = jnp.full((MLSTM_HEAD_DIM, LANES), 1.0 / MLSTM_HEAD_DIM, BF16)
    k_scale = MLSTM_HEAD_DIM ** -0.5

    for h in range(nh):
        cols = slice(h * MLSTM_HEAD_DIM, (h + 1) * MLSTM_HEAD_DIM)
        qh = q_ref[tok, cols]
        k_t = k_ref[tok, cols].T
        v_aug = jnp.concatenate([v_ref[tok, cols], ones_tile], axis=1)
        c_prev = cstate_ref[h]
        g_row = rows[ML_ROW_G + h:ML_ROW_G + h + 1, :]
        wgt_row = rows[ML_ROW_WGT + h:ML_ROW_WGT + h + 1, :]
        m_prev_row = chunk[ML_CHUNK_MPREV + h:ML_CHUNK_MPREV + h + 1, :]
        s_old_row = chunk[ML_CHUNK_SOLD + h:ML_CHUNK_SOLD + h + 1, :]

        c1_d = jnp.broadcast_to(cols_t[:, ML_COL_C1 + h:ML_COL_C1 + h + 1], (CHUNK, CHUNK))
        clamp_d = jnp.broadcast_to(cols_t[:, ML_COL_CLAMP + h:ML_COL_CLAMP + h + 1], (CHUNK, CHUNK))
        dw = jnp.exp(jnp.where(tril, c1_d + g_row, -jnp.inf))
        inter_w = jnp.exp(c1_d + m_prev_row)

        s = _dot(qh, k_t) * k_scale
        intra = _dot((s * dw).astype(BF16), v_aug)
        inter = _dot(qh, c_prev.astype(BF16))
        nd = [intra[:, j * LANES:(j + 1) * LANES] + inter_w * inter[:, j * LANES:(j + 1) * LANES]
              for j in range(n_tiles)]
        rden = 1.0 / jnp.maximum(jnp.abs(nd[-1]), clamp_d)

        wk_t = (k_t.astype(F32) * wgt_row).astype(BF16)
        cstate_ref[h] = (c_prev * jnp.concatenate([s_old_row] * n_tiles, axis=1)
                         + _dot(wk_t, v_aug))

        og = _sigmoid(og_ref[tok, cols].astype(F32))
        hc = jnp.concatenate([nd[j] * rden for j in range(v_tiles)], axis=1) * og
        mu = _dot(hc.astype(BF16), mean_mat)
        dev = hc - jnp.concatenate([mu] * v_tiles, axis=1)
        var = _dot((dev * dev).astype(BF16), mean_mat)
        y = dev * jnp.concatenate([lax.rsqrt(var + EPS)] * v_tiles, axis=1)
        y = y * normw_ref[:, cols] * _silu(z_ref[tok, cols].astype(F32))
        o_ref[tok, cols] = y.astype(o_ref.dtype)


def _mlstm_operands(proj_m, ml_cols, ml_rows, ml_chunk, norm_w):
    row = lambda b, c: b * STEPS_PER_SEQ + c
    seg = lambda s: pl.BlockSpec((STEP_ROWS, D_MODEL), lambda b, c: (row(b, c), s))
    in_specs = [seg(SEG_MQ), seg(SEG_MK), seg(SEG_MV), seg(SEG_MO), seg(SEG_MZ),
                pl.BlockSpec((STEP_ROWS, LANES), lambda b, c: (row(b, c), 0)),
                pl.BlockSpec((1, 2 * MLSTM_HEADS, STEP_ROWS), lambda b, c: (b, 0, c)),
                pl.BlockSpec((1, CHUNKS_PER_STEP, 2 * MLSTM_HEADS, LANES), lambda b, c: (b, c, 0, 0)),
                pl.BlockSpec((1, D_MODEL), lambda b, c: (0, 0))]
    operands = [proj_m, proj_m, proj_m, proj_m, proj_m, ml_cols, ml_rows, ml_chunk, norm_w]
    scratch = [pltpu.VMEM((MLSTM_HEADS, MLSTM_HEAD_DIM, MLSTM_AUG), F32)]
    assert len(in_specs) == len(operands) == N_MLSTM_IN
    return in_specs, operands, scratch


def _recurrent_mixers_kernel(*refs):
    ssd_in = refs[:N_SSD_IN]
    ml_in = refs[N_SSD_IN:N_SSD_IN + N_MLSTM_IN]
    o_ssd_ref, o_ml_ref = refs[N_SSD_IN + N_MLSTM_IN:N_SSD_IN + N_MLSTM_IN + 2]
    extx_ref, extb_ref, extc_ref, state_ref, cstate_ref = refs[N_SSD_IN + N_MLSTM_IN + 2:]
    c = pl.program_id(1)
    _ssd_init(c, extx_ref, extb_ref, extc_ref, state_ref)

    @pl.when(c == 0)
    def _():
        cstate_ref[...] = jnp.zeros(cstate_ref.shape, F32)

    xs_ref, _, b_ref, c_ref = ssd_in[:4]
    for ext_ref, new_ref in ((extx_ref, xs_ref), (extb_ref, b_ref), (extc_ref, c_ref)):
        ext_ref[CONV_HALO:CONV_HALO + STEP_ROWS, :] = new_ref[...]
    for sub in range(CHUNKS_PER_STEP):
        _ssd_body(sub, *ssd_in, o_ssd_ref, extx_ref, extb_ref, extc_ref, state_ref)
        _mlstm_body(sub, *ml_in, o_ml_ref, cstate_ref)


def _recurrent_mixers(ssd_args, mlstm_args):
    s_specs, s_ops, s_scratch = _ssd_operands(*ssd_args)
    m_specs, m_ops, m_scratch = _mlstm_operands(*mlstm_args)
    out_spec = pl.BlockSpec((STEP_ROWS, D_MODEL), lambda b, c: (b * STEPS_PER_SEQ + c, 0))
    out_shape = jax.ShapeDtypeStruct((TOKENS, D_MODEL), BF16)
    return pl.pallas_call(
        _recurrent_mixers_kernel,
        grid=(BATCH, STEPS_PER_SEQ),
        in_specs=s_specs + m_specs,
        out_specs=[out_spec, out_spec],
        out_shape=[out_shape, out_shape],
        scratch_shapes=s_scratch + m_scratch,
        compiler_params=_params(2),
        name="recurrent_mixers",
    )(*s_ops, *m_ops)


MERGE_TM = 512
MERGE_TN = 1024


def _merge_kernel(y0_ref, y1_ref, y2_ref, g0_ref, g1_ref, g2_ref, bg_ref, w_ref, o_ref):
    acc = None
    for b, (y_ref, g_ref) in enumerate(((y0_ref, g0_ref), (y1_ref, g1_ref), (y2_ref, g2_ref))):
        gate = _sigmoid(g_ref[...].astype(F32) + bg_ref[b:b + 1, :])
        term = gate * _dot(y_ref[...], w_ref[b])
        acc = term if acc is None else acc + term
    o_ref[...] = acc.astype(o_ref.dtype)


def _merge(y_pool, y_ssm, y_mlstm, proj_g, b_gate, w_branch):
    tm, tn = MERGE_TM, MERGE_TN
    per_seg = D_MODEL // tn
    y_spec = pl.BlockSpec((tm, D_MODEL), lambda i, j: (i, 0))
    gate = lambda b: pl.BlockSpec((tm, tn), lambda i, j: (i, b * per_seg + j))
    return pl.pallas_call(
        _merge_kernel,
        grid=(TOKENS // tm, D_MODEL // tn),
        in_specs=[y_spec, y_spec, y_spec, gate(0), gate(1), gate(2),
                  pl.BlockSpec((N_BRANCH, tn), lambda i, j: (0, j)),
                  pl.BlockSpec((N_BRANCH, D_MODEL, tn), lambda i, j: (0, 0, j))],
        out_specs=pl.BlockSpec((tm, tn), lambda i, j: (i, j)),
        out_shape=jax.ShapeDtypeStruct((TOKENS, D_MODEL), BF16),
        compiler_params=_params(2),
        name="branch_merge",
    )(y_pool, y_ssm, y_mlstm, proj_g, proj_g, proj_g, b_gate, w_branch)


OUT_TM = 512


def _out_kernel(m_ref, x_ref, w_ref, g_ref, b_ref, o_ref, obf_ref):
    h = ALPHA * x_ref[...] + _dot(m_ref[...], w_ref[...])
    mu = jnp.mean(h, axis=-1, keepdims=True)
    var = jnp.mean(jnp.square(h - mu), axis=-1, keepdims=True)
    y = (h - mu) * lax.rsqrt(var + EPS) * g_ref[...] + b_ref[...]
    o_ref[...] = y
    obf_ref[...] = y.astype(BF16)


def _out_proj(merged, x, w_out, ln_g, ln_b):
    tm = OUT_TM
    row = pl.BlockSpec((tm, D_MODEL), lambda i: (i, 0))
    vec = pl.BlockSpec((1, D_MODEL), lambda i: (0, 0))
    return pl.pallas_call(
        _out_kernel,
        grid=(TOKENS // tm,),
        in_specs=[row, row, pl.BlockSpec((D_MODEL, D_MODEL), lambda i: (0, 0)), vec, vec],
        out_specs=[row, row],
        out_shape=[jax.ShapeDtypeStruct((TOKENS, D_MODEL), F32),
                   jax.ShapeDtypeStruct((TOKENS, D_MODEL), BF16)],
        compiler_params=_params(1),
        name="out_proj_ln",
    )(merged, x, w_out, ln_g, ln_b)


def _lane_bcast(vec):
    return jnp.broadcast_to(vec.astype(F32)[:, None], (vec.shape[0], LANES))


def _layer(layer, x, x_mm, w_in_all, b_gate, w_pool, pool_scale, conv_w, conv_b, dt_bias, a_log,
           d_skip, ssm_norm_w, i_bias, f_bias, mlstm_norm_w, w_branch, w_out, ln_g, ln_b, expand01):
    w_a = _pack_region(w_in_all, layer, 0, OFF_SSM_DT, "pack_w_a")
    w_m = _pack_region(w_in_all, layer, OFF_MLSTM, 5 * D_MODEL, "pack_w_m")
    w_g = _pack_region(w_in_all, layer, OFF_MERGE_GATES, N_BRANCH * D_MODEL, "pack_w_g")
    w_s = _pack_gates(w_in_all, layer)
    small, x_bf = _gates_matmul(x_mm, w_s)
    proj_a = _matmul(x_bf, w_a, BF16, PROJ_TM, PROJ_TN, "in_proj_a")
    proj_m = _matmul(x_bf, w_m, BF16, PROJ_TM, PROJ_TN, "in_proj_m")
    proj_g = _matmul(x_bf, w_g, BF16, PROJ_TM, PROJ_TN, "in_proj_g")

    ssd_cols, ssd_rows, ml_cols, ml_rows, ml_chunk = _gate_scan(
        small, _lane_bcast(dt_bias), _lane_bcast(a_log),
        _lane_bcast(jnp.concatenate([i_bias, f_bias])))

    y_pool = _pool_branch(proj_a, w_pool.astype(BF16), pool_scale.reshape(1, D_MODEL))
    y_ssm, y_mlstm = _recurrent_mixers(
        (proj_a, ssd_cols, ssd_rows, conv_w, conv_b.reshape(1, -1),
         jnp.repeat(d_skip, SSM_HEAD_DIM).reshape(1, D_MODEL),
         ssm_norm_w.reshape(1, D_MODEL), expand01),
        (proj_m, ml_cols, ml_rows, ml_chunk, mlstm_norm_w.reshape(1, D_MODEL)))
    merged = _merge(y_pool, y_ssm, y_mlstm, proj_g, b_gate, w_branch.astype(BF16))
    return _out_proj(merged, x, w_out.astype(BF16), ln_g.reshape(1, D_MODEL), ln_b.reshape(1, D_MODEL))


def kernel(x, w_in, b_gate, w_pool, pool_scale, conv_w, conv_b, dt_bias, a_log, d_skip,
           ssm_norm_w, i_bias, f_bias, mlstm_norm_w, w_branch, w_out, ln_g, ln_b):
    lane = lax.broadcasted_iota(jnp.int32, (3, 2 * LANES, D_MODEL), 1) % LANES
    chan = lax.broadcasted_iota(jnp.int32, (3, 2 * LANES, D_MODEL), 2)
    which = lax.broadcasted_iota(jnp.int32, (3, 2 * LANES, D_MODEL), 0)
    expand01 = (lane == which * SSM_HEADS + chan // SSM_HEAD_DIM).astype(BF16)

    w_in_t = jnp.swapaxes(w_in, 1, 2)
    h = x.reshape(TOKENS, D_MODEL)
    h_mm = h
    for l in range(DEPTH):
        h, h_mm = _layer(l, h, h_mm, w_in_t, b_gate[l], w_pool[l], pool_scale[l], conv_w[l],
                         conv_b[l], dt_bias[l], a_log[l], d_skip[l], ssm_norm_w[l], i_bias[l],
                         f_bias[l], mlstm_norm_w[l], w_branch[l], w_out[l], ln_g[l], ln_b[l],
                         expand01)
    return h.reshape(BATCH, SEQ, D_MODEL)
```

```python
import functools

import jax
import jax.numpy as jnp
from jax import lax
from jax.experimental import pallas as pl
from jax.experimental.pallas import tpu as pltpu

F32 = jnp.float32
BF16 = jnp.bfloat16

D_MODEL = 2048
BATCH = 8
SEQ = 2048
DEPTH = 2
TOKENS = BATCH * SEQ
CHUNK = 128
N_CHUNKS = SEQ // CHUNK
CHUNKS_PER_STEP = 2
STEP_ROWS = CHUNKS_PER_STEP * CHUNK
STEPS_PER_SEQ = N_CHUNKS // CHUNKS_PER_STEP
LANES = 128

POOL_GROUPS = 4
POOL_WINDOWS = (2, 4, 8, 16)
POOL_GDIM = D_MODEL // POOL_GROUPS
POOL_HALO = 16

SSM_HEAD_DIM = 64
SSM_HEADS = D_MODEL // SSM_HEAD_DIM
SSM_GROUPS = 4
SSM_HEADS_PER_GROUP = SSM_HEADS // SSM_GROUPS
SSM_STATE = 128
SSM_CONV = 4
SSM_BC = SSM_GROUPS * SSM_STATE
SSM_GROUP_WIDTH = D_MODEL // SSM_GROUPS
HEADS_PER_TILE = LANES // SSM_HEAD_DIM
CONV_HALO = 16

MLSTM_HEADS = 8
MLSTM_HEAD_DIM = D_MODEL // MLSTM_HEADS
MLSTM_AUG = MLSTM_HEAD_DIM + LANES

N_BRANCH = 3
ALPHA = (2 * DEPTH) ** 0.25
EPS = 1e-5

OFF_SSM_DT = 4 * D_MODEL + 2 * SSM_BC
OFF_MLSTM = OFF_SSM_DT + SSM_HEADS
OFF_MLSTM_GATES = OFF_MLSTM + 5 * D_MODEL
OFF_MERGE_GATES = OFF_MLSTM_GATES + 2 * MLSTM_HEADS
IN_DIM = OFF_MERGE_GATES + N_BRANCH * D_MODEL

SEG_PU, SEG_PZ, SEG_SX, SEG_SZ = range(4)
SEG_SB = 4 * D_MODEL // SSM_BC
SEG_SC = SEG_SB + 1
SEG_MQ, SEG_MK, SEG_MV, SEG_MO, SEG_MZ = range(5)
SMALL_WIDTH = LANES
LANE_DT = 0
LANE_IG = SSM_HEADS
LANE_FG = SSM_HEADS + MLSTM_HEADS

VMEM_LIMIT = 56 * 1024 * 1024


def _params(n_axes):
    return pltpu.CompilerParams(dimension_semantics=("arbitrary",) * n_axes,
                                vmem_limit_bytes=VMEM_LIMIT)


def _sigmoid(v):
    return 0.5 * jnp.tanh(0.5 * v) + 0.5


def _silu(v):
    h = 0.5 * v
    return h * jnp.tanh(h) + h


def _split3(v):
    hi = v.astype(BF16)
    r1 = v - hi.astype(F32)
    mid = r1.astype(BF16)
    lo = (r1 - mid.astype(F32)).astype(BF16)
    return hi, mid, lo


def _dot(a, b):
    return jnp.dot(a, b, preferred_element_type=F32)


def _exact_dot_right01(v, mat01):
    hi, mid, lo = _split3(v)
    return _dot(hi, mat01) + _dot(mid, mat01) + _dot(lo, mat01)


def _tril_mask(n):
    r = lax.broadcasted_iota(jnp.int32, (n, n), 0)
    c = lax.broadcasted_iota(jnp.int32, (n, n), 1)
    return c <= r


def _matmul_kernel(x_ref, w_ref, o_ref):
    o_ref[...] = _dot(x_ref[...], w_ref[...]).astype(o_ref.dtype)


def _matmul(x, w, out_dtype, tm, tn, name):
    m, k = x.shape
    n = w.shape[1]
    return pl.pallas_call(
        _matmul_kernel,
        grid=(m // tm, n // tn),
        in_specs=[pl.BlockSpec((tm, k), lambda i, j: (i, 0)),
                  pl.BlockSpec((k, tn), lambda i, j: (0, j))],
        out_specs=pl.BlockSpec((tm, tn), lambda i, j: (i, j)),
        out_shape=jax.ShapeDtypeStruct((m, n), out_dtype),
        compiler_params=_params(2),
        name=name,
    )(x, w)


PROJ_TM = 2048
PROJ_TN = 1024
GATES_TM = 1024


def _in_proj_kernel(x_ref, wt_ref, o_ref, w_ref):
    @pl.when(pl.program_id(1) == 0)
    def _():
        w_ref[...] = wt_ref[0].T.astype(BF16)

    o_ref[...] = _dot(x_ref[...], w_ref[...]).astype(o_ref.dtype)


def _in_proj(x_bf, w_in_t, layer, col_start, width, name):
    m, k = x_bf.shape
    return pl.pallas_call(
        _in_proj_kernel,
        grid=(width // PROJ_TN, m // PROJ_TM),
        in_specs=[pl.BlockSpec((PROJ_TM, k), lambda j, i: (i, 0)),
                  pl.BlockSpec((pl.Element(1), pl.Element(PROJ_TN), pl.Element(k)),
                               lambda j, i: (layer, pl.multiple_of(col_start + j * PROJ_TN, 16), 0))],
        out_specs=pl.BlockSpec((PROJ_TM, PROJ_TN), lambda j, i: (i, j)),
        out_shape=jax.ShapeDtypeStruct((m, width), BF16),
        scratch_shapes=[pltpu.VMEM((k, PROJ_TN), BF16)],
        compiler_params=_params(2),
        name=name,
    )(x_bf, w_in_t)


def _gates_cast_kernel(x_ref, w_ref, o_ref, xbf_ref):
    xb = x_ref[...].astype(BF16)
    xbf_ref[...] = xb
    o_ref[...] = _dot(xb, w_ref[...])


def _gates_matmul(x, w_s):
    if x.dtype == BF16:
        return _matmul(x, w_s, F32, GATES_TM, SMALL_WIDTH, "in_proj_gates"), x
    m, k = x.shape
    row = pl.BlockSpec((GATES_TM, k), lambda i: (i, 0))
    return pl.pallas_call(
        _gates_cast_kernel,
        grid=(m // GATES_TM,),
        in_specs=[row, pl.BlockSpec((k, SMALL_WIDTH), lambda i: (0, 0))],
        out_specs=[pl.BlockSpec((GATES_TM, SMALL_WIDTH), lambda i: (i, 0)), row],
        out_shape=[jax.ShapeDtypeStruct((m, SMALL_WIDTH), F32), jax.ShapeDtypeStruct((m, k), BF16)],
        compiler_params=_params(1),
        name="in_proj_gates_cast",
    )(x, w_s)


def _pack_gates_kernel(dt_ref, gate_ref, o_ref):
    pad = jnp.zeros((SMALL_WIDTH - SSM_HEADS - 2 * MLSTM_HEADS, D_MODEL), F32)
    rows = jnp.concatenate([dt_ref[0], gate_ref[0], pad], axis=0)
    o_ref[...] = rows.T.astype(BF16)


def _pack_gates(w_in_t, layer):
    rows = lambda n, start: pl.BlockSpec((pl.Element(1), pl.Element(n), pl.Element(D_MODEL)),
                                         lambda i: (layer, start, 0))
    return pl.pallas_call(
        _pack_gates_kernel,
        grid=(1,),
        in_specs=[rows(SSM_HEADS, OFF_SSM_DT), rows(2 * MLSTM_HEADS, OFF_MLSTM_GATES)],
        out_specs=pl.BlockSpec((D_MODEL, LANES), lambda i: (0, 0)),
        out_shape=jax.ShapeDtypeStruct((D_MODEL, SMALL_WIDTH), BF16),
        compiler_params=_params(1),
        name="pack_w_gates",
    )(w_in_t, w_in_t)


POOL_TM = 512


POOL_BAND = CHUNK


def _pool_kernel(u_ref, z_ref, w_ref, scale_ref, band_ref, o_ref, ext_ref):
    t = pl.program_id(1)
    tm = POOL_TM

    @pl.when(t == 0)
    def _():
        ext_ref[0:POOL_HALO, :] = jnp.zeros((POOL_HALO, D_MODEL), BF16)

    @pl.when(t != 0)
    def _():
        ext_ref[0:POOL_HALO, :] = ext_ref[tm:tm + POOL_HALO, :]

    ext_ref[POOL_HALO:POOL_HALO + tm, :] = u_ref[...]

    row = lax.broadcasted_iota(jnp.int32, (POOL_BAND, LANES), 0)
    for g, win in enumerate(POOL_WINDOWS):
        cols = slice(g * POOL_GDIM, (g + 1) * POOL_GDIM)
        pooled = []
        for r in range(tm // POOL_BAND):
            r0 = r * POOL_BAND
            win_sum = _dot(band_ref[g], ext_ref[r0:r0 + POOL_HALO + POOL_BAND, cols])
            cur = u_ref[r0:r0 + POOL_BAND, cols].astype(F32)
            inv_cnt = 1.0 / jnp.minimum(t * tm + r0 + row + 1, win).astype(F32)
            inv_cnt = jnp.concatenate([inv_cnt] * (POOL_GDIM // LANES), axis=1)
            pooled.append((win_sum * inv_cnt - cur).astype(BF16))
        mixed = _dot(jnp.concatenate(pooled, axis=0), w_ref[g])
        y = mixed * scale_ref[:, cols] * _silu(z_ref[:, cols].astype(F32))
        o_ref[:, cols] = y.astype(o_ref.dtype)


def _pool_branch(proj_a, w_pool, pool_scale):
    nt = SEQ // POOL_TM
    shape = (POOL_GROUPS, POOL_BAND, POOL_HALO + POOL_BAND)
    back = (POOL_HALO + lax.broadcasted_iota(jnp.int32, shape, 1)
            - lax.broadcasted_iota(jnp.int32, shape, 2))
    width = jnp.asarray(POOL_WINDOWS, jnp.int32)[:, None, None]
    band01 = ((back >= 0) & (back < width)).astype(BF16)
    return pl.pallas_call(
        _pool_kernel,
        grid=(BATCH, nt),
        in_specs=[pl.BlockSpec((POOL_TM, D_MODEL), lambda b, t: (b * nt + t, SEG_PU)),
                  pl.BlockSpec((POOL_TM, D_MODEL), lambda b, t: (b * nt + t, SEG_PZ)),
                  pl.BlockSpec((POOL_GROUPS, POOL_GDIM, POOL_GDIM), lambda b, t: (0, 0, 0)),
                  pl.BlockSpec((1, D_MODEL), lambda b, t: (0, 0)),
                  pl.BlockSpec(shape, lambda b, t: (0, 0, 0))],
        out_specs=pl.BlockSpec((POOL_TM, D_MODEL), lambda b, t: (b * nt + t, 0)),
        out_shape=jax.ShapeDtypeStruct((TOKENS, D_MODEL), BF16),
        scratch_shapes=[pltpu.VMEM((POOL_HALO + POOL_TM, D_MODEL), BF16)],
        compiler_params=_params(2),
        name="pool_mixer",
    )(proj_a, proj_a, w_pool, pool_scale, band01)


SSD_COL_DT, SSD_COL_DTDECAY, SSD_COL_EXPCS, SSD_COL_CS = (k * SSM_HEADS for k in range(4))
ML_COL_C1, ML_COL_CLAMP = 0, MLSTM_HEADS
ML_ROW_G, ML_ROW_WGT = 0, MLSTM_HEADS
ML_CHUNK_MPREV, ML_CHUNK_SOLD = 0, MLSTM_HEADS


def _gate_scan_kernel(small_ref, dtb_ref, alog_ref, gbias_ref,
                      ssd_cols_ref, ssd_rows_ref, ml_cols_ref, ml_rows_ref, ml_chunk_ref):
    nh = MLSTM_HEADS
    r = lax.broadcasted_iota(jnp.int32, (CHUNK, CHUNK), 0)
    cidx = lax.broadcasted_iota(jnp.int32, (CHUNK, CHUNK), 1)
    triu01 = (r <= cidx).astype(BF16)
    lane = lax.broadcasted_iota(jnp.int32, (nh, CHUNK), 1)
    a_coef = -jnp.exp(alog_ref[...])
    m_prev = jnp.zeros((nh, LANES), F32)
    ml_pad = jnp.zeros((CHUNK - 2 * nh, CHUNK), F32)

    for c in range(N_CHUNKS):
        tok = slice(c * CHUNK, (c + 1) * CHUNK)
        small_t = small_ref[tok, :].T

        dt = jax.nn.softplus(small_t[LANE_DT:LANE_DT + SSM_HEADS, :] + dtb_ref[...])
        a_cs = _exact_dot_right01(dt * a_coef, triu01)
        a_last = a_cs[:, CHUNK - 1:CHUNK]
        decay = jnp.exp(a_last - a_cs)
        ssd_rows_ref[0, :, tok] = a_cs
        ssd_cols_ref[tok, :] = jnp.concatenate([dt, dt * decay, jnp.exp(a_cs), a_cs], axis=0).T

        pre = small_t[LANE_IG:LANE_IG + 2 * nh, :] + gbias_ref[...]
        cum = _exact_dot_right01(jax.nn.log_sigmoid(pre), triu01)
        ig = pre[0:nh, :]
        bcum = cum[nh:2 * nh, :]
        g = ig - bcum
        b_last = bcum[:, CHUNK - 1:CHUNK]
        pmax = g
        shift = 1
        while shift < CHUNK:
            pmax = jnp.maximum(pmax, jnp.where(lane >= shift, pltpu.roll(pmax, shift, axis=1), -jnp.inf))
            shift *= 2
        m_t = jnp.maximum(bcum + pmax, bcum + m_prev)
        w_log = b_last + g
        m_loc = jnp.max(w_log, axis=1, keepdims=True)
        m_new = jnp.maximum(b_last + m_prev, m_loc)
        s_old = jnp.exp(b_last + m_prev - m_new)
        s_loc = jnp.exp(m_loc - m_new)
        wgt = jnp.exp(w_log - m_loc) * (s_loc * MLSTM_HEAD_DIM ** -0.5)
        ml_rows_ref[0, :, tok] = jnp.concatenate([g, wgt], axis=0)
        ml_chunk_ref[0, c] = jnp.concatenate([m_prev, s_old], axis=0)
        ml_cols_ref[tok, :] = jnp.concatenate([bcum - m_t, jnp.exp(-m_t), ml_pad], axis=0).T
        m_prev = m_new


def _gate_scan(small, dt_bias, a_log, gate_bias):
    const2 = lambda b: (0, 0)
    return pl.pallas_call(
        _gate_scan_kernel,
        grid=(BATCH,),
        in_specs=[pl.BlockSpec((SEQ, SMALL_WIDTH), lambda b: (b, 0)),
                  pl.BlockSpec((SSM_HEADS, LANES), const2),
                  pl.BlockSpec((SSM_HEADS, LANES), const2),
                  pl.BlockSpec((2 * MLSTM_HEADS, LANES), const2)],
        out_specs=[pl.BlockSpec((SEQ, LANES), lambda b: (b, 0)),
                   pl.BlockSpec((1, SSM_HEADS, SEQ), lambda b: (b, 0, 0)),
                   pl.BlockSpec((SEQ, LANES), lambda b: (b, 0)),
                   pl.BlockSpec((1, 2 * MLSTM_HEADS, SEQ), lambda b: (b, 0, 0)),
                   pl.BlockSpec((1, N_CHUNKS, 2 * MLSTM_HEADS, LANES), lambda b: (b, 0, 0, 0))],
        out_shape=[jax.ShapeDtypeStruct((TOKENS, LANES), F32),
                   jax.ShapeDtypeStruct((BATCH, SSM_HEADS, SEQ), F32),
                   jax.ShapeDtypeStruct((TOKENS, LANES), F32),
                   jax.ShapeDtypeStruct((BATCH, 2 * MLSTM_HEADS, SEQ), F32),
                   jax.ShapeDtypeStruct((BATCH, N_CHUNKS, 2 * MLSTM_HEADS, LANES), F32)],
        compiler_params=_params(1),
        name="gate_scan",
    )(small, dt_bias, a_log, gate_bias)


def _ssd_init(c, extx_ref, extb_ref, extc_ref, state_ref):
    @pl.when(c == 0)
    def _():
        state_ref[...] = jnp.zeros(state_ref.shape, F32)
        for ext_ref in (extx_ref, extb_ref, extc_ref):
            ext_ref[0:CONV_HALO, :] = jnp.zeros((CONV_HALO, ext_ref.shape[1]), BF16)

    @pl.when(c != 0)
    def _():
        for ext_ref in (extx_ref, extb_ref, extc_ref):
            ext_ref[0:CONV_HALO, :] = ext_ref[STEP_ROWS:STEP_ROWS + CONV_HALO, :]


def _conv_silu(sub, ext_ref, new_ref, w_ref, b_ref, shift_ref):
    window = ext_ref[sub * CHUNK:sub * CHUNK + CONV_HALO + CHUNK, :]
    taps = _dot(shift_ref[...], window)
    cur = new_ref[sub * CHUNK:(sub + 1) * CHUNK, :].astype(F32)
    acc = b_ref[...] + w_ref[SSM_CONV - 1:SSM_CONV, :] * cur
    for j in range(1, SSM_CONV):
        acc = acc + w_ref[SSM_CONV - 1 - j:SSM_CONV - j, :] * taps[(j - 1) * CHUNK:j * CHUNK, :]
    return _silu(acc)


N_SSD_IN = 16
N_MLSTM_IN = 9


def _ssd_body(sub, xs_ref, z_ref, b_ref, c_ref, cols_ref, rows_ref,
              cwx_ref, cwb_ref, cwc_ref, cbx_ref, cbb_ref, cbc_ref,
              dskip_ref, normw_ref, expand_ref, shift_ref,
              o_ref, extx_ref, extb_ref, extc_ref, state_ref):
    tok = slice(sub * CHUNK, (sub + 1) * CHUNK)
    xc = _conv_silu(sub, extx_ref, xs_ref, cwx_ref, cbx_ref, shift_ref)
    bc = _conv_silu(sub, extb_ref, b_ref, cwb_ref, cbb_ref, shift_ref)
    cc = _conv_silu(sub, extc_ref, c_ref, cwc_ref, cbc_ref, shift_ref)

    cols = cols_ref[tok, :]
    cols_hi = cols.astype(BF16)
    cols_mid = (cols - cols_hi.astype(F32)).astype(BF16)
    cols_2 = jnp.concatenate([cols_hi, cols_mid], axis=1)
    dt_e = _dot(cols_2, expand_ref[0])
    dtdecay_e = _dot(cols_2, expand_ref[1])
    exp_cs_e = _dot(cols_2, expand_ref[2])
    chunk_decay_e = exp_cs_e[CHUNK - 1:CHUNK, :]
    a_cs_rows = rows_ref[0, :, tok]
    tril = _tril_mask(CHUNK)
    tile_head = lax.broadcasted_iota(jnp.int32, (CHUNK, LANES), 1) // SSM_HEAD_DIM

    xdt_bf = (xc * dt_e).astype(BF16)
    xdecay_bf = (xc * dtdecay_e).astype(BF16)
    cc_bf = cc.astype(BF16)

    for g in range(SSM_GROUPS):
        ncols = slice(g * SSM_STATE, (g + 1) * SSM_STATE)
        wcols = slice(g * SSM_GROUP_WIDTH, (g + 1) * SSM_GROUP_WIDTH)
        bg_t = bc[:, ncols].T.astype(BF16)
        cg = cc_bf[:, ncols]
        cb = _dot(cg, bg_t)
        prev = state_ref[g]
        y_off = _dot(cg, prev.astype(BF16)) * exp_cs_e[:, wcols]
        state_ref[g] = prev * chunk_decay_e[:, wcols] + _dot(bg_t, xdecay_bf[:, wcols])
        y_pairs = []
        for pair in range(SSM_HEADS_PER_GROUP // HEADS_PER_TILE):
            wts = []
            for r in range(HEADS_PER_TILE):
                h = g * SSM_HEADS_PER_GROUP + pair * HEADS_PER_TILE + r
                seg = cols[:, SSD_COL_CS + h:SSD_COL_CS + h + 1] - a_cs_rows[h:h + 1, :]
                lmat = jnp.exp(jnp.where(tril, seg, -jnp.inf))
                wts.append((cb * lmat).astype(BF16))
            tile0 = g * SSM_GROUP_WIDTH + pair * LANES
            x_tile = xdt_bf[:, tile0:tile0 + LANES]
            x_diag = jnp.concatenate(
                [jnp.where(tile_head == r, x_tile, jnp.zeros_like(x_tile)) for r in range(HEADS_PER_TILE)],
                axis=0)
            y_pairs.append(_dot(jnp.concatenate(wts, axis=1), x_diag))
        y = jnp.concatenate(y_pairs, axis=1) + y_off
        y = y + xc[:, wcols] * dskip_ref[:, wcols]
        y = y * _silu(z_ref[tok, wcols].astype(F32))
        y = y * lax.rsqrt(jnp.mean(y * y, axis=-1, keepdims=True) + EPS)
        o_ref[tok, wcols] = (y * normw_ref[:, wcols]).astype(o_ref.dtype)


def _ssd_operands(proj_a, ssd_cols, ssd_rows, conv_w, conv_b, d_skip, norm_w, expand01):
    row = lambda b, c: b * STEPS_PER_SEQ + c
    const2 = lambda b, c: (0, 0)
    n_shift = (SSM_CONV - 1) * CHUNK
    out_row = lax.broadcasted_iota(jnp.int32, (n_shift, CONV_HALO + CHUNK), 0)
    src_row = lax.broadcasted_iota(jnp.int32, (n_shift, CONV_HALO + CHUNK), 1)
    shift01 = (src_row == CONV_HALO + out_row % CHUNK - out_row // CHUNK - 1).astype(BF16)
    cwx, cwb, cwc = conv_w[:, :D_MODEL], conv_w[:, D_MODEL:D_MODEL + SSM_BC], conv_w[:, D_MODEL + SSM_BC:]
    cbx, cbb, cbc = conv_b[:, :D_MODEL], conv_b[:, D_MODEL:D_MODEL + SSM_BC], conv_b[:, D_MODEL + SSM_BC:]
    in_specs = [pl.BlockSpec((STEP_ROWS, D_MODEL), lambda b, c: (row(b, c), SEG_SX)),
                pl.BlockSpec((STEP_ROWS, D_MODEL), lambda b, c: (row(b, c), SEG_SZ)),
                pl.BlockSpec((STEP_ROWS, SSM_BC), lambda b, c: (row(b, c), SEG_SB)),
                pl.BlockSpec((STEP_ROWS, SSM_BC), lambda b, c: (row(b, c), SEG_SC)),
                pl.BlockSpec((STEP_ROWS, LANES), lambda b, c: (row(b, c), 0)),
                pl.BlockSpec((1, SSM_HEADS, STEP_ROWS), lambda b, c: (b, 0, c)),
                pl.BlockSpec((SSM_CONV, D_MODEL), const2),
                pl.BlockSpec((SSM_CONV, SSM_BC), const2),
                pl.BlockSpec((SSM_CONV, SSM_BC), const2),
                pl.BlockSpec((1, D_MODEL), const2),
                pl.BlockSpec((1, SSM_BC), const2),
                pl.BlockSpec((1, SSM_BC), const2),
                pl.BlockSpec((1, D_MODEL), const2),
                pl.BlockSpec((1, D_MODEL), const2),
                pl.BlockSpec((3, 2 * LANES, D_MODEL), lambda b, c: (0, 0, 0)),
                pl.BlockSpec((n_shift, CONV_HALO + CHUNK), const2)]
    operands = [proj_a, proj_a, proj_a, proj_a, ssd_cols, ssd_rows, cwx, cwb, cwc, cbx, cbb, cbc,
                d_skip, norm_w, expand01, shift01]
    scratch = [pltpu.VMEM((CONV_HALO + STEP_ROWS, D_MODEL), BF16),
               pltpu.VMEM((CONV_HALO + STEP_ROWS, SSM_BC), BF16),
               pltpu.VMEM((CONV_HALO + STEP_ROWS, SSM_BC), BF16),
               pltpu.VMEM((SSM_GROUPS, SSM_STATE, SSM_GROUP_WIDTH), F32)]
    assert len(in_specs) == len(operands) == N_SSD_IN
    return in_specs, operands, scratch


def _mlstm_body(sub, q_ref, k_ref, v_ref, og_ref, z_ref, cols_ref, rows_ref, chunk_ref, normw_ref,
                o_ref, cstate_ref):
    tok = slice(sub * CHUNK, (sub + 1) * CHUNK)
    nh = MLSTM_HEADS
    n_tiles = MLSTM_AUG // LANES
    v_tiles = MLSTM_HEAD_DIM // LANES

    cols_t = cols_ref[tok, :]
    rows = rows_ref[0, :, tok]
    chunk = chunk_ref[0, sub]
    tril = _tril_mask(CHUNK)
    ones_tile = jnp.ones((CHUNK, LANES), BF16)
    mean_mat = jnp.full((MLSTM_HEAD_DIM, LANES), 1.0 / MLSTM_HEAD_DIM, BF16)
    k_scale = MLSTM_HEAD_DIM ** -0.5

    for h in range(nh):
        cols = slice(h * MLSTM_HEAD_DIM, (h + 1) * MLSTM_HEAD_DIM)
        qh = q_ref[tok, cols]
        k_t = k_ref[tok, cols].T
        v_aug = jnp.concatenate([v_ref[tok, cols], ones_tile], axis=1)
        c_prev = cstate_ref[h]
        g_row = rows[ML_ROW_G + h:ML_ROW_G + h + 1, :]
        wgt_row = rows[ML_ROW_WGT + h:ML_ROW_WGT + h + 1, :]
        m_prev_row = chunk[ML_CHUNK_MPREV + h:ML_CHUNK_MPREV + h + 1, :]
        s_old_row = chunk[ML_CHUNK_SOLD + h:ML_CHUNK_SOLD + h + 1, :]

        c1_d = jnp.broadcast_to(cols_t[:, ML_COL_C1 + h:ML_COL_C1 + h + 1], (CHUNK, CHUNK))
        clamp_d = jnp.broadcast_to(cols_t[:, ML_COL_CLAMP + h:ML_COL_CLAMP + h + 1], (CHUNK, CHUNK))
        dw = jnp.exp(jnp.where(tril, c1_d + g_row, -jnp.inf))
        inter_w = jnp.exp(c1_d + m_prev_row)

        s = _dot(qh, k_t) * k_scale
        intra = _dot((s * dw).astype(BF16), v_aug)
        inter = _dot(qh, c_prev.astype(BF16))
        nd = [intra[:, j * LANES:(j + 1) * LANES] + inter_w * inter[:, j * LANES:(j + 1) * LANES]
              for j in range(n_tiles)]
        rden = 1.0 / jnp.maximum(jnp.abs(nd[-1]), clamp_d)

        wk_t = (k_t.astype(F32) * wgt_row).astype(BF16)
        cstate_ref[h] = (c_prev * jnp.concatenate([s_old_row] * n_tiles, axis=1)
                         + _dot(wk_t, v_aug))

        og = _sigmoid(og_ref[tok, cols].astype(F32))
        hc = jnp.concatenate([nd[j] * rden for j in range(v_tiles)], axis=1) * og
        mu = _dot(hc.astype(BF16), mean_mat)
        dev = hc - jnp.concatenate([mu] * v_tiles, axis=1)
        var = _dot((dev * dev).astype(BF16), mean_mat)
        y = dev * jnp.concatenate([lax.rsqrt(var + EPS)] * v_tiles, axis=1)
        y = y * normw_ref[:, cols] * _silu(z_ref[tok, cols].astype(F32))
        o_ref[tok, cols] = y.astype(o_ref.dtype)


def _mlstm_operands(proj_m, ml_cols, ml_rows, ml_chunk, norm_w):
    row = lambda b, c: b * STEPS_PER_SEQ + c
    seg = lambda s: pl.BlockSpec((STEP_ROWS, D_MODEL), lambda b, c: (row(b, c), s))
    in_specs = [seg(SEG_MQ), seg(SEG_MK), seg(SEG_MV), seg(SEG_MO), seg(SEG_MZ),
                pl.BlockSpec((STEP_ROWS, LANES), lambda b, c: (row(b, c), 0)),
                pl.BlockSpec((1, 2 * MLSTM_HEADS, STEP_ROWS), lambda b, c: (b, 0, c)),
                pl.BlockSpec((1, CHUNKS_PER_STEP, 2 * MLSTM_HEADS, LANES), lambda b, c: (b, c, 0, 0)),
                pl.BlockSpec((1, D_MODEL), lambda b, c: (0, 0))]
    operands = [proj_m, proj_m, proj_m, proj_m, proj_m, ml_cols, ml_rows, ml_chunk, norm_w]
    scratch = [pltpu.VMEM((MLSTM_HEADS, MLSTM_HEAD_DIM, MLSTM_AUG), F32)]
    assert len(in_specs) == len(operands) == N_MLSTM_IN
    return in_specs, operands, scratch


def _recurrent_mixers_kernel(*refs):
    ssd_in = refs[:N_SSD_IN]
    ml_in = refs[N_SSD_IN:N_SSD_IN + N_MLSTM_IN]
    o_ssd_ref, o_ml_ref = refs[N_SSD_IN + N_MLSTM_IN:N_SSD_IN + N_MLSTM_IN + 2]
    extx_ref, extb_ref, extc_ref, state_ref, cstate_ref = refs[N_SSD_IN + N_MLSTM_IN + 2:]
    c = pl.program_id(1)
    _ssd_init(c, extx_ref, extb_ref, extc_ref, state_ref)

    @pl.when(c == 0)
    def _():
        cstate_ref[...] = jnp.zeros(cstate_ref.shape, F32)

    xs_ref, _, b_ref, c_ref = ssd_in[:4]
    for ext_ref, new_ref in ((extx_ref, xs_ref), (extb_ref, b_ref), (extc_ref, c_ref)):
        ext_ref[CONV_HALO:CONV_HALO + STEP_ROWS, :] = new_ref[...]
    for sub in range(CHUNKS_PER_STEP):
        _ssd_body(sub, *ssd_in, o_ssd_ref, extx_ref, extb_ref, extc_ref, state_ref)
        _mlstm_body(sub, *ml_in, o_ml_ref, cstate_ref)


def _recurrent_mixers(ssd_args, mlstm_args):
    s_specs, s_ops, s_scratch = _ssd_operands(*ssd_args)
    m_specs, m_ops, m_scratch = _mlstm_operands(*mlstm_args)
    out_spec = pl.BlockSpec((STEP_ROWS, D_MODEL), lambda b, c: (b * STEPS_PER_SEQ + c, 0))
    out_shape = jax.ShapeDtypeStruct((TOKENS, D_MODEL), BF16)
    return pl.pallas_call(
        _recurrent_mixers_kernel,
        grid=(BATCH, STEPS_PER_SEQ),
        in_specs=s_specs + m_specs,
        out_specs=[out_spec, out_spec],
        out_shape=[out_shape, out_shape],
        scratch_shapes=s_scratch + m_scratch,
        compiler_params=_params(2),
        name="recurrent_mixers",
    )(*s_ops, *m_ops)


MERGE_TM = 512
MERGE_TN = 1024


def _merge_kernel(y0_ref, y1_ref, y2_ref, g0_ref, g1_ref, g2_ref, bg_ref, w_ref, o_ref):
    acc = None
    for b, (y_ref, g_ref) in enumerate(((y0_ref, g0_ref), (y1_ref, g1_ref), (y2_ref, g2_ref))):
        gate = _sigmoid(g_ref[...].astype(F32) + bg_ref[b:b + 1, :])
        term = gate * _dot(y_ref[...], w_ref[b])
        acc = term if acc is None else acc + term
    o_ref[...] = acc.astype(o_ref.dtype)


def _merge(y_pool, y_ssm, y_mlstm, proj_g, b_gate, w_branch):
    tm, tn = MERGE_TM, MERGE_TN
    per_seg = D_MODEL // tn
    y_spec = pl.BlockSpec((tm, D_MODEL), lambda i, j: (i, 0))
    gate = lambda b: pl.BlockSpec((tm, tn), lambda i, j: (i, b * per_seg + j))
    return pl.pallas_call(
        _merge_kernel,
        grid=(TOKENS // tm, D_MODEL // tn),
        in_specs=[y_spec, y_spec, y_spec, gate(0), gate(1), gate(2),
                  pl.BlockSpec((N_BRANCH, tn), lambda i, j: (0, j)),
                  pl.BlockSpec((N_BRANCH, D_MODEL, tn), lambda i, j: (0, 0, j))],
        out_specs=pl.BlockSpec((tm, tn), lambda i, j: (i, j)),
        out_shape=jax.ShapeDtypeStruct((TOKENS, D_MODEL), BF16),
        compiler_params=_params(2),
        name="branch_merge",
    )(y_pool, y_ssm, y_mlstm, proj_g, proj_g, proj_g, b_gate, w_branch)


OUT_TM = 512


def _out_kernel(m_ref, x_ref, w_ref, g_ref, b_ref, o_ref, obf_ref):
    h = ALPHA * x_ref[...] + _dot(m_ref[...], w_ref[...])
    mu = jnp.mean(h, axis=-1, keepdims=True)
    var = jnp.mean(jnp.square(h - mu), axis=-1, keepdims=True)
    y = (h - mu) * lax.rsqrt(var + EPS) * g_ref[...] + b_ref[...]
    o_ref[...] = y
    obf_ref[...] = y.astype(BF16)


def _out_proj(merged, x, w_out, ln_g, ln_b):
    tm = OUT_TM
    row = pl.BlockSpec((tm, D_MODEL), lambda i: (i, 0))
    vec = pl.BlockSpec((1, D_MODEL), lambda i: (0, 0))
    return pl.pallas_call(
        _out_kernel,
        grid=(TOKENS // tm,),
        in_specs=[row, row, pl.BlockSpec((D_MODEL, D_MODEL), lambda i: (0, 0)), vec, vec],
        out_specs=[row, row],
        out_shape=[jax.ShapeDtypeStruct((TOKENS, D_MODEL), F32),
                   jax.ShapeDtypeStruct((TOKENS, D_MODEL), BF16)],
        compiler_params=_params(1),
        name="out_proj_ln",
    )(merged, x, w_out, ln_g, ln_b)


def _lane_bcast(vec):
    return jnp.broadcast_to(vec.astype(F32)[:, None], (vec.shape[0], LANES))


def _layer(layer, x, x_mm, w_in_all, b_gate, w_pool, pool_scale, conv_w, conv_b, dt_bias, a_log,
           d_skip, ssm_norm_w, i_bias, f_bias, mlstm_norm_w, w_branch, w_out, ln_g, ln_b, expand01):
    w_s = _pack_gates(w_in_all, layer)
    small, x_bf = _gates_matmul(x_mm, w_s)
    proj_a = _in_proj(x_bf, w_in_all, layer, 0, OFF_SSM_DT, "in_proj_a")
    proj_m = _in_proj(x_bf, w_in_all, layer, OFF_MLSTM, 5 * D_MODEL, "in_proj_m")
    proj_g = _in_proj(x_bf, w_in_all, layer, OFF_MERGE_GATES, N_BRANCH * D_MODEL, "in_proj_g")

    ssd_cols, ssd_rows, ml_cols, ml_rows, ml_chunk = _gate_scan(
        small, _lane_bcast(dt_bias), _lane_bcast(a_log),
        _lane_bcast(jnp.concatenate([i_bias, f_bias])))

    y_pool = _pool_branch(proj_a, w_pool.astype(BF16), pool_scale.reshape(1, D_MODEL))
    y_ssm, y_mlstm = _recurrent_mixers(
        (proj_a, ssd_cols, ssd_rows, conv_w, conv_b.reshape(1, -1),
         jnp.repeat(d_skip, SSM_HEAD_DIM).reshape(1, D_MODEL),
         ssm_norm_w.reshape(1, D_MODEL), expand01),
        (proj_m, ml_cols, ml_rows, ml_chunk, mlstm_norm_w.reshape(1, D_MODEL)))
    merged = _merge(y_pool, y_ssm, y_mlstm, proj_g, b_gate, w_branch.astype(BF16))
    return _out_proj(merged, x, w_out.astype(BF16), ln_g.reshape(1, D_MODEL), ln_b.reshape(1, D_MODEL))


def kernel(x, w_in, b_gate, w_pool, pool_scale, conv_w, conv_b, dt_bias, a_log, d_skip,
           ssm_norm_w, i_bias, f_bias, mlstm_norm_w, w_branch, w_out, ln_g, ln_b):
    lane = lax.broadcasted_iota(jnp.int32, (3, 2 * LANES, D_MODEL), 1) % LANES
    chan = lax.broadcasted_iota(jnp.int32, (3, 2 * LANES, D_MODEL), 2)
    which = lax.broadcasted_iota(jnp.int32, (3, 2 * LANES, D_MODEL), 0)
    expand01 = (lane == which * SSM_HEADS + chan // SSM_HEAD_DIM).astype(BF16)

    w_in_t = jnp.swapaxes(w_in, 1, 2)
    h = x.reshape(TOKENS, D_MODEL)
    h_mm = h
    for l in range(DEPTH):
        h, h_mm = _layer(l, h, h_mm, w_in_t, b_gate[l], w_pool[l], pool_scale[l], conv_w[l],
                         conv_b[l], dt_bias[l], a_log[l], d_skip[l], ssm_norm_w[l], i_bias[l],
                         f_bias[l], mlstm_norm_w[l], w_branch[l], w_out[l], ln_g[l], ln_b[l],
                         expand01)
    return h.reshape(BATCH, SEQ, D_MODEL)
```

```python
import functools

import jax
import jax.numpy as jnp
from jax import lax
from jax.experimental import pallas as pl
from jax.experimental.pallas import tpu as pltpu

F32 = jnp.float32
BF16 = jnp.bfloat16

D_MODEL = 2048
BATCH = 8
SEQ = 2048
DEPTH = 2
TOKENS = BATCH * SEQ
CHUNK = 128
N_CHUNKS = SEQ // CHUNK
CHUNKS_PER_STEP = 2
STEP_ROWS = CHUNKS_PER_STEP * CHUNK
STEPS_PER_SEQ = N_CHUNKS // CHUNKS_PER_STEP
LANES = 128

POOL_GROUPS = 4
POOL_WINDOWS = (2, 4, 8, 16)
POOL_GDIM = D_MODEL // POOL_GROUPS
POOL_HALO = 16

SSM_HEAD_DIM = 64
SSM_HEADS = D_MODEL // SSM_HEAD_DIM
SSM_GROUPS = 4
SSM_HEADS_PER_GROUP = SSM_HEADS // SSM_GROUPS
SSM_STATE = 128
SSM_CONV = 4
SSM_BC = SSM_GROUPS * SSM_STATE
SSM_GROUP_WIDTH = D_MODEL // SSM_GROUPS
HEADS_PER_TILE = LANES // SSM_HEAD_DIM
CONV_HALO = 16

MLSTM_HEADS = 8
MLSTM_HEAD_DIM = D_MODEL // MLSTM_HEADS
MLSTM_AUG = MLSTM_HEAD_DIM + LANES

N_BRANCH = 3
ALPHA = (2 * DEPTH) ** 0.25
EPS = 1e-5

OFF_SSM_DT = 4 * D_MODEL + 2 * SSM_BC
OFF_MLSTM = OFF_SSM_DT + SSM_HEADS
OFF_MLSTM_GATES = OFF_MLSTM + 5 * D_MODEL
OFF_MERGE_GATES = OFF_MLSTM_GATES + 2 * MLSTM_HEADS
IN_DIM = OFF_MERGE_GATES + N_BRANCH * D_MODEL

SEG_PU, SEG_PZ, SEG_SX, SEG_SZ = range(4)
SEG_SB = 4 * D_MODEL // SSM_BC
SEG_SC = SEG_SB + 1
SEG_MQ, SEG_MK, SEG_MV, SEG_MO, SEG_MZ = range(5)
SMALL_WIDTH = LANES
LANE_DT = 0
LANE_IG = SSM_HEADS
LANE_FG = SSM_HEADS + MLSTM_HEADS

VMEM_LIMIT = 56 * 1024 * 1024


def _params(n_axes):
    return pltpu.CompilerParams(dimension_semantics=("arbitrary",) * n_axes,
                                vmem_limit_bytes=VMEM_LIMIT)


def _sigmoid(v):
    return 0.5 * jnp.tanh(0.5 * v) + 0.5


def _silu(v):
    h = 0.5 * v
    return h * jnp.tanh(h) + h


def _split3(v):
    hi = v.astype(BF16)
    r1 = v - hi.astype(F32)
    mid = r1.astype(BF16)
    lo = (r1 - mid.astype(F32)).astype(BF16)
    return hi, mid, lo


def _dot(a, b):
    return jnp.dot(a, b, preferred_element_type=F32)


def _exact_dot_right01(v, mat01):
    hi, mid, lo = _split3(v)
    return _dot(hi, mat01) + _dot(mid, mat01) + _dot(lo, mat01)


def _tril_mask(n):
    r = lax.broadcasted_iota(jnp.int32, (n, n), 0)
    c = lax.broadcasted_iota(jnp.int32, (n, n), 1)
    return c <= r


def _matmul_kernel(x_ref, w_ref, o_ref):
    o_ref[...] = _dot(x_ref[...], w_ref[...]).astype(o_ref.dtype)


def _matmul(x, w, out_dtype, tm, tn, name):
    m, k = x.shape
    n = w.shape[1]
    return pl.pallas_call(
        _matmul_kernel,
        grid=(m // tm, n // tn),
        in_specs=[pl.BlockSpec((tm, k), lambda i, j: (i, 0)),
                  pl.BlockSpec((k, tn), lambda i, j: (0, j))],
        out_specs=pl.BlockSpec((tm, tn), lambda i, j: (i, j)),
        out_shape=jax.ShapeDtypeStruct((m, n), out_dtype),
        compiler_params=_params(2),
        name=name,
    )(x, w)


PROJ_TM = 2048
PROJ_TN = 1024
GATES_TM = 1024


def _in_proj_kernel(x_ref, wt_ref, o_ref, w_ref):
    @pl.when(pl.program_id(1) == 0)
    def _():
        w_ref[...] = wt_ref[0].T.astype(BF16)

    o_ref[...] = _dot(x_ref[...], w_ref[...]).astype(o_ref.dtype)


def _in_proj(x_bf, w_in_t, layer, col_start, width, name):
    m, k = x_bf.shape
    return pl.pallas_call(
        _in_proj_kernel,
        grid=(width // PROJ_TN, m // PROJ_TM),
        in_specs=[pl.BlockSpec((PROJ_TM, k), lambda j, i: (i, 0)),
                  pl.BlockSpec((pl.Element(1), pl.Element(PROJ_TN), pl.Element(k)),
                               lambda j, i: (layer, pl.multiple_of(col_start + j * PROJ_TN, 16), 0))],
        out_specs=pl.BlockSpec((PROJ_TM, PROJ_TN), lambda j, i: (i, j)),
        out_shape=jax.ShapeDtypeStruct((m, width), BF16),
        scratch_shapes=[pltpu.VMEM((k, PROJ_TN), BF16)],
        compiler_params=_params(2),
        name=name,
    )(x_bf, w_in_t)


def _gates_cast_kernel(x_ref, w_ref, o_ref, xbf_ref):
    xb = x_ref[...].astype(BF16)
    xbf_ref[...] = xb
    o_ref[...] = _dot(xb, w_ref[...])


def _gates_matmul(x, w_s):
    if x.dtype == BF16:
        return _matmul(x, w_s, F32, GATES_TM, SMALL_WIDTH, "in_proj_gates"), x
    m, k = x.shape
    row = pl.BlockSpec((GATES_TM, k), lambda i: (i, 0))
    return pl.pallas_call(
        _gates_cast_kernel,
        grid=(m // GATES_TM,),
        in_specs=[row, pl.BlockSpec((k, SMALL_WIDTH), lambda i: (0, 0))],
        out_specs=[pl.BlockSpec((GATES_TM, SMALL_WIDTH), lambda i: (i, 0)), row],
        out_shape=[jax.ShapeDtypeStruct((m, SMALL_WIDTH), F32), jax.ShapeDtypeStruct((m, k), BF16)],
        compiler_params=_params(1),
        name="in_proj_gates_cast",
    )(x, w_s)


def _pack_gates_kernel(dt_ref, gate_ref, o_ref):
    pad = jnp.zeros((SMALL_WIDTH - SSM_HEADS - 2 * MLSTM_HEADS, D_MODEL), F32)
    rows = jnp.concatenate([dt_ref[0], gate_ref[0], pad], axis=0)
    o_ref[...] = rows.T.astype(BF16)


def _pack_gates(w_in_t, layer):
    rows = lambda n, start: pl.BlockSpec((pl.Element(1), pl.Element(n), pl.Element(D_MODEL)),
                                         lambda i: (layer, start, 0))
    return pl.pallas_call(
        _pack_gates_kernel,
        grid=(1,),
        in_specs=[rows(SSM_HEADS, OFF_SSM_DT), rows(2 * MLSTM_HEADS, OFF_MLSTM_GATES)],
        out_specs=pl.BlockSpec((D_MODEL, LANES), lambda i: (0, 0)),
        out_shape=jax.ShapeDtypeStruct((D_MODEL, SMALL_WIDTH), BF16),
        compiler_params=_params(1),
        name="pack_w_gates",
    )(w_in_t, w_in_t)


POOL_TM = 512


POOL_BAND = CHUNK


def _pool_kernel(u_ref, z_ref, w_ref, scale_ref, band_ref, o_ref, ext_ref):
    t = pl.program_id(1)
    tm = POOL_TM

    @pl.when(t == 0)
    def _():
        ext_ref[0:POOL_HALO, :] = jnp.zeros((POOL_HALO, D_MODEL), BF16)

    @pl.when(t != 0)
    def _():
        ext_ref[0:POOL_HALO, :] = ext_ref[tm:tm + POOL_HALO, :]

    ext_ref[POOL_HALO:POOL_HALO + tm, :] = u_ref[...]

    row = lax.broadcasted_iota(jnp.int32, (POOL_BAND, LANES), 0)
    for g, win in enumerate(POOL_WINDOWS):
        cols = slice(g * POOL_GDIM, (g + 1) * POOL_GDIM)
        pooled = []
        for r in range(tm // POOL_BAND):
            r0 = r * POOL_BAND
            win_sum = _dot(band_ref[g], ext_ref[r0:r0 + POOL_HALO + POOL_BAND, cols])
            cur = u_ref[r0:r0 + POOL_BAND, cols].astype(F32)
            inv_cnt = 1.0 / jnp.minimum(t * tm + r0 + row + 1, win).astype(F32)
            inv_cnt = jnp.concatenate([inv_cnt] * (POOL_GDIM // LANES), axis=1)
            pooled.append((win_sum * inv_cnt - cur).astype(BF16))
        mixed = _dot(jnp.concatenate(pooled, axis=0), w_ref[g])
        y = mixed * scale_ref[:, cols] * _silu(z_ref[:, cols].astype(F32))
        o_ref[:, cols] = y.astype(o_ref.dtype)


def _pool_branch(proj_a, w_pool, pool_scale):
    nt = SEQ // POOL_TM
    shape = (POOL_GROUPS, POOL_BAND, POOL_HALO + POOL_BAND)
    back = (POOL_HALO + lax.broadcasted_iota(jnp.int32, shape, 1)
            - lax.broadcasted_iota(jnp.int32, shape, 2))
    width = jnp.asarray(POOL_WINDOWS, jnp.int32)[:, None, None]
    band01 = ((back >= 0) & (back < width)).astype(BF16)
    return pl.pallas_call(
        _pool_kernel,
        grid=(BATCH, nt),
        in_specs=[pl.BlockSpec((POOL_TM, D_MODEL), lambda b, t: (b * nt + t, SEG_PU)),
                  pl.BlockSpec((POOL_TM, D_MODEL), lambda b, t: (b * nt + t, SEG_PZ)),
                  pl.BlockSpec((POOL_GROUPS, POOL_GDIM, POOL_GDIM), lambda b, t: (0, 0, 0)),
                  pl.BlockSpec((1, D_MODEL), lambda b, t: (0, 0)),
                  pl.BlockSpec(shape, lambda b, t: (0, 0, 0))],
        out_specs=pl.BlockSpec((POOL_TM, D_MODEL), lambda b, t: (b * nt + t, 0)),
        out_shape=jax.ShapeDtypeStruct((TOKENS, D_MODEL), BF16),
        scratch_shapes=[pltpu.VMEM((POOL_HALO + POOL_TM, D_MODEL), BF16)],
        compiler_params=_params(2),
        name="pool_mixer",
    )(proj_a, proj_a, w_pool, pool_scale, band01)


SSD_COL_DT, SSD_COL_DTDECAY, SSD_COL_EXPCS, SSD_COL_CS = (k * SSM_HEADS for k in range(4))
ML_COL_C1, ML_COL_CLAMP = 0, MLSTM_HEADS
ML_ROW_G, ML_ROW_WGT = 0, MLSTM_HEADS
ML_CHUNK_MPREV, ML_CHUNK_SOLD = 0, MLSTM_HEADS


def _gate_scan_kernel(small_ref, dtb_ref, alog_ref, gbias_ref,
                      ssd_cols_ref, ssd_rows_ref, ml_cols_ref, ml_rows_ref, ml_chunk_ref):
    nh = MLSTM_HEADS
    r = lax.broadcasted_iota(jnp.int32, (CHUNK, CHUNK), 0)
    cidx = lax.broadcasted_iota(jnp.int32, (CHUNK, CHUNK), 1)
    triu01 = (r <= cidx).astype(BF16)
    lane = lax.broadcasted_iota(jnp.int32, (nh, CHUNK), 1)
    a_coef = -jnp.exp(alog_ref[...])
    m_prev = jnp.zeros((nh, LANES), F32)
    ml_pad = jnp.zeros((CHUNK - 2 * nh, CHUNK), F32)

    for c in range(N_CHUNKS):
        tok = slice(c * CHUNK, (c + 1) * CHUNK)
        small_t = small_ref[tok, :].T

        dt = jax.nn.softplus(small_t[LANE_DT:LANE_DT + SSM_HEADS, :] + dtb_ref[...])
        a_cs = _exact_dot_right01(dt * a_coef, triu01)
        a_last = a_cs[:, CHUNK - 1:CHUNK]
        decay = jnp.exp(a_last - a_cs)
        ssd_rows_ref[0, :, tok] = a_cs
        ssd_cols_ref[tok, :] = jnp.concatenate([dt, dt * decay, jnp.exp(a_cs), a_cs], axis=0).T

        pre = small_t[LANE_IG:LANE_IG + 2 * nh, :] + gbias_ref[...]
        cum = _exact_dot_right01(jax.nn.log_sigmoid(pre), triu01)
        ig = pre[0:nh, :]
        bcum = cum[nh:2 * nh, :]
        g = ig - bcum
        b_last = bcum[:, CHUNK - 1:CHUNK]
        pmax = g
        shift = 1
        while shift < CHUNK:
            pmax = jnp.maximum(pmax, jnp.where(lane >= shift, pltpu.roll(pmax, shift, axis=1), -jnp.inf))
            shift *= 2
        m_t = jnp.maximum(bcum + pmax, bcum + m_prev)
        w_log = b_last + g
        m_loc = jnp.max(w_log, axis=1, keepdims=True)
        m_new = jnp.maximum(b_last + m_prev, m_loc)
        s_old = jnp.exp(b_last + m_prev - m_new)
        s_loc = jnp.exp(m_loc - m_new)
        wgt = jnp.exp(w_log - m_loc) * (s_loc * MLSTM_HEAD_DIM ** -0.5)
        ml_rows_ref[0, :, tok] = jnp.concatenate([g, wgt], axis=0)
        ml_chunk_ref[0, c] = jnp.concatenate([m_prev, s_old], axis=0)
        ml_cols_ref[tok, :] = jnp.concatenate([bcum - m_t, jnp.exp(-m_t), ml_pad], axis=0).T
        m_prev = m_new


def _gate_scan(small, dt_bias, a_log, gate_bias):
    const2 = lambda b: (0, 0)
    return pl.pallas_call(
        _gate_scan_kernel,
        grid=(BATCH,),
        in_specs=[pl.BlockSpec((SEQ, SMALL_WIDTH), lambda b: (b, 0)),
                  pl.BlockSpec((SSM_HEADS, LANES), const2),
                  pl.BlockSpec((SSM_HEADS, LANES), const2),
                  pl.BlockSpec((2 * MLSTM_HEADS, LANES), const2)],
        out_specs=[pl.BlockSpec((SEQ, LANES), lambda b: (b, 0)),
                   pl.BlockSpec((1, SSM_HEADS, SEQ), lambda b: (b, 0, 0)),
                   pl.BlockSpec((SEQ, LANES), lambda b: (b, 0)),
                   pl.BlockSpec((1, 2 * MLSTM_HEADS, SEQ), lambda b: (b, 0, 0)),
                   pl.BlockSpec((1, N_CHUNKS, 2 * MLSTM_HEADS, LANES), lambda b: (b, 0, 0, 0))],
        out_shape=[jax.ShapeDtypeStruct((TOKENS, LANES), F32),
                   jax.ShapeDtypeStruct((BATCH, SSM_HEADS, SEQ), F32),
                   jax.ShapeDtypeStruct((TOKENS, LANES), F32),
                   jax.ShapeDtypeStruct((BATCH, 2 * MLSTM_HEADS, SEQ), F32),
                   jax.ShapeDtypeStruct((BATCH, N_CHUNKS, 2 * MLSTM_HEADS, LANES), F32)],
        compiler_params=_params(1),
        name="gate_scan",
    )(small, dt_bias, a_log, gate_bias)


def _ssd_init(c, extx_ref, extb_ref, extc_ref, state_ref):
    @pl.when(c == 0)
    def _():
        state_ref[...] = jnp.zeros(state_ref.shape, F32)
        for ext_ref in (extx_ref, extb_ref, extc_ref):
            ext_ref[0:CONV_HALO, :] = jnp.zeros((CONV_HALO, ext_ref.shape[1]), BF16)

    @pl.when(c != 0)
    def _():
        for ext_ref in (extx_ref, extb_ref, extc_ref):
            ext_ref[0:CONV_HALO, :] = ext_ref[STEP_ROWS:STEP_ROWS + CONV_HALO, :]


def _conv_silu(sub, ext_ref, new_ref, w_ref, b_ref, shift_ref):
    window = ext_ref[sub * CHUNK:sub * CHUNK + CONV_HALO + CHUNK, :]
    taps = _dot(shift_ref[...], window)
    cur = new_ref[sub * CHUNK:(sub + 1) * CHUNK, :].astype(F32)
    w_half = 0.5 * w_ref[...]
    half = 0.5 * b_ref[...] + w_half[SSM_CONV - 1:SSM_CONV, :] * cur
    for j in range(1, SSM_CONV):
        half = half + w_half[SSM_CONV - 1 - j:SSM_CONV - j, :] * taps[(j - 1) * CHUNK:j * CHUNK, :]
    return half * jnp.tanh(half) + half


N_SSD_IN = 16
N_MLSTM_IN = 9


def _ssd_body(sub, xs_ref, z_ref, b_ref, c_ref, cols_ref, rows_ref,
              cwx_ref, cwb_ref, cwc_ref, cbx_ref, cbb_ref, cbc_ref,
              dskip_ref, normw_ref, expand_ref, shift_ref,
              o_ref, extx_ref, extb_ref, extc_ref, state_ref):
    tok = slice(sub * CHUNK, (sub + 1) * CHUNK)
    xc = _conv_silu(sub, extx_ref, xs_ref, cwx_ref, cbx_ref, shift_ref)
    bc = _conv_silu(sub, extb_ref, b_ref, cwb_ref, cbb_ref, shift_ref)
    cc = _conv_silu(sub, extc_ref, c_ref, cwc_ref, cbc_ref, shift_ref)

    cols = cols_ref[tok, :]
    cols_hi = cols.astype(BF16)
    cols_mid = (cols - cols_hi.astype(F32)).astype(BF16)
    cols_2 = jnp.concatenate([cols_hi, cols_mid], axis=1)
    dt_e = _dot(cols_2, expand_ref[0])
    dtdecay_e = _dot(cols_2, expand_ref[1])
    exp_cs_e = _dot(cols_2, expand_ref[2])
    chunk_decay_e = exp_cs_e[CHUNK - 1:CHUNK, :]
    a_cs_rows = rows_ref[0, :, tok]
    tril = _tril_mask(CHUNK)
    tile_head = lax.broadcasted_iota(jnp.int32, (CHUNK, LANES), 1) // SSM_HEAD_DIM

    xdt_bf = (xc * dt_e).astype(BF16)
    xdecay_bf = (xc * dtdecay_e).astype(BF16)
    cc_bf = cc.astype(BF16)

    for g in range(SSM_GROUPS):
        ncols = slice(g * SSM_STATE, (g + 1) * SSM_STATE)
        wcols = slice(g * SSM_GROUP_WIDTH, (g + 1) * SSM_GROUP_WIDTH)
        bg_t = bc[:, ncols].T.astype(BF16)
        cg = cc_bf[:, ncols]
        cb = _dot(cg, bg_t)
        prev = state_ref[g]
        y_off = _dot(cg, prev.astype(BF16)) * exp_cs_e[:, wcols]
        state_ref[g] = prev * chunk_decay_e[:, wcols] + _dot(bg_t, xdecay_bf[:, wcols])
        y_pairs = []
        for pair in range(SSM_HEADS_PER_GROUP // HEADS_PER_TILE):
            wts = []
            for r in range(HEADS_PER_TILE):
                h = g * SSM_HEADS_PER_GROUP + pair * HEADS_PER_TILE + r
                seg = cols[:, SSD_COL_CS + h:SSD_COL_CS + h + 1] - a_cs_rows[h:h + 1, :]
                lmat = jnp.exp(jnp.where(tril, seg, -jnp.inf))
                wts.append((cb * lmat).astype(BF16))
            tile0 = g * SSM_GROUP_WIDTH + pair * LANES
            x_tile = xdt_bf[:, tile0:tile0 + LANES]
            x_diag = jnp.concatenate(
                [jnp.where(tile_head == r, x_tile, jnp.zeros_like(x_tile)) for r in range(HEADS_PER_TILE)],
                axis=0)
            y_pairs.append(_dot(jnp.concatenate(wts, axis=1), x_diag))
        y = jnp.concatenate(y_pairs, axis=1) + y_off
        y = y + xc[:, wcols] * dskip_ref[:, wcols]
        y = y * _silu(z_ref[tok, wcols].astype(F32))
        y = y * lax.rsqrt(jnp.mean(y * y, axis=-1, keepdims=True) + EPS)
        o_ref[tok, wcols] = (y * normw_ref[:, wcols]).astype(o_ref.dtype)


def _ssd_operands(proj_a, ssd_cols, ssd_rows, conv_w, conv_b, d_skip, norm_w, expand01):
    row = lambda b, c: b * STEPS_PER_SEQ + c
    const2 = lambda b, c: (0, 0)
    n_shift = (SSM_CONV - 1) * CHUNK
    out_row = lax.broadcasted_iota(jnp.int32, (n_shift, CONV_HALO + CHUNK), 0)
    src_row = lax.broadcasted_iota(jnp.int32, (n_shift, CONV_HALO + CHUNK), 1)
    shift01 = (src_row == CONV_HALO + out_row % CHUNK - out_row // CHUNK - 1).astype(BF16)
    cwx, cwb, cwc = conv_w[:, :D_MODEL], conv_w[:, D_MODEL:D_MODEL + SSM_BC], conv_w[:, D_MODEL + SSM_BC:]
    cbx, cbb, cbc = conv_b[:, :D_MODEL], conv_b[:, D_MODEL:D_MODEL + SSM_BC], conv_b[:, D_MODEL + SSM_BC:]
    in_specs = [pl.BlockSpec((STEP_ROWS, D_MODEL), lambda b, c: (row(b, c), SEG_SX)),
                pl.BlockSpec((STEP_ROWS, D_MODEL), lambda b, c: (row(b, c), SEG_SZ)),
                pl.BlockSpec((STEP_ROWS, SSM_BC), lambda b, c: (row(b, c), SEG_SB)),
                pl.BlockSpec((STEP_ROWS, SSM_BC), lambda b, c: (row(b, c), SEG_SC)),
                pl.BlockSpec((STEP_ROWS, LANES), lambda b, c: (row(b, c), 0)),
                pl.BlockSpec((1, SSM_HEADS, STEP_ROWS), lambda b, c: (b, 0, c)),
                pl.BlockSpec((SSM_CONV, D_MODEL), const2),
                pl.BlockSpec((SSM_CONV, SSM_BC), const2),
                pl.BlockSpec((SSM_CONV, SSM_BC), const2),
                pl.BlockSpec((1, D_MODEL), const2),
                pl.BlockSpec((1, SSM_BC), const2),
                pl.BlockSpec((1, SSM_BC), const2),
                pl.BlockSpec((1, D_MODEL), const2),
                pl.BlockSpec((1, D_MODEL), const2),
                pl.BlockSpec((3, 2 * LANES, D_MODEL), lambda b, c: (0, 0, 0)),
                pl.BlockSpec((n_shift, CONV_HALO + CHUNK), const2)]
    operands = [proj_a, proj_a, proj_a, proj_a, ssd_cols, ssd_rows, cwx, cwb, cwc, cbx, cbb, cbc,
                d_skip, norm_w, expand01, shift01]
    scratch = [pltpu.VMEM((CONV_HALO + STEP_ROWS, D_MODEL), BF16),
               pltpu.VMEM((CONV_HALO + STEP_ROWS, SSM_BC), BF16),
               pltpu.VMEM((CONV_HALO + STEP_ROWS, SSM_BC), BF16),
               pltpu.VMEM((SSM_GROUPS, SSM_STATE, SSM_GROUP_WIDTH), F32)]
    assert len(in_specs) == len(operands) == N_SSD_IN
    return in_specs, operands, scratch


def _mlstm_body(sub, q_ref, k_ref, v_ref, og_ref, z_ref, cols_ref, rows_ref, chunk_ref, normw_ref,
                o_ref, cstate_ref):
    tok = slice(sub * CHUNK, (sub + 1) * CHUNK)
    nh = MLSTM_HEADS
    n_tiles = MLSTM_AUG // LANES
    v_tiles = MLSTM_HEAD_DIM // LANES

    cols_t = cols_ref[tok, :]
    rows = rows_ref[0, :, tok]
    chunk = chunk_ref[0, sub]
    tril = _tril_mask(CHUNK)
    ones_tile = jnp.ones((CHUNK, LANES), BF16)
    mean_mat = jnp.full((MLSTM_HEAD_DIM, LANES), 1.0 / MLSTM_HEAD_DIM, BF16)
    k_scale = MLSTM_HEAD_DIM ** -0.5

    for h in range(nh):
        cols = slice(h * MLSTM_HEAD_DIM, (h + 1) * MLSTM_HEAD_DIM)
        qh = q_ref[tok, cols]
        k_t = k_ref[tok, cols].T
        v_aug = jnp.concatenate([v_ref[tok, cols], ones_tile], axis=1)
        c_prev = cstate_ref[h]
        g_row = rows[ML_ROW_G + h:ML_ROW_G + h + 1, :]
        wgt_row = rows[ML_ROW_WGT + h:ML_ROW_WGT + h + 1, :]
        m_prev_row = chunk[ML_CHUNK_MPREV + h:ML_CHUNK_MPREV + h + 1, :]
        s_old_row = chunk[ML_CHUNK_SOLD + h:ML_CHUNK_SOLD + h + 1, :]

        c1_d = jnp.broadcast_to(cols_t[:, ML_COL_C1 + h:ML_COL_C1 + h + 1], (CHUNK, CHUNK))
        clamp_d = jnp.broadcast_to(cols_t[:, ML_COL_CLAMP + h:ML_COL_CLAMP + h + 1], (CHUNK, CHUNK))
        dw = jnp.exp(jnp.where(tril, c1_d + g_row, -jnp.inf))
        inter_w = jnp.exp(c1_d + m_prev_row)

        s = _dot(qh, k_t) * k_scale
        intra = _dot((s * dw).astype(BF16), v_aug)
        inter = _dot(qh, c_prev.astype(BF16))
        nd = [intra[:, j * LANES:(j + 1) * LANES] + inter_w * inter[:, j * LANES:(j + 1) * LANES]
              for j in range(n_tiles)]
        rden = 1.0 / jnp.maximum(jnp.abs(nd[-1]), clamp_d)

        wk_t = (k_t.astype(F32) * wgt_row).astype(BF16)
        cstate_ref[h] = (c_prev * jnp.concatenate([s_old_row] * n_tiles, axis=1)
                         + _dot(wk_t, v_aug))

        og = _sigmoid(og_ref[tok, cols].astype(F32))
        hc = jnp.concatenate([nd[j] * rden for j in range(v_tiles)], axis=1) * og
        mu = _dot(hc.astype(BF16), mean_mat)
        dev = hc - jnp.concatenate([mu] * v_tiles, axis=1)
        var = _dot((dev * dev).astype(BF16), mean_mat)
        y = dev * jnp.concatenate([lax.rsqrt(var + EPS)] * v_tiles, axis=1)
        y = y * normw_ref[:, cols] * _silu(z_ref[tok, cols].astype(F32))
        o_ref[tok, cols] = y.astype(o_ref.dtype)


def _mlstm_operands(proj_m, ml_cols, ml_rows, ml_chunk, norm_w):
    row = lambda b, c: b * STEPS_PER_SEQ + c
    seg = lambda s: pl.BlockSpec((STEP_ROWS, D_MODEL), lambda b, c: (row(b, c), s))
    in_specs = [seg(SEG_MQ), seg(SEG_MK), seg(SEG_MV), seg(SEG_MO), seg(SEG_MZ),
                pl.BlockSpec((STEP_ROWS, LANES), lambda b, c: (row(b, c), 0)),
                pl.BlockSpec((1, 2 * MLSTM_HEADS, STEP_ROWS), lambda b, c: (b, 0, c)),
                pl.BlockSpec((1, CHUNKS_PER_STEP, 2 * MLSTM_HEADS, LANES), lambda b, c: (b, c, 0, 0)),
                pl.BlockSpec((1, D_MODEL), lambda b, c: (0, 0))]
    operands = [proj_m, proj_m, proj_m, proj_m, proj_m, ml_cols, ml_rows, ml_chunk, norm_w]
    scratch = [pltpu.VMEM((MLSTM_HEADS, MLSTM_HEAD_DIM, MLSTM_AUG), F32)]
    assert len(in_specs) == len(operands) == N_MLSTM_IN
    return in_specs, operands, scratch


def _recurrent_mixers_kernel(*refs):
    ssd_in = refs[:N_SSD_IN]
    ml_in = refs[N_SSD_IN:N_SSD_IN + N_MLSTM_IN]
    o_ssd_ref, o_ml_ref = refs[N_SSD_IN + N_MLSTM_IN:N_SSD_IN + N_MLSTM_IN + 2]
    extx_ref, extb_ref, extc_ref, state_ref, cstate_ref = refs[N_SSD_IN + N_MLSTM_IN + 2:]
    c = pl.program_id(1)
    _ssd_init(c, extx_ref, extb_ref, extc_ref, state_ref)

    @pl.when(c == 0)
    def _():
        cstate_ref[...] = jnp.zeros(cstate_ref.shape, F32)

    xs_ref, _, b_ref, c_ref = ssd_in[:4]
    for ext_ref, new_ref in ((extx_ref, xs_ref), (extb_ref, b_ref), (extc_ref, c_ref)):
        ext_ref[CONV_HALO:CONV_HALO + STEP_ROWS, :] = new_ref[...]
    for sub in range(CHUNKS_PER_STEP):
        _ssd_body(sub, *ssd_in, o_ssd_ref, extx_ref, extb_ref, extc_ref, state_ref)
        _mlstm_body(sub, *ml_in, o_ml_ref, cstate_ref)


def _recurrent_mixers(ssd_args, mlstm_args):
    s_specs, s_ops, s_scratch = _ssd_operands(*ssd_args)
    m_specs, m_ops, m_scratch = _mlstm_operands(*mlstm_args)
    out_spec = pl.BlockSpec((STEP_ROWS, D_MODEL), lambda b, c: (b * STEPS_PER_SEQ + c, 0))
    out_shape = jax.ShapeDtypeStruct((TOKENS, D_MODEL), BF16)
    return pl.pallas_call(
        _recurrent_mixers_kernel,
        grid=(BATCH, STEPS_PER_SEQ),
        in_specs=s_specs + m_specs,
        out_specs=[out_spec, out_spec],
        out_shape=[out_shape, out_shape],
        scratch_shapes=s_scratch + m_scratch,
        compiler_params=_params(2),
        name="recurrent_mixers",
    )(*s_ops, *m_ops)


MERGE_TM = 512
MERGE_TN = 1024


def _merge_kernel(y0_ref, y1_ref, y2_ref, g0_ref, g1_ref, g2_ref, bg_ref, w_ref, o_ref):
    acc = None
    for b, (y_ref, g_ref) in enumerate(((y0_ref, g0_ref), (y1_ref, g1_ref), (y2_ref, g2_ref))):
        gate = _sigmoid(g_ref[...].astype(F32) + bg_ref[b:b + 1, :])
        term = gate * _dot(y_ref[...], w_ref[b])
        acc = term if acc is None else acc + term
    o_ref[...] = acc.astype(o_ref.dtype)


def _merge(layer, y_pool, y_ssm, y_mlstm, proj_g, b_gate, w_branch_all):
    tm, tn = MERGE_TM, MERGE_TN
    per_seg = D_MODEL // tn
    y_spec = pl.BlockSpec((tm, D_MODEL), lambda i, j: (i, 0))
    gate = lambda b: pl.BlockSpec((tm, tn), lambda i, j: (i, b * per_seg + j))
    return pl.pallas_call(
        _merge_kernel,
        grid=(TOKENS // tm, D_MODEL // tn),
        in_specs=[y_spec, y_spec, y_spec, gate(0), gate(1), gate(2),
                  pl.BlockSpec((N_BRANCH, tn), lambda i, j: (0, j)),
                  pl.BlockSpec((None, N_BRANCH, D_MODEL, tn), lambda i, j: (layer, 0, 0, j))],
        out_specs=pl.BlockSpec((tm, tn), lambda i, j: (i, j)),
        out_shape=jax.ShapeDtypeStruct((TOKENS, D_MODEL), BF16),
        compiler_params=_params(2),
        name="branch_merge",
    )(y_pool, y_ssm, y_mlstm, proj_g, proj_g, proj_g, b_gate, w_branch_all)


OUT_TM = 512


def _out_kernel(m_ref, x_ref, w_ref, g_ref, b_ref, o_ref, obf_ref):
    half = OUT_TM // 2
    for r in range(2):
        rows = slice(r * half, (r + 1) * half)
        h = ALPHA * x_ref[rows, :] + _dot(m_ref[rows, :], w_ref[...])
        mu = jnp.mean(h, axis=-1, keepdims=True)
        var = jnp.mean(jnp.square(h - mu), axis=-1, keepdims=True)
        y = (h - mu) * lax.rsqrt(var + EPS) * g_ref[...] + b_ref[...]
        o_ref[rows, :] = y
        obf_ref[rows, :] = y.astype(BF16)


def _out_proj(layer, merged, x, w_out_all, ln_g, ln_b):
    tm = OUT_TM
    row = pl.BlockSpec((tm, D_MODEL), lambda i: (i, 0))
    vec = pl.BlockSpec((1, D_MODEL), lambda i: (0, 0))
    return pl.pallas_call(
        _out_kernel,
        grid=(TOKENS // tm,),
        in_specs=[row, row, pl.BlockSpec((None, D_MODEL, D_MODEL), lambda i: (layer, 0, 0)), vec, vec],
        out_specs=[row, row],
        out_shape=[jax.ShapeDtypeStruct((TOKENS, D_MODEL), F32),
                   jax.ShapeDtypeStruct((TOKENS, D_MODEL), BF16)],
        compiler_params=_params(1),
        name="out_proj_ln",
    )(merged, x, w_out_all, ln_g, ln_b)


def _lane_bcast(vec):
    return jnp.broadcast_to(vec.astype(F32)[:, None], (vec.shape[0], LANES))


def _layer(layer, x, x_mm, w_in_all, b_gate, w_pool, pool_scale, conv_w, conv_b, dt_bias, a_log,
           d_skip, ssm_norm_w, i_bias, f_bias, mlstm_norm_w, w_branch, w_out, ln_g, ln_b, expand01):
    w_s = _pack_gates(w_in_all, layer)
    small, x_bf = _gates_matmul(x_mm, w_s)
    proj_a = _in_proj(x_bf, w_in_all, layer, 0, OFF_SSM_DT, "in_proj_a")
    proj_m = _in_proj(x_bf, w_in_all, layer, OFF_MLSTM, 5 * D_MODEL, "in_proj_m")
    proj_g = _in_proj(x_bf, w_in_all, layer, OFF_MERGE_GATES, N_BRANCH * D_MODEL, "in_proj_g")

    ssd_cols, ssd_rows, ml_cols, ml_rows, ml_chunk = _gate_scan(
        small, _lane_bcast(dt_bias), _lane_bcast(a_log),
        _lane_bcast(jnp.concatenate([i_bias, f_bias])))

    y_pool = _pool_branch(proj_a, w_pool.astype(BF16), pool_scale.reshape(1, D_MODEL))
    y_ssm, y_mlstm = _recurrent_mixers(
        (proj_a, ssd_cols, ssd_rows, conv_w, conv_b.reshape(1, -1),
         jnp.repeat(d_skip, SSM_HEAD_DIM).reshape(1, D_MODEL),
         ssm_norm_w.reshape(1, D_MODEL), expand01),
        (proj_m, ml_cols, ml_rows, ml_chunk, mlstm_norm_w.reshape(1, D_MODEL)))
    merged = _merge(layer, y_pool, y_ssm, y_mlstm, proj_g, b_gate, w_branch)
    return _out_proj(layer, merged, x, w_out, ln_g.reshape(1, D_MODEL), ln_b.reshape(1, D_MODEL))


def kernel(x, w_in, b_gate, w_pool, pool_scale, conv_w, conv_b, dt_bias, a_log, d_skip,
           ssm_norm_w, i_bias, f_bias, mlstm_norm_w, w_branch, w_out, ln_g, ln_b):
    lane = lax.broadcasted_iota(jnp.int32, (3, 2 * LANES, D_MODEL), 1) % LANES
    chan = lax.broadcasted_iota(jnp.int32, (3, 2 * LANES, D_MODEL), 2)
    which = lax.broadcasted_iota(jnp.int32, (3, 2 * LANES, D_MODEL), 0)
    expand01 = (lane == which * SSM_HEADS + chan // SSM_HEAD_DIM).astype(BF16)

    w_in_t = jnp.swapaxes(w_in, 1, 2)
    w_branch_bf = w_branch.astype(BF16)
    w_out_bf = w_out.astype(BF16)
    h = x.reshape(TOKENS, D_MODEL)
    h_mm = h
    for l in range(DEPTH):
        h, h_mm = _layer(l, h, h_mm, w_in_t, b_gate[l], w_pool[l], pool_scale[l], conv_w[l],
                         conv_b[l], dt_bias[l], a_log[l], d_skip[l], ssm_norm_w[l], i_bias[l],
                         f_bias[l], mlstm_norm_w[l], w_branch_bf, w_out_bf, ln_g[l], ln_b[l],
                         expand01)
    return h.reshape(BATCH, SEQ, D_MODEL)
```

```python
import jax
import jax.numpy as jnp
from jax import lax
from jax.experimental import pallas as pl
from jax.experimental.pallas import tpu as pltpu

F32 = jnp.float32
BF16 = jnp.bfloat16

D_MODEL = 2048
BATCH = 8
SEQ = 2048
DEPTH = 2
TOKENS = BATCH * SEQ
CHUNK = 128
N_CHUNKS = SEQ // CHUNK
CHUNKS_PER_STEP = 2
STEP_ROWS = CHUNKS_PER_STEP * CHUNK
STEPS_PER_SEQ = N_CHUNKS // CHUNKS_PER_STEP
LANES = 128

POOL_GROUPS = 4
POOL_WINDOWS = (2, 4, 8, 16)
POOL_GDIM = D_MODEL // POOL_GROUPS
POOL_HALO = 16

SSM_HEAD_DIM = 64
SSM_HEADS = D_MODEL // SSM_HEAD_DIM
SSM_GROUPS = 4
SSM_HEADS_PER_GROUP = SSM_HEADS // SSM_GROUPS
SSM_STATE = 128
SSM_CONV = 4
SSM_BC = SSM_GROUPS * SSM_STATE
SSM_GROUP_WIDTH = D_MODEL // SSM_GROUPS
HEADS_PER_TILE = LANES // SSM_HEAD_DIM
CONV_HALO = 16

MLSTM_HEADS = 8
MLSTM_HEAD_DIM = D_MODEL // MLSTM_HEADS
MLSTM_AUG = MLSTM_HEAD_DIM + LANES

N_BRANCH = 3
ALPHA = (2 * DEPTH) ** 0.25
EPS = 1e-5

OFF_SSM_DT = 4 * D_MODEL + 2 * SSM_BC
OFF_MLSTM = OFF_SSM_DT + SSM_HEADS
OFF_MLSTM_GATES = OFF_MLSTM + 5 * D_MODEL
OFF_MERGE_GATES = OFF_MLSTM_GATES + 2 * MLSTM_HEADS
IN_DIM = OFF_MERGE_GATES + N_BRANCH * D_MODEL

SEG_PU, SEG_PZ, SEG_SX, SEG_SZ = range(4)
SEG_SB = 4 * D_MODEL // SSM_BC
SEG_SC = SEG_SB + 1
SEG_MQ, SEG_MK, SEG_MV, SEG_MO, SEG_MZ = range(5)
SMALL_WIDTH = LANES
LANE_DT = 0
LANE_IG = SSM_HEADS
LANE_FG = SSM_HEADS + MLSTM_HEADS

VMEM_LIMIT = 56 * 1024 * 1024


def _params(n_axes):
    return pltpu.CompilerParams(dimension_semantics=("arbitrary",) * n_axes,
                                vmem_limit_bytes=VMEM_LIMIT)


def _sigmoid(v):
    return 0.5 * jnp.tanh(0.5 * v) + 0.5


def _silu(v):
    h = 0.5 * v
    return h * jnp.tanh(h) + h


def _split3(v):
    hi = v.astype(BF16)
    r1 = v - hi.astype(F32)
    mid = r1.astype(BF16)
    lo = (r1 - mid.astype(F32)).astype(BF16)
    return hi, mid, lo


def _dot(a, b):
    return jnp.dot(a, b, preferred_element_type=F32)


def _exact_dot_right01(v, mat01):
    hi, mid, lo = _split3(v)
    return _dot(hi, mat01) + _dot(mid, mat01) + _dot(lo, mat01)


def _tril_mask(n):
    r = lax.broadcasted_iota(jnp.int32, (n, n), 0)
    c = lax.broadcasted_iota(jnp.int32, (n, n), 1)
    return c <= r


def _matmul_kernel(x_ref, w_ref, o_ref):
    o_ref[...] = _dot(x_ref[...], w_ref[...]).astype(o_ref.dtype)


def _matmul(x, w, out_dtype, tm, tn, name):
    m, k = x.shape
    n = w.shape[1]
    return pl.pallas_call(
        _matmul_kernel,
        grid=(m // tm, n // tn),
        in_specs=[pl.BlockSpec((tm, k), lambda i, j: (i, 0)),
                  pl.BlockSpec((k, tn), lambda i, j: (0, j))],
        out_specs=pl.BlockSpec((tm, tn), lambda i, j: (i, j)),
        out_shape=jax.ShapeDtypeStruct((m, n), out_dtype),
        compiler_params=_params(2),
        name=name,
    )(x, w)


PROJ_TM = 2048
PROJ_TN = 1024
GATES_TM = 1024


def _in_proj_kernel(x_ref, wt_ref, o_ref, w_ref):
    @pl.when(pl.program_id(1) == 0)
    def _():
        w_ref[...] = wt_ref[0].T.astype(BF16)

    o_ref[...] = _dot(x_ref[...], w_ref[...]).astype(o_ref.dtype)


def _in_proj(x_bf, w_in_t, layer, col_start, width, name):
    m, k = x_bf.shape
    return pl.pallas_call(
        _in_proj_kernel,
        grid=(width // PROJ_TN, m // PROJ_TM),
        in_specs=[pl.BlockSpec((PROJ_TM, k), lambda j, i: (i, 0)),
                  pl.BlockSpec((pl.Element(1), pl.Element(PROJ_TN), pl.Element(k)),
                               lambda j, i: (layer, pl.multiple_of(col_start + j * PROJ_TN, 16), 0))],
        out_specs=pl.BlockSpec((PROJ_TM, PROJ_TN), lambda j, i: (i, j)),
        out_shape=jax.ShapeDtypeStruct((m, width), BF16),
        scratch_shapes=[pltpu.VMEM((k, PROJ_TN), BF16)],
        compiler_params=_params(2),
        name=name,
    )(x_bf, w_in_t)


def _gates_cast_kernel(x_ref, w_ref, o_ref, xbf_ref):
    xb = x_ref[...].astype(BF16)
    xbf_ref[...] = xb
    o_ref[...] = _dot(xb, w_ref[...])


def _gates_matmul(x, w_s):
    if x.dtype == BF16:
        return _matmul(x, w_s, F32, GATES_TM, SMALL_WIDTH, "in_proj_gates"), x
    m, k = x.shape
    row = pl.BlockSpec((GATES_TM, k), lambda i: (i, 0))
    return pl.pallas_call(
        _gates_cast_kernel,
        grid=(m // GATES_TM,),
        in_specs=[row, pl.BlockSpec((k, SMALL_WIDTH), lambda i: (0, 0))],
        out_specs=[pl.BlockSpec((GATES_TM, SMALL_WIDTH), lambda i: (i, 0)), row],
        out_shape=[jax.ShapeDtypeStruct((m, SMALL_WIDTH), F32), jax.ShapeDtypeStruct((m, k), BF16)],
        compiler_params=_params(1),
        name="in_proj_gates_cast",
    )(x, w_s)


def _pack_gates_kernel(dt_ref, gate_ref, o_ref):
    pad = jnp.zeros((SMALL_WIDTH - SSM_HEADS - 2 * MLSTM_HEADS, D_MODEL), F32)
    rows = jnp.concatenate([dt_ref[0], gate_ref[0], pad], axis=0)
    o_ref[...] = rows.T.astype(BF16)


def _pack_gates(w_in_t, layer):
    rows = lambda n, start: pl.BlockSpec((pl.Element(1), pl.Element(n), pl.Element(D_MODEL)),
                                         lambda i: (layer, start, 0))
    return pl.pallas_call(
        _pack_gates_kernel,
        grid=(1,),
        in_specs=[rows(SSM_HEADS, OFF_SSM_DT), rows(2 * MLSTM_HEADS, OFF_MLSTM_GATES)],
        out_specs=pl.BlockSpec((D_MODEL, LANES), lambda i: (0, 0)),
        out_shape=jax.ShapeDtypeStruct((D_MODEL, SMALL_WIDTH), BF16),
        compiler_params=_params(1),
        name="pack_w_gates",
    )(w_in_t, w_in_t)


POOL_TM = 512


POOL_BAND = CHUNK


def _pool_kernel(u_ref, z_ref, w_ref, scale_ref, band_ref, o_ref, ext_ref):
    t = pl.program_id(1)
    tm = POOL_TM

    @pl.when(t == 0)
    def _():
        ext_ref[0:POOL_HALO, :] = jnp.zeros((POOL_HALO, D_MODEL), BF16)

    @pl.when(t != 0)
    def _():
        ext_ref[0:POOL_HALO, :] = ext_ref[tm:tm + POOL_HALO, :]

    ext_ref[POOL_HALO:POOL_HALO + tm, :] = u_ref[...]

    row = lax.broadcasted_iota(jnp.int32, (POOL_BAND, LANES), 0)
    for g, win in enumerate(POOL_WINDOWS):
        cols = slice(g * POOL_GDIM, (g + 1) * POOL_GDIM)
        pooled = []
        for r in range(tm // POOL_BAND):
            r0 = r * POOL_BAND
            win_sum = _dot(band_ref[g], ext_ref[r0:r0 + POOL_HALO + POOL_BAND, cols])
            cur = u_ref[r0:r0 + POOL_BAND, cols].astype(F32)
            inv_cnt = 1.0 / jnp.minimum(t * tm + r0 + row + 1, win).astype(F32)
            inv_cnt = jnp.concatenate([inv_cnt] * (POOL_GDIM // LANES), axis=1)
            pooled.append((win_sum * inv_cnt - cur).astype(BF16))
        mixed = _dot(jnp.concatenate(pooled, axis=0), w_ref[g])
        y = mixed * scale_ref[:, cols] * _silu(z_ref[:, cols].astype(F32))
        o_ref[:, cols] = y.astype(o_ref.dtype)


def _pool_branch(proj_a, w_pool, pool_scale):
    nt = SEQ // POOL_TM
    shape = (POOL_GROUPS, POOL_BAND, POOL_HALO + POOL_BAND)
    back = (POOL_HALO + lax.broadcasted_iota(jnp.int32, shape, 1)
            - lax.broadcasted_iota(jnp.int32, shape, 2))
    width = jnp.asarray(POOL_WINDOWS, jnp.int32)[:, None, None]
    band01 = ((back >= 0) & (back < width)).astype(BF16)
    return pl.pallas_call(
        _pool_kernel,
        grid=(BATCH, nt),
        in_specs=[pl.BlockSpec((POOL_TM, D_MODEL), lambda b, t: (b * nt + t, SEG_PU)),
                  pl.BlockSpec((POOL_TM, D_MODEL), lambda b, t: (b * nt + t, SEG_PZ)),
                  pl.BlockSpec((POOL_GROUPS, POOL_GDIM, POOL_GDIM), lambda b, t: (0, 0, 0)),
                  pl.BlockSpec((1, D_MODEL), lambda b, t: (0, 0)),
                  pl.BlockSpec(shape, lambda b, t: (0, 0, 0))],
        out_specs=pl.BlockSpec((POOL_TM, D_MODEL), lambda b, t: (b * nt + t, 0)),
        out_shape=jax.ShapeDtypeStruct((TOKENS, D_MODEL), BF16),
        scratch_shapes=[pltpu.VMEM((POOL_HALO + POOL_TM, D_MODEL), BF16)],
        compiler_params=_params(2),
        name="pool_mixer",
    )(proj_a, proj_a, w_pool, pool_scale, band01)


SSD_COL_DT, SSD_COL_DTDECAY, SSD_COL_EXPCS, SSD_COL_CS = (k * SSM_HEADS for k in range(4))
ML_COL_C1, ML_COL_CLAMP = 0, MLSTM_HEADS
ML_ROW_G, ML_ROW_WGT = 0, MLSTM_HEADS
ML_CHUNK_MPREV, ML_CHUNK_SOLD = 0, MLSTM_HEADS


def _gate_scan_kernel(small_ref, dtb_ref, alog_ref, gbias_ref,
                      ssd_cols_ref, ssd_rows_ref, ml_cols_ref, ml_rows_ref, ml_chunk_ref):
    nh = MLSTM_HEADS
    r = lax.broadcasted_iota(jnp.int32, (CHUNK, CHUNK), 0)
    cidx = lax.broadcasted_iota(jnp.int32, (CHUNK, CHUNK), 1)
    triu01 = (r <= cidx).astype(BF16)
    lane = lax.broadcasted_iota(jnp.int32, (nh, CHUNK), 1)
    a_coef = -jnp.exp(alog_ref[...])
    m_prev = jnp.zeros((nh, LANES), F32)
    ml_pad = jnp.zeros((CHUNK - 2 * nh, CHUNK), F32)

    for c in range(N_CHUNKS):
        tok = slice(c * CHUNK, (c + 1) * CHUNK)
        small_t = small_ref[tok, :].T

        dt = jax.nn.softplus(small_t[LANE_DT:LANE_DT + SSM_HEADS, :] + dtb_ref[...])
        a_cs = _exact_dot_right01(dt * a_coef, triu01)
        a_last = a_cs[:, CHUNK - 1:CHUNK]
        decay = jnp.exp(a_last - a_cs)
        ssd_rows_ref[0, :, tok] = a_cs
        ssd_cols_ref[tok, :] = jnp.concatenate([dt, dt * decay, jnp.exp(a_cs), a_cs], axis=0).T

        pre = small_t[LANE_IG:LANE_IG + 2 * nh, :] + gbias_ref[...]
        cum = _exact_dot_right01(jax.nn.log_sigmoid(pre), triu01)
        ig = pre[0:nh, :]
        bcum = cum[nh:2 * nh, :]
        g = ig - bcum
        b_last = bcum[:, CHUNK - 1:CHUNK]
        pmax = g
        shift = 1
        while shift < CHUNK:
            pmax = jnp.maximum(pmax, jnp.where(lane >= shift, pltpu.roll(pmax, shift, axis=1), -jnp.inf))
            shift *= 2
        m_t = jnp.maximum(bcum + pmax, bcum + m_prev)
        w_log = b_last + g
        m_loc = jnp.max(w_log, axis=1, keepdims=True)
        m_new = jnp.maximum(b_last + m_prev, m_loc)
        s_old = jnp.exp(b_last + m_prev - m_new)
        s_loc = jnp.exp(m_loc - m_new)
        wgt = jnp.exp(w_log - m_loc) * (s_loc * MLSTM_HEAD_DIM ** -0.5)
        ml_rows_ref[0, :, tok] = jnp.concatenate([g, wgt], axis=0)
        ml_chunk_ref[0, c] = jnp.concatenate([m_prev, s_old], axis=0)
        ml_cols_ref[tok, :] = jnp.concatenate([bcum - m_t, jnp.exp(-m_t), ml_pad], axis=0).T
        m_prev = m_new


def _gate_scan(small, dt_bias, a_log, gate_bias):
    const2 = lambda b: (0, 0)
    return pl.pallas_call(
        _gate_scan_kernel,
        grid=(BATCH,),
        in_specs=[pl.BlockSpec((SEQ, SMALL_WIDTH), lambda b: (b, 0)),
                  pl.BlockSpec((SSM_HEADS, LANES), const2),
                  pl.BlockSpec((SSM_HEADS, LANES), const2),
                  pl.BlockSpec((2 * MLSTM_HEADS, LANES), const2)],
        out_specs=[pl.BlockSpec((SEQ, LANES), lambda b: (b, 0)),
                   pl.BlockSpec((1, SSM_HEADS, SEQ), lambda b: (b, 0, 0)),
                   pl.BlockSpec((SEQ, LANES), lambda b: (b, 0)),
                   pl.BlockSpec((1, 2 * MLSTM_HEADS, SEQ), lambda b: (b, 0, 0)),
                   pl.BlockSpec((1, N_CHUNKS, 2 * MLSTM_HEADS, LANES), lambda b: (b, 0, 0, 0))],
        out_shape=[jax.ShapeDtypeStruct((TOKENS, LANES), F32),
                   jax.ShapeDtypeStruct((BATCH, SSM_HEADS, SEQ), F32),
                   jax.ShapeDtypeStruct((TOKENS, LANES), F32),
                   jax.ShapeDtypeStruct((BATCH, 2 * MLSTM_HEADS, SEQ), F32),
                   jax.ShapeDtypeStruct((BATCH, N_CHUNKS, 2 * MLSTM_HEADS, LANES), F32)],
        compiler_params=_params(1),
        name="gate_scan",
    )(small, dt_bias, a_log, gate_bias)


def _ssd_init(c, extx_ref, extb_ref, extc_ref, state_ref):
    @pl.when(c == 0)
    def _():
        state_ref[...] = jnp.zeros(state_ref.shape, F32)
        for ext_ref in (extx_ref, extb_ref, extc_ref):
            ext_ref[0:CONV_HALO, :] = jnp.zeros((CONV_HALO, ext_ref.shape[1]), BF16)

    @pl.when(c != 0)
    def _():
        for ext_ref in (extx_ref, extb_ref, extc_ref):
            ext_ref[0:CONV_HALO, :] = ext_ref[STEP_ROWS:STEP_ROWS + CONV_HALO, :]


def _conv_silu(sub, ext_ref, new_ref, w_ref, b_ref, shift_ref):
    window = ext_ref[sub * CHUNK:sub * CHUNK + CONV_HALO + CHUNK, :]
    taps = _dot(shift_ref[...], window)
    cur = new_ref[sub * CHUNK:(sub + 1) * CHUNK, :].astype(F32)
    w_half = 0.5 * w_ref[...]
    half = 0.5 * b_ref[...] + w_half[SSM_CONV - 1:SSM_CONV, :] * cur
    for j in range(1, SSM_CONV):
        half = half + w_half[SSM_CONV - 1 - j:SSM_CONV - j, :] * taps[(j - 1) * CHUNK:j * CHUNK, :]
    return half * jnp.tanh(half) + half


N_SSD_IN = 16
N_MLSTM_IN = 9


def _ssd_body(sub, xs_ref, z_ref, b_ref, c_ref, cols_ref, rows_ref,
              cwx_ref, cwb_ref, cwc_ref, cbx_ref, cbb_ref, cbc_ref,
              dskip_ref, normw_ref, expand_ref, shift_ref,
              o_ref, extx_ref, extb_ref, extc_ref, state_ref):
    tok = slice(sub * CHUNK, (sub + 1) * CHUNK)
    xc = _conv_silu(sub, extx_ref, xs_ref, cwx_ref, cbx_ref, shift_ref)
    bc = _conv_silu(sub, extb_ref, b_ref, cwb_ref, cbb_ref, shift_ref)
    cc = _conv_silu(sub, extc_ref, c_ref, cwc_ref, cbc_ref, shift_ref)

    cols = cols_ref[tok, :]
    cols_hi = cols.astype(BF16)
    cols_mid = (cols - cols_hi.astype(F32)).astype(BF16)
    cols_2 = jnp.concatenate([cols_hi, cols_mid], axis=1)
    dt_e = _dot(cols_2, expand_ref[0])
    dtdecay_e = _dot(cols_2, expand_ref[1])
    exp_cs_e = _dot(cols_2, expand_ref[2])
    chunk_decay_e = exp_cs_e[CHUNK - 1:CHUNK, :]
    a_cs_rows = rows_ref[0, :, tok]
    tril = _tril_mask(CHUNK)
    tile_head = lax.broadcasted_iota(jnp.int32, (CHUNK, LANES), 1) // SSM_HEAD_DIM

    xdt_bf = (xc * dt_e).astype(BF16)
    xdecay_bf = (xc * dtdecay_e).astype(BF16)
    cc_bf = cc.astype(BF16)

    for g in range(SSM_GROUPS):
        ncols = slice(g * SSM_STATE, (g + 1) * SSM_STATE)
        wcols = slice(g * SSM_GROUP_WIDTH, (g + 1) * SSM_GROUP_WIDTH)
        bg_t = bc[:, ncols].T.astype(BF16)
        cg = cc_bf[:, ncols]
        cb = _dot(cg, bg_t)
        prev = state_ref[g]
        y_off = _dot(cg, prev.astype(BF16)) * exp_cs_e[:, wcols]
        state_ref[g] = prev * chunk_decay_e[:, wcols] + _dot(bg_t, xdecay_bf[:, wcols])
        y_pairs = []
        for pair in range(SSM_HEADS_PER_GROUP // HEADS_PER_TILE):
            wts = []
            for r in range(HEADS_PER_TILE):
                h = g * SSM_HEADS_PER_GROUP + pair * HEADS_PER_TILE + r
                seg = cols[:, SSD_COL_CS + h:SSD_COL_CS + h + 1] - a_cs_rows[h:h + 1, :]
                lmat = jnp.exp(jnp.where(tril, seg, -jnp.inf))
                wts.append((cb * lmat).astype(BF16))
            tile0 = g * SSM_GROUP_WIDTH + pair * LANES
            x_tile = xdt_bf[:, tile0:tile0 + LANES]
            x_diag = jnp.concatenate(
                [jnp.where(tile_head == r, x_tile, jnp.zeros_like(x_tile)) for r in range(HEADS_PER_TILE)],
                axis=0)
            y_pairs.append(_dot(jnp.concatenate(wts, axis=1), x_diag))
        y = jnp.concatenate(y_pairs, axis=1) + y_off
        y = y + xc[:, wcols] * dskip_ref[:, wcols]
        y = y * _silu(z_ref[tok, wcols].astype(F32))
        y = y * lax.rsqrt(jnp.mean(y * y, axis=-1, keepdims=True) + EPS)
        o_ref[tok, wcols] = (y * normw_ref[:, wcols]).astype(o_ref.dtype)


def _ssd_operands(proj_a, ssd_cols, ssd_rows, conv_w, conv_b, d_skip, norm_w, expand01):
    row = lambda b, c: b * STEPS_PER_SEQ + c
    const2 = lambda b, c: (0, 0)
    n_shift = (SSM_CONV - 1) * CHUNK
    out_row = lax.broadcasted_iota(jnp.int32, (n_shift, CONV_HALO + CHUNK), 0)
    src_row = lax.broadcasted_iota(jnp.int32, (n_shift, CONV_HALO + CHUNK), 1)
    shift01 = (src_row == CONV_HALO + out_row % CHUNK - out_row // CHUNK - 1).astype(BF16)
    cwx, cwb, cwc = conv_w[:, :D_MODEL], conv_w[:, D_MODEL:D_MODEL + SSM_BC], conv_w[:, D_MODEL + SSM_BC:]
    cbx, cbb, cbc = conv_b[:, :D_MODEL], conv_b[:, D_MODEL:D_MODEL + SSM_BC], conv_b[:, D_MODEL + SSM_BC:]
    in_specs = [pl.BlockSpec((STEP_ROWS, D_MODEL), lambda b, c: (row(b, c), SEG_SX)),
                pl.BlockSpec((STEP_ROWS, D_MODEL), lambda b, c: (row(b, c), SEG_SZ)),
                pl.BlockSpec((STEP_ROWS, SSM_BC), lambda b, c: (row(b, c), SEG_SB)),
                pl.BlockSpec((STEP_ROWS, SSM_BC), lambda b, c: (row(b, c), SEG_SC)),
                pl.BlockSpec((STEP_ROWS, LANES), lambda b, c: (row(b, c), 0)),
                pl.BlockSpec((1, SSM_HEADS, STEP_ROWS), lambda b, c: (b, 0, c)),
                pl.BlockSpec((SSM_CONV, D_MODEL), const2),
                pl.BlockSpec((SSM_CONV, SSM_BC), const2),
                pl.BlockSpec((SSM_CONV, SSM_BC), const2),
                pl.BlockSpec((1, D_MODEL), const2),
                pl.BlockSpec((1, SSM_BC), const2),
                pl.BlockSpec((1, SSM_BC), const2),
                pl.BlockSpec((1, D_MODEL), const2),
                pl.BlockSpec((1, D_MODEL), const2),
                pl.BlockSpec((3, 2 * LANES, D_MODEL), lambda b, c: (0, 0, 0)),
                pl.BlockSpec((n_shift, CONV_HALO + CHUNK), const2)]
    operands = [proj_a, proj_a, proj_a, proj_a, ssd_cols, ssd_rows, cwx, cwb, cwc, cbx, cbb, cbc,
                d_skip, norm_w, expand01, shift01]
    scratch = [pltpu.VMEM((CONV_HALO + STEP_ROWS, D_MODEL), BF16),
               pltpu.VMEM((CONV_HALO + STEP_ROWS, SSM_BC), BF16),
               pltpu.VMEM((CONV_HALO + STEP_ROWS, SSM_BC), BF16),
               pltpu.VMEM((SSM_GROUPS, SSM_STATE, SSM_GROUP_WIDTH), F32)]
    assert len(in_specs) == len(operands) == N_SSD_IN
    return in_specs, operands, scratch


def _mlstm_body(sub, q_ref, k_ref, v_ref, og_ref, z_ref, cols_ref, rows_ref, chunk_ref, normw_ref,
                o_ref, cstate_ref):
    tok = slice(sub * CHUNK, (sub + 1) * CHUNK)
    nh = MLSTM_HEADS
    n_tiles = MLSTM_AUG // LANES
    v_tiles = MLSTM_HEAD_DIM // LANES

    cols_t = cols_ref[tok, :]
    rows = rows_ref[0, :, tok]
    chunk = chunk_ref[0, sub]
    tril = _tril_mask(CHUNK)
    ones_tile = jnp.ones((CHUNK, LANES), BF16)
    mean_mat = jnp.full((MLSTM_HEAD_DIM, LANES), 1.0 / MLSTM_HEAD_DIM, BF16)
    k_scale = MLSTM_HEAD_DIM ** -0.5

    for h in range(nh):
        cols = slice(h * MLSTM_HEAD_DIM, (h + 1) * MLSTM_HEAD_DIM)
        qh = q_ref[tok, cols]
        k_t = k_ref[tok, cols].T
        v_aug = jnp.concatenate([v_ref[tok, cols], ones_tile], axis=1)
        c_prev = cstate_ref[h]
        g_row = rows[ML_ROW_G + h:ML_ROW_G + h + 1, :]
        wgt_row = rows[ML_ROW_WGT + h:ML_ROW_WGT + h + 1, :]
        m_prev_row = chunk[ML_CHUNK_MPREV + h:ML_CHUNK_MPREV + h + 1, :]
        s_old_row = chunk[ML_CHUNK_SOLD + h:ML_CHUNK_SOLD + h + 1, :]

        c1_d = jnp.broadcast_to(cols_t[:, ML_COL_C1 + h:ML_COL_C1 + h + 1], (CHUNK, CHUNK))
        clamp_d = jnp.broadcast_to(cols_t[:, ML_COL_CLAMP + h:ML_COL_CLAMP + h + 1], (CHUNK, CHUNK))
        dw = jnp.exp(jnp.where(tril, c1_d + g_row, -jnp.inf))
        inter_w = jnp.exp(c1_d + m_prev_row)

        s = _dot(qh, k_t) * k_scale
        intra = _dot((s * dw).astype(BF16), v_aug)
        inter = _dot(qh, c_prev.astype(BF16))
        nd = [intra[:, j * LANES:(j + 1) * LANES] + inter_w * inter[:, j * LANES:(j + 1) * LANES]
              for j in range(n_tiles)]
        rden = 1.0 / jnp.maximum(jnp.abs(nd[-1]), clamp_d)

        wk_t = (k_t.astype(F32) * wgt_row).astype(BF16)
        cstate_ref[h] = (c_prev * jnp.concatenate([s_old_row] * n_tiles, axis=1)
                         + _dot(wk_t, v_aug))

        og = _sigmoid(og_ref[tok, cols].astype(F32))
        hc = jnp.concatenate([nd[j] * rden for j in range(v_tiles)], axis=1) * og
        mu = _dot(hc.astype(BF16), mean_mat)
        dev = hc - jnp.concatenate([mu] * v_tiles, axis=1)
        var = _dot((dev * dev).astype(BF16), mean_mat)
        y = dev * jnp.concatenate([lax.rsqrt(var + EPS)] * v_tiles, axis=1)
        y = y * normw_ref[:, cols] * _silu(z_ref[tok, cols].astype(F32))
        o_ref[tok, cols] = y.astype(o_ref.dtype)


def _mlstm_operands(proj_m, ml_cols, ml_rows, ml_chunk, norm_w):
    row = lambda b, c: b * STEPS_PER_SEQ + c
    seg = lambda s: pl.BlockSpec((STEP_ROWS, D_MODEL), lambda b, c: (row(b, c), s))
    in_specs = [seg(SEG_MQ), seg(SEG_MK), seg(SEG_MV), seg(SEG_MO), seg(SEG_MZ),
                pl.BlockSpec((STEP_ROWS, LANES), lambda b, c: (row(b, c), 0)),
                pl.BlockSpec((1, 2 * MLSTM_HEADS, STEP_ROWS), lambda b, c: (b, 0, c)),
                pl.BlockSpec((1, CHUNKS_PER_STEP, 2 * MLSTM_HEADS, LANES), lambda b, c: (b, c, 0, 0)),
                pl.BlockSpec((1, D_MODEL), lambda b, c: (0, 0))]
    operands = [proj_m, proj_m, proj_m, proj_m, proj_m, ml_cols, ml_rows, ml_chunk, norm_w]
    scratch = [pltpu.VMEM((MLSTM_HEADS, MLSTM_HEAD_DIM, MLSTM_AUG), F32)]
    assert len(in_specs) == len(operands) == N_MLSTM_IN
    return in_specs, operands, scratch


def _recurrent_mixers_kernel(*refs):
    ssd_in = refs[:N_SSD_IN]
    ml_in = refs[N_SSD_IN:N_SSD_IN + N_MLSTM_IN]
    o_ssd_ref, o_ml_ref = refs[N_SSD_IN + N_MLSTM_IN:N_SSD_IN + N_MLSTM_IN + 2]
    extx_ref, extb_ref, extc_ref, state_ref, cstate_ref = refs[N_SSD_IN + N_MLSTM_IN + 2:]
    c = pl.program_id(1)
    _ssd_init(c, extx_ref, extb_ref, extc_ref, state_ref)

    @pl.when(c == 0)
    def _():
        cstate_ref[...] = jnp.zeros(cstate_ref.shape, F32)

    xs_ref, _, b_ref, c_ref = ssd_in[:4]
    for ext_ref, new_ref in ((extx_ref, xs_ref), (extb_ref, b_ref), (extc_ref, c_ref)):
        ext_ref[CONV_HALO:CONV_HALO + STEP_ROWS, :] = new_ref[...]
    for sub in range(CHUNKS_PER_STEP):
        _ssd_body(sub, *ssd_in, o_ssd_ref, extx_ref, extb_ref, extc_ref, state_ref)
        _mlstm_body(sub, *ml_in, o_ml_ref, cstate_ref)


def _recurrent_mixers(ssd_args, mlstm_args):
    s_specs, s_ops, s_scratch = _ssd_operands(*ssd_args)
    m_specs, m_ops, m_scratch = _mlstm_operands(*mlstm_args)
    out_spec = pl.BlockSpec((STEP_ROWS, D_MODEL), lambda b, c: (b * STEPS_PER_SEQ + c, 0))
    out_shape = jax.ShapeDtypeStruct((TOKENS, D_MODEL), BF16)
    return pl.pallas_call(
        _recurrent_mixers_kernel,
        grid=(BATCH, STEPS_PER_SEQ),
        in_specs=s_specs + m_specs,
        out_specs=[out_spec, out_spec],
        out_shape=[out_shape, out_shape],
        scratch_shapes=s_scratch + m_scratch,
        compiler_params=_params(2),
        name="recurrent_mixers",
    )(*s_ops, *m_ops)


MERGE_TM = 512
MERGE_TN = 1024


def _merge_kernel(y0_ref, y1_ref, y2_ref, g0_ref, g1_ref, g2_ref, bg_ref, w_ref, o_ref):
    acc = None
    for b, (y_ref, g_ref) in enumerate(((y0_ref, g0_ref), (y1_ref, g1_ref), (y2_ref, g2_ref))):
        gate = _sigmoid(g_ref[...].astype(F32) + bg_ref[b:b + 1, :])
        term = gate * _dot(y_ref[...], w_ref[b])
        acc = term if acc is None else acc + term
    o_ref[...] = acc.astype(o_ref.dtype)


def _merge(layer, y_pool, y_ssm, y_mlstm, proj_g, b_gate, w_branch_all):
    tm, tn = MERGE_TM, MERGE_TN
    per_seg = D_MODEL // tn
    y_spec = pl.BlockSpec((tm, D_MODEL), lambda i, j: (i, 0))
    gate = lambda b: pl.BlockSpec((tm, tn), lambda i, j: (i, b * per_seg + j))
    return pl.pallas_call(
        _merge_kernel,
        grid=(TOKENS // tm, D_MODEL // tn),
        in_specs=[y_spec, y_spec, y_spec, gate(0), gate(1), gate(2),
                  pl.BlockSpec((N_BRANCH, tn), lambda i, j: (0, j)),
                  pl.BlockSpec((None, N_BRANCH, D_MODEL, tn), lambda i, j: (layer, 0, 0, j))],
        out_specs=pl.BlockSpec((tm, tn), lambda i, j: (i, j)),
        out_shape=jax.ShapeDtypeStruct((TOKENS, D_MODEL), BF16),
        compiler_params=_params(2),
        name="branch_merge",
    )(y_pool, y_ssm, y_mlstm, proj_g, proj_g, proj_g, b_gate, w_branch_all)


OUT_TM = 512


def _out_kernel(m_ref, x_ref, w_ref, g_ref, b_ref, o_ref, obf_ref):
    half = OUT_TM // 2
    for r in range(2):
        rows = slice(r * half, (r + 1) * half)
        h = ALPHA * x_ref[rows, :] + _dot(m_ref[rows, :], w_ref[...])
        mu = jnp.mean(h, axis=-1, keepdims=True)
        var = jnp.mean(jnp.square(h - mu), axis=-1, keepdims=True)
        y = (h - mu) * lax.rsqrt(var + EPS) * g_ref[...] + b_ref[...]
        o_ref[rows, :] = y
        obf_ref[rows, :] = y.astype(BF16)


def _out_proj(layer, merged, x, w_out_all, ln_g, ln_b):
    tm = OUT_TM
    row = pl.BlockSpec((tm, D_MODEL), lambda i: (i, 0))
    vec = pl.BlockSpec((1, D_MODEL), lambda i: (0, 0))
    return pl.pallas_call(
        _out_kernel,
        grid=(TOKENS // tm,),
        in_specs=[row, row, pl.BlockSpec((None, D_MODEL, D_MODEL), lambda i: (layer, 0, 0)), vec, vec],
        out_specs=[row, row],
        out_shape=[jax.ShapeDtypeStruct((TOKENS, D_MODEL), F32),
                   jax.ShapeDtypeStruct((TOKENS, D_MODEL), BF16)],
        compiler_params=_params(1),
        name="out_proj_ln",
    )(merged, x, w_out_all, ln_g, ln_b)


def _lane_bcast(vec):
    return jnp.broadcast_to(vec.astype(F32)[:, None], (vec.shape[0], LANES))


def _layer(layer, x, x_mm, w_in_all, b_gate, w_pool, pool_scale, conv_w, conv_b, dt_bias, a_log,
           d_skip, ssm_norm_w, i_bias, f_bias, mlstm_norm_w, w_branch, w_out, ln_g, ln_b, expand01):
    w_s = _pack_gates(w_in_all, layer)
    small, x_bf = _gates_matmul(x_mm, w_s)
    proj_a = _in_proj(x_bf, w_in_all, layer, 0, OFF_SSM_DT, "in_proj_a")
    proj_m = _in_proj(x_bf, w_in_all, layer, OFF_MLSTM, 5 * D_MODEL, "in_proj_m")
    proj_g = _in_proj(x_bf, w_in_all, layer, OFF_MERGE_GATES, N_BRANCH * D_MODEL, "in_proj_g")

    ssd_cols, ssd_rows, ml_cols, ml_rows, ml_chunk = _gate_scan(
        small, _lane_bcast(dt_bias), _lane_bcast(a_log),
        _lane_bcast(jnp.concatenate([i_bias, f_bias])))

    y_pool = _pool_branch(proj_a, w_pool.astype(BF16), pool_scale.reshape(1, D_MODEL))
    y_ssm, y_mlstm = _recurrent_mixers(
        (proj_a, ssd_cols, ssd_rows, conv_w, conv_b.reshape(1, -1),
         jnp.repeat(d_skip, SSM_HEAD_DIM).reshape(1, D_MODEL),
         ssm_norm_w.reshape(1, D_MODEL), expand01),
        (proj_m, ml_cols, ml_rows, ml_chunk, mlstm_norm_w.reshape(1, D_MODEL)))
    merged = _merge(layer, y_pool, y_ssm, y_mlstm, proj_g, b_gate, w_branch)
    return _out_proj(layer, merged, x, w_out, ln_g.reshape(1, D_MODEL), ln_b.reshape(1, D_MODEL))


def kernel(x, w_in, b_gate, w_pool, pool_scale, conv_w, conv_b, dt_bias, a_log, d_skip,
           ssm_norm_w, i_bias, f_bias, mlstm_norm_w, w_branch, w_out, ln_g, ln_b):
    lane = lax.broadcasted_iota(jnp.int32, (3, 2 * LANES, D_MODEL), 1) % LANES
    chan = lax.broadcasted_iota(jnp.int32, (3, 2 * LANES, D_MODEL), 2)
    which = lax.broadcasted_iota(jnp.int32, (3, 2 * LANES, D_MODEL), 0)
    expand01 = (lane == which * SSM_HEADS + chan // SSM_HEAD_DIM).astype(BF16)

    w_in_t = jnp.swapaxes(w_in, 1, 2)
    w_branch_bf = w_branch.astype(BF16)
    w_out_bf = w_out.astype(BF16)
    h = x.reshape(TOKENS, D_MODEL)
    h_mm = h
    for l in range(DEPTH):
        h, h_mm = _layer(l, h, h_mm, w_in_t, b_gate[l], w_pool[l], pool_scale[l], conv_w[l],
                         conv_b[l], dt_bias[l], a_log[l], d_skip[l], ssm_norm_w[l], i_bias[l],
                         f_bias[l], mlstm_norm_w[l], w_branch_bf, w_out_bf, ln_g[l], ln_b[l],
                         expand01)
    return h.reshape(BATCH, SEQ, D_MODEL)
```

```python
import jax
import jax.numpy as jnp
from jax import lax
from jax.experimental import pallas as pl
from jax.experimental.pallas import tpu as pltpu

F32 = jnp.float32
BF16 = jnp.bfloat16

D_MODEL = 2048
BATCH = 8
SEQ = 2048
DEPTH = 2
TOKENS = BATCH * SEQ
CHUNK = 128
N_CHUNKS = SEQ // CHUNK
CHUNKS_PER_STEP = 2
STEP_ROWS = CHUNKS_PER_STEP * CHUNK
STEPS_PER_SEQ = N_CHUNKS // CHUNKS_PER_STEP
LANES = 128

POOL_GROUPS = 4
POOL_WINDOWS = (2, 4, 8, 16)
POOL_GDIM = D_MODEL // POOL_GROUPS
POOL_HALO = 16

SSM_HEAD_DIM = 64
SSM_HEADS = D_MODEL // SSM_HEAD_DIM
SSM_GROUPS = 4
SSM_HEADS_PER_GROUP = SSM_HEADS // SSM_GROUPS
SSM_STATE = 128
SSM_CONV = 4
SSM_BC = SSM_GROUPS * SSM_STATE
SSM_GROUP_WIDTH = D_MODEL // SSM_GROUPS
HEADS_PER_TILE = LANES // SSM_HEAD_DIM
CONV_HALO = 16

MLSTM_HEADS = 8
MLSTM_HEAD_DIM = D_MODEL // MLSTM_HEADS
MLSTM_AUG = MLSTM_HEAD_DIM + LANES

N_BRANCH = 3
ALPHA = (2 * DEPTH) ** 0.25
EPS = 1e-5

OFF_SSM_DT = 4 * D_MODEL + 2 * SSM_BC
OFF_MLSTM = OFF_SSM_DT + SSM_HEADS
OFF_MLSTM_GATES = OFF_MLSTM + 5 * D_MODEL
OFF_MERGE_GATES = OFF_MLSTM_GATES + 2 * MLSTM_HEADS
IN_DIM = OFF_MERGE_GATES + N_BRANCH * D_MODEL

SEG_PU, SEG_PZ, SEG_SX, SEG_SZ = range(4)
SEG_SB = 4 * D_MODEL // SSM_BC
SEG_SC = SEG_SB + 1
SEG_MQ, SEG_MK, SEG_MV, SEG_MO, SEG_MZ = range(5)
SMALL_WIDTH = LANES
LANE_DT = 0
LANE_IG = SSM_HEADS
LANE_FG = SSM_HEADS + MLSTM_HEADS

VMEM_LIMIT = 56 * 1024 * 1024


def _params(n_axes):
    return pltpu.CompilerParams(dimension_semantics=("arbitrary",) * n_axes,
                                vmem_limit_bytes=VMEM_LIMIT)


def _sigmoid(v):
    return 0.5 * jnp.tanh(0.5 * v) + 0.5


def _silu(v):
    h = 0.5 * v
    return h * jnp.tanh(h) + h


def _split3(v):
    hi = v.astype(BF16)
    r1 = v - hi.astype(F32)
    mid = r1.astype(BF16)
    lo = (r1 - mid.astype(F32)).astype(BF16)
    return hi, mid, lo


def _dot(a, b):
    return jnp.dot(a, b, preferred_element_type=F32)


def _exact_dot_right01(v, mat01):
    hi, mid, lo = _split3(v)
    return _dot(hi, mat01) + _dot(mid, mat01) + _dot(lo, mat01)


def _tril_mask(n):
    r = lax.broadcasted_iota(jnp.int32, (n, n), 0)
    c = lax.broadcasted_iota(jnp.int32, (n, n), 1)
    return c <= r


def _matmul_kernel(x_ref, w_ref, o_ref):
    o_ref[...] = _dot(x_ref[...], w_ref[...]).astype(o_ref.dtype)


def _matmul(x, w, out_dtype, tm, tn, name):
    m, k = x.shape
    n = w.shape[1]
    return pl.pallas_call(
        _matmul_kernel,
        grid=(m // tm, n // tn),
        in_specs=[pl.BlockSpec((tm, k), lambda i, j: (i, 0)),
                  pl.BlockSpec((k, tn), lambda i, j: (0, j))],
        out_specs=pl.BlockSpec((tm, tn), lambda i, j: (i, j)),
        out_shape=jax.ShapeDtypeStruct((m, n), out_dtype),
        compiler_params=_params(2),
        name=name,
    )(x, w)


PROJ_TM = 2048
PROJ_TN = 1024
GATES_TM = 1024


def _in_proj_kernel(x_ref, wt_ref, o_ref, w_ref):
    @pl.when(pl.program_id(1) == 0)
    def _():
        w_ref[...] = wt_ref[0].T.astype(BF16)

    o_ref[...] = _dot(x_ref[...], w_ref[...]).astype(o_ref.dtype)


def _in_proj(x_bf, w_in_t, layer, col_start, width, name):
    m, k = x_bf.shape
    return pl.pallas_call(
        _in_proj_kernel,
        grid=(width // PROJ_TN, m // PROJ_TM),
        in_specs=[pl.BlockSpec((PROJ_TM, k), lambda j, i: (i, 0)),
                  pl.BlockSpec((pl.Element(1), pl.Element(PROJ_TN), pl.Element(k)),
                               lambda j, i: (layer, pl.multiple_of(col_start + j * PROJ_TN, 16), 0))],
        out_specs=pl.BlockSpec((PROJ_TM, PROJ_TN), lambda j, i: (i, j)),
        out_shape=jax.ShapeDtypeStruct((m, width), BF16),
        scratch_shapes=[pltpu.VMEM((k, PROJ_TN), BF16)],
        compiler_params=_params(2),
        name=name,
    )(x_bf, w_in_t)


def _gates_cast_kernel(x_ref, w_ref, o_ref, xbf_ref):
    xb = x_ref[...].astype(BF16)
    xbf_ref[...] = xb
    o_ref[...] = _dot(xb, w_ref[...])


def _gates_matmul(x, w_s):
    if x.dtype == BF16:
        return _matmul(x, w_s, F32, GATES_TM, SMALL_WIDTH, "in_proj_gates"), x
    m, k = x.shape
    row = pl.BlockSpec((GATES_TM, k), lambda i: (i, 0))
    return pl.pallas_call(
        _gates_cast_kernel,
        grid=(m // GATES_TM,),
        in_specs=[row, pl.BlockSpec((k, SMALL_WIDTH), lambda i: (0, 0))],
        out_specs=[pl.BlockSpec((GATES_TM, SMALL_WIDTH), lambda i: (i, 0)), row],
        out_shape=[jax.ShapeDtypeStruct((m, SMALL_WIDTH), F32), jax.ShapeDtypeStruct((m, k), BF16)],
        compiler_params=_params(1),
        name="in_proj_gates_cast",
    )(x, w_s)


def _pack_gates_kernel(dt_ref, gate_ref, o_ref):
    pad = jnp.zeros((SMALL_WIDTH - SSM_HEADS - 2 * MLSTM_HEADS, D_MODEL), F32)
    rows = jnp.concatenate([dt_ref[0], gate_ref[0], pad], axis=0)
    o_ref[...] = rows.T.astype(BF16)


def _pack_gates(w_in_t, layer):
    rows = lambda n, start: pl.BlockSpec((pl.Element(1), pl.Element(n), pl.Element(D_MODEL)),
                                         lambda i: (layer, start, 0))
    return pl.pallas_call(
        _pack_gates_kernel,
        grid=(1,),
        in_specs=[rows(SSM_HEADS, OFF_SSM_DT), rows(2 * MLSTM_HEADS, OFF_MLSTM_GATES)],
        out_specs=pl.BlockSpec((D_MODEL, LANES), lambda i: (0, 0)),
        out_shape=jax.ShapeDtypeStruct((D_MODEL, SMALL_WIDTH), BF16),
        compiler_params=_params(1),
        name="pack_w_gates",
    )(w_in_t, w_in_t)


POOL_TM = 512


POOL_BAND = CHUNK


def _pool_kernel(u_ref, z_ref, w_ref, scale_ref, band_ref, o_ref, ext_ref):
    t = pl.program_id(1)
    tm = POOL_TM

    @pl.when(t == 0)
    def _():
        ext_ref[0:POOL_HALO, :] = jnp.zeros((POOL_HALO, D_MODEL), BF16)

    @pl.when(t != 0)
    def _():
        ext_ref[0:POOL_HALO, :] = ext_ref[tm:tm + POOL_HALO, :]

    ext_ref[POOL_HALO:POOL_HALO + tm, :] = u_ref[...]

    row = lax.broadcasted_iota(jnp.int32, (POOL_BAND, LANES), 0)
    for g, win in enumerate(POOL_WINDOWS):
        cols = slice(g * POOL_GDIM, (g + 1) * POOL_GDIM)
        pooled = []
        for r in range(tm // POOL_BAND):
            r0 = r * POOL_BAND
            win_sum = _dot(band_ref[g], ext_ref[r0:r0 + POOL_HALO + POOL_BAND, cols])
            cur = u_ref[r0:r0 + POOL_BAND, cols].astype(F32)
            inv_cnt = 1.0 / jnp.minimum(t * tm + r0 + row + 1, win).astype(F32)
            inv_cnt = jnp.concatenate([inv_cnt] * (POOL_GDIM // LANES), axis=1)
            pooled.append((win_sum * inv_cnt - cur).astype(BF16))
        mixed = _dot(jnp.concatenate(pooled, axis=0), w_ref[g])
        y = mixed * scale_ref[:, cols] * _silu(z_ref[:, cols].astype(F32))
        o_ref[:, cols] = y.astype(o_ref.dtype)


def _pool_branch(proj_a, w_pool, pool_scale):
    nt = SEQ // POOL_TM
    shape = (POOL_GROUPS, POOL_BAND, POOL_HALO + POOL_BAND)
    back = (POOL_HALO + lax.broadcasted_iota(jnp.int32, shape, 1)
            - lax.broadcasted_iota(jnp.int32, shape, 2))
    width = jnp.asarray(POOL_WINDOWS, jnp.int32)[:, None, None]
    band01 = ((back >= 0) & (back < width)).astype(BF16)
    return pl.pallas_call(
        _pool_kernel,
        grid=(BATCH, nt),
        in_specs=[pl.BlockSpec((POOL_TM, D_MODEL), lambda b, t: (b * nt + t, SEG_PU)),
                  pl.BlockSpec((POOL_TM, D_MODEL), lambda b, t: (b * nt + t, SEG_PZ)),
                  pl.BlockSpec((POOL_GROUPS, POOL_GDIM, POOL_GDIM), lambda b, t: (0, 0, 0)),
                  pl.BlockSpec((1, D_MODEL), lambda b, t: (0, 0)),
                  pl.BlockSpec(shape, lambda b, t: (0, 0, 0))],
        out_specs=pl.BlockSpec((POOL_TM, D_MODEL), lambda b, t: (b * nt + t, 0)),
        out_shape=jax.ShapeDtypeStruct((TOKENS, D_MODEL), BF16),
        scratch_shapes=[pltpu.VMEM((POOL_HALO + POOL_TM, D_MODEL), BF16)],
        compiler_params=_params(2),
        name="pool_mixer",
    )(proj_a, proj_a, w_pool, pool_scale, band01)


SSD_COL_DT, SSD_COL_DTDECAY, SSD_COL_EXPCS, SSD_COL_CS = (k * SSM_HEADS for k in range(4))
ML_COL_C1, ML_COL_CLAMP = 0, MLSTM_HEADS
ML_ROW_G, ML_ROW_WGT = 0, MLSTM_HEADS
ML_CHUNK_MPREV, ML_CHUNK_SOLD = 0, MLSTM_HEADS


def _gate_scan_kernel(small_ref, dtb_ref, alog_ref, gbias_ref,
                      ssd_cols_ref, ssd_rows_ref, ml_cols_ref, ml_rows_ref, ml_chunk_ref):
    nh = MLSTM_HEADS
    r = lax.broadcasted_iota(jnp.int32, (CHUNK, CHUNK), 0)
    cidx = lax.broadcasted_iota(jnp.int32, (CHUNK, CHUNK), 1)
    triu01 = (r <= cidx).astype(BF16)
    lane = lax.broadcasted_iota(jnp.int32, (nh, CHUNK), 1)
    a_coef = -jnp.exp(alog_ref[...])
    m_prev = jnp.zeros((nh, LANES), F32)
    ml_pad = jnp.zeros((CHUNK - 2 * nh, CHUNK), F32)

    for c in range(N_CHUNKS):
        tok = slice(c * CHUNK, (c + 1) * CHUNK)
        small_t = small_ref[tok, :].T

        dt = jax.nn.softplus(small_t[LANE_DT:LANE_DT + SSM_HEADS, :] + dtb_ref[...])
        a_cs = _exact_dot_right01(dt * a_coef, triu01)
        a_last = a_cs[:, CHUNK - 1:CHUNK]
        decay = jnp.exp(a_last - a_cs)
        ssd_rows_ref[0, :, tok] = a_cs
        ssd_cols_ref[tok, :] = jnp.concatenate([dt, dt * decay, jnp.exp(a_cs), a_cs], axis=0).T

        pre = small_t[LANE_IG:LANE_IG + 2 * nh, :] + gbias_ref[...]
        cum = _exact_dot_right01(jax.nn.log_sigmoid(pre), triu01)
        ig = pre[0:nh, :]
        bcum = cum[nh:2 * nh, :]
        g = ig - bcum
        b_last = bcum[:, CHUNK - 1:CHUNK]
        pmax = g
        shift = 1
        while shift < CHUNK:
            pmax = jnp.maximum(pmax, jnp.where(lane >= shift, pltpu.roll(pmax, shift, axis=1), -jnp.inf))
            shift *= 2
        m_t = jnp.maximum(bcum + pmax, bcum + m_prev)
        w_log = b_last + g
        m_loc = jnp.max(w_log, axis=1, keepdims=True)
        m_new = jnp.maximum(b_last + m_prev, m_loc)
        s_old = jnp.exp(b_last + m_prev - m_new)
        s_loc = jnp.exp(m_loc - m_new)
        wgt = jnp.exp(w_log - m_loc) * (s_loc * MLSTM_HEAD_DIM ** -0.5)
        ml_rows_ref[0, :, tok] = jnp.concatenate([g, wgt], axis=0)
        ml_chunk_ref[0, c] = jnp.concatenate([m_prev, s_old], axis=0)
        ml_cols_ref[tok, :] = jnp.concatenate([bcum - m_t, jnp.exp(-m_t), ml_pad], axis=0).T
        m_prev = m_new


def _gate_scan(small, dt_bias, a_log, gate_bias):
    const2 = lambda b: (0, 0)
    return pl.pallas_call(
        _gate_scan_kernel,
        grid=(BATCH,),
        in_specs=[pl.BlockSpec((SEQ, SMALL_WIDTH), lambda b: (b, 0)),
                  pl.BlockSpec((SSM_HEADS, LANES), const2),
                  pl.BlockSpec((SSM_HEADS, LANES), const2),
                  pl.BlockSpec((2 * MLSTM_HEADS, LANES), const2)],
        out_specs=[pl.BlockSpec((SEQ, LANES), lambda b: (b, 0)),
                   pl.BlockSpec((1, SSM_HEADS, SEQ), lambda b: (b, 0, 0)),
                   pl.BlockSpec((SEQ, LANES), lambda b: (b, 0)),
                   pl.BlockSpec((1, 2 * MLSTM_HEADS, SEQ), lambda b: (b, 0, 0)),
                   pl.BlockSpec((1, N_CHUNKS, 2 * MLSTM_HEADS, LANES), lambda b: (b, 0, 0, 0))],
        out_shape=[jax.ShapeDtypeStruct((TOKENS, LANES), F32),
                   jax.ShapeDtypeStruct((BATCH, SSM_HEADS, SEQ), F32),
                   jax.ShapeDtypeStruct((TOKENS, LANES), F32),
                   jax.ShapeDtypeStruct((BATCH, 2 * MLSTM_HEADS, SEQ), F32),
                   jax.ShapeDtypeStruct((BATCH, N_CHUNKS, 2 * MLSTM_HEADS, LANES), F32)],
        compiler_params=_params(1),
        name="gate_scan",
    )(small, dt_bias, a_log, gate_bias)


def _ssd_init(c, extx_ref, extb_ref, extc_ref, state_ref):
    @pl.when(c == 0)
    def _():
        state_ref[...] = jnp.zeros(state_ref.shape, F32)
        for ext_ref in (extx_ref, extb_ref, extc_ref):
            ext_ref[0:CONV_HALO, :] = jnp.zeros((CONV_HALO, ext_ref.shape[1]), BF16)

    @pl.when(c != 0)
    def _():
        for ext_ref in (extx_ref, extb_ref, extc_ref):
            ext_ref[0:CONV_HALO, :] = ext_ref[STEP_ROWS:STEP_ROWS + CONV_HALO, :]


def _conv_silu(sub, ext_ref, new_ref, w_ref, b_ref, shift_ref, cols=slice(None)):
    window = ext_ref[sub * CHUNK:sub * CHUNK + CONV_HALO + CHUNK, cols]
    taps = _dot(shift_ref[...], window)
    cur = new_ref[sub * CHUNK:(sub + 1) * CHUNK, cols].astype(F32)
    w_half = 0.5 * w_ref[:, cols]
    half = 0.5 * b_ref[:, cols] + w_half[SSM_CONV - 1:SSM_CONV, :] * cur
    for j in range(1, SSM_CONV):
        half = half + w_half[SSM_CONV - 1 - j:SSM_CONV - j, :] * taps[(j - 1) * CHUNK:j * CHUNK, :]
    return half * jnp.tanh(half) + half


N_SSD_IN = 16
N_MLSTM_IN = 9
MLSTM_HEADS_PER_SLICE = 2
_DONE = object()


def _ssd_body(sub, xs_ref, z_ref, b_ref, c_ref, cols_ref, rows_ref,
              cwx_ref, cwb_ref, cwc_ref, cbx_ref, cbb_ref, cbc_ref,
              dskip_ref, normw_ref, expand_ref, shift_ref,
              o_ref, extx_ref, extb_ref, extc_ref, state_ref):
    tok = slice(sub * CHUNK, (sub + 1) * CHUNK)
    bc = _conv_silu(sub, extb_ref, b_ref, cwb_ref, cbb_ref, shift_ref)
    cc = _conv_silu(sub, extc_ref, c_ref, cwc_ref, cbc_ref, shift_ref)

    cols = cols_ref[tok, :]
    cols_hi = cols.astype(BF16)
    cols_mid = (cols - cols_hi.astype(F32)).astype(BF16)
    cols_2 = jnp.concatenate([cols_hi, cols_mid], axis=1)
    a_cs_rows = rows_ref[0, :, tok]
    tril = _tril_mask(CHUNK)
    tile_head = lax.broadcasted_iota(jnp.int32, (CHUNK, LANES), 1) // SSM_HEAD_DIM
    cc_bf = cc.astype(BF16)
    yield

    for g in range(SSM_GROUPS):
        ncols = slice(g * SSM_STATE, (g + 1) * SSM_STATE)
        wcols = slice(g * SSM_GROUP_WIDTH, (g + 1) * SSM_GROUP_WIDTH)
        xc = _conv_silu(sub, extx_ref, xs_ref, cwx_ref, cbx_ref, shift_ref, wcols)
        dt_e = _dot(cols_2, expand_ref[0, :, wcols])
        dtdecay_e = _dot(cols_2, expand_ref[1, :, wcols])
        exp_cs_e = _dot(cols_2, expand_ref[2, :, wcols])
        xdt_bf = (xc * dt_e).astype(BF16)
        xdecay_bf = (xc * dtdecay_e).astype(BF16)
        bg_t = bc[:, ncols].T.astype(BF16)
        cg = cc_bf[:, ncols]
        cb = _dot(cg, bg_t)
        prev = state_ref[g]
        y_off = _dot(cg, prev.astype(BF16)) * exp_cs_e
        state_ref[g] = prev * exp_cs_e[CHUNK - 1:CHUNK, :] + _dot(bg_t, xdecay_bf)
        y_pairs = []
        for pair in range(SSM_HEADS_PER_GROUP // HEADS_PER_TILE):
            wts = []
            for r in range(HEADS_PER_TILE):
                h = g * SSM_HEADS_PER_GROUP + pair * HEADS_PER_TILE + r
                seg = cols[:, SSD_COL_CS + h:SSD_COL_CS + h + 1] - a_cs_rows[h:h + 1, :]
                lmat = jnp.exp(jnp.where(tril, seg, -jnp.inf))
                wts.append((cb * lmat).astype(BF16))
            x_tile = xdt_bf[:, pair * LANES:(pair + 1) * LANES]
            x_diag = jnp.concatenate(
                [jnp.where(tile_head == r, x_tile, jnp.zeros_like(x_tile)) for r in range(HEADS_PER_TILE)],
                axis=0)
            y_pairs.append(_dot(jnp.concatenate(wts, axis=1), x_diag))
        y = jnp.concatenate(y_pairs, axis=1) + y_off
        y = y + xc * dskip_ref[:, wcols]
        y = y * _silu(z_ref[tok, wcols].astype(F32))
        y = y * lax.rsqrt(jnp.mean(y * y, axis=-1, keepdims=True) + EPS)
        o_ref[tok, wcols] = (y * normw_ref[:, wcols]).astype(o_ref.dtype)
        yield


def _ssd_operands(proj_a, ssd_cols, ssd_rows, conv_w, conv_b, d_skip, norm_w, expand01):
    row = lambda b, c: b * STEPS_PER_SEQ + c
    const2 = lambda b, c: (0, 0)
    n_shift = (SSM_CONV - 1) * CHUNK
    out_row = lax.broadcasted_iota(jnp.int32, (n_shift, CONV_HALO + CHUNK), 0)
    src_row = lax.broadcasted_iota(jnp.int32, (n_shift, CONV_HALO + CHUNK), 1)
    shift01 = (src_row == CONV_HALO + out_row % CHUNK - out_row // CHUNK - 1).astype(BF16)
    cwx, cwb, cwc = conv_w[:, :D_MODEL], conv_w[:, D_MODEL:D_MODEL + SSM_BC], conv_w[:, D_MODEL + SSM_BC:]
    cbx, cbb, cbc = conv_b[:, :D_MODEL], conv_b[:, D_MODEL:D_MODEL + SSM_BC], conv_b[:, D_MODEL + SSM_BC:]
    in_specs = [pl.BlockSpec((STEP_ROWS, D_MODEL), lambda b, c: (row(b, c), SEG_SX)),
                pl.BlockSpec((STEP_ROWS, D_MODEL), lambda b, c: (row(b, c), SEG_SZ)),
                pl.BlockSpec((STEP_ROWS, SSM_BC), lambda b, c: (row(b, c), SEG_SB)),
                pl.BlockSpec((STEP_ROWS, SSM_BC), lambda b, c: (row(b, c), SEG_SC)),
                pl.BlockSpec((STEP_ROWS, LANES), lambda b, c: (row(b, c), 0)),
                pl.BlockSpec((1, SSM_HEADS, STEP_ROWS), lambda b, c: (b, 0, c)),
                pl.BlockSpec((SSM_CONV, D_MODEL), const2),
                pl.BlockSpec((SSM_CONV, SSM_BC), const2),
                pl.BlockSpec((SSM_CONV, SSM_BC), const2),
                pl.BlockSpec((1, D_MODEL), const2),
                pl.BlockSpec((1, SSM_BC), const2),
                pl.BlockSpec((1, SSM_BC), const2),
                pl.BlockSpec((1, D_MODEL), const2),
                pl.BlockSpec((1, D_MODEL), const2),
                pl.BlockSpec((3, 2 * LANES, D_MODEL), lambda b, c: (0, 0, 0)),
                pl.BlockSpec((n_shift, CONV_HALO + CHUNK), const2)]
    operands = [proj_a, proj_a, proj_a, proj_a, ssd_cols, ssd_rows, cwx, cwb, cwc, cbx, cbb, cbc,
                d_skip, norm_w, expand01, shift01]
    scratch = [pltpu.VMEM((CONV_HALO + STEP_ROWS, D_MODEL), BF16),
               pltpu.VMEM((CONV_HALO + STEP_ROWS, SSM_BC), BF16),
               pltpu.VMEM((CONV_HALO + STEP_ROWS, SSM_BC), BF16),
               pltpu.VMEM((SSM_GROUPS, SSM_STATE, SSM_GROUP_WIDTH), F32)]
    assert len(in_specs) == len(operands) == N_SSD_IN
    return in_specs, operands, scratch


def _mlstm_body(sub, q_ref, k_ref, v_ref, og_ref, z_ref, cols_ref, rows_ref, chunk_ref, normw_ref,
                o_ref, cstate_ref):
    tok = slice(sub * CHUNK, (sub + 1) * CHUNK)
    nh = MLSTM_HEADS
    n_tiles = MLSTM_AUG // LANES
    v_tiles = MLSTM_HEAD_DIM // LANES

    cols_t = cols_ref[tok, :]
    rows = rows_ref[0, :, tok]
    chunk = chunk_ref[0, sub]
    tril = _tril_mask(CHUNK)
    ones_tile = jnp.ones((CHUNK, LANES), BF16)
    mean_mat = jnp.full((MLSTM_HEAD_DIM, LANES), 1.0 / MLSTM_HEAD_DIM, BF16)
    k_scale = MLSTM_HEAD_DIM ** -0.5

    for h in range(nh):
        cols = slice(h * MLSTM_HEAD_DIM, (h + 1) * MLSTM_HEAD_DIM)
        qh = q_ref[tok, cols]
        k_t = k_ref[tok, cols].T
        v_aug = jnp.concatenate([v_ref[tok, cols], ones_tile], axis=1)
        c_prev = cstate_ref[h]
        g_row = rows[ML_ROW_G + h:ML_ROW_G + h + 1, :]
        wgt_row = rows[ML_ROW_WGT + h:ML_ROW_WGT + h + 1, :]
        m_prev_row = chunk[ML_CHUNK_MPREV + h:ML_CHUNK_MPREV + h + 1, :]
        s_old_row = chunk[ML_CHUNK_SOLD + h:ML_CHUNK_SOLD + h + 1, :]

        c1_d = jnp.broadcast_to(cols_t[:, ML_COL_C1 + h:ML_COL_C1 + h + 1], (CHUNK, CHUNK))
        clamp_d = jnp.broadcast_to(cols_t[:, ML_COL_CLAMP + h:ML_COL_CLAMP + h + 1], (CHUNK, CHUNK))
        dw = jnp.exp(jnp.where(tril, c1_d + g_row, -jnp.inf))
        inter_w = jnp.exp(c1_d + m_prev_row)

        s = _dot(qh, k_t) * k_scale
        intra = _dot((s * dw).astype(BF16), v_aug)
        inter = _dot(qh, c_prev.astype(BF16))
        nd = [intra[:, j * LANES:(j + 1) * LANES] + inter_w * inter[:, j * LANES:(j + 1) * LANES]
              for j in range(n_tiles)]
        rden = 1.0 / jnp.maximum(jnp.abs(nd[-1]), clamp_d)

        wk_t = (k_t.astype(F32) * wgt_row).astype(BF16)
        cstate_ref[h] = (c_prev * jnp.concatenate([s_old_row] * n_tiles, axis=1)
                         + _dot(wk_t, v_aug))

        og = _sigmoid(og_ref[tok, cols].astype(F32))
        hc = jnp.concatenate([nd[j] * rden for j in range(v_tiles)], axis=1) * og
        mu = _dot(hc.astype(BF16), mean_mat)
        dev = hc - jnp.concatenate([mu] * v_tiles, axis=1)
        var = _dot((dev * dev).astype(BF16), mean_mat)
        y = dev * jnp.concatenate([lax.rsqrt(var + EPS)] * v_tiles, axis=1)
        y = y * normw_ref[:, cols] * _silu(z_ref[tok, cols].astype(F32))
        o_ref[tok, cols] = y.astype(o_ref.dtype)
        if h % MLSTM_HEADS_PER_SLICE == MLSTM_HEADS_PER_SLICE - 1:
            yield


def _mlstm_operands(proj_m, ml_cols, ml_rows, ml_chunk, norm_w):
    row = lambda b, c: b * STEPS_PER_SEQ + c
    seg = lambda s: pl.BlockSpec((STEP_ROWS, D_MODEL), lambda b, c: (row(b, c), s))
    in_specs = [seg(SEG_MQ), seg(SEG_MK), seg(SEG_MV), seg(SEG_MO), seg(SEG_MZ),
                pl.BlockSpec((STEP_ROWS, LANES), lambda b, c: (row(b, c), 0)),
                pl.BlockSpec((1, 2 * MLSTM_HEADS, STEP_ROWS), lambda b, c: (b, 0, c)),
                pl.BlockSpec((1, CHUNKS_PER_STEP, 2 * MLSTM_HEADS, LANES), lambda b, c: (b, c, 0, 0)),
                pl.BlockSpec((1, D_MODEL), lambda b, c: (0, 0))]
    operands = [proj_m, proj_m, proj_m, proj_m, proj_m, ml_cols, ml_rows, ml_chunk, norm_w]
    scratch = [pltpu.VMEM((MLSTM_HEADS, MLSTM_HEAD_DIM, MLSTM_AUG), F32)]
    assert len(in_specs) == len(operands) == N_MLSTM_IN
    return in_specs, operands, scratch


def _recurrent_mixers_kernel(*refs):
    ssd_in = refs[:N_SSD_IN]
    ml_in = refs[N_SSD_IN:N_SSD_IN + N_MLSTM_IN]
    o_ssd_ref, o_ml_ref = refs[N_SSD_IN + N_MLSTM_IN:N_SSD_IN + N_MLSTM_IN + 2]
    extx_ref, extb_ref, extc_ref, state_ref, cstate_ref = refs[N_SSD_IN + N_MLSTM_IN + 2:]
    c = pl.program_id(1)
    _ssd_init(c, extx_ref, extb_ref, extc_ref, state_ref)

    @pl.when(c == 0)
    def _():
        cstate_ref[...] = jnp.zeros(cstate_ref.shape, F32)

    xs_ref, _, b_ref, c_ref = ssd_in[:4]
    for ext_ref, new_ref in ((extx_ref, xs_ref), (extb_ref, b_ref), (extc_ref, c_ref)):
        ext_ref[CONV_HALO:CONV_HALO + STEP_ROWS, :] = new_ref[...]
    for sub in range(CHUNKS_PER_STEP):
        slices = [_ssd_body(sub, *ssd_in, o_ssd_ref, extx_ref, extb_ref, extc_ref, state_ref),
                  _mlstm_body(sub, *ml_in, o_ml_ref, cstate_ref)]
        while slices:
            for gen in list(slices):
                if next(gen, _DONE) is _DONE:
                    slices.remove(gen)


def _recurrent_mixers(ssd_args, mlstm_args):
    s_specs, s_ops, s_scratch = _ssd_operands(*ssd_args)
    m_specs, m_ops, m_scratch = _mlstm_operands(*mlstm_args)
    out_spec = pl.BlockSpec((STEP_ROWS, D_MODEL), lambda b, c: (b * STEPS_PER_SEQ + c, 0))
    out_shape = jax.ShapeDtypeStruct((TOKENS, D_MODEL), BF16)
    return pl.pallas_call(
        _recurrent_mixers_kernel,
        grid=(BATCH, STEPS_PER_SEQ),
        in_specs=s_specs + m_specs,
        out_specs=[out_spec, out_spec],
        out_shape=[out_shape, out_shape],
        scratch_shapes=s_scratch + m_scratch,
        compiler_params=_params(2),
        name="recurrent_mixers",
    )(*s_ops, *m_ops)


MERGE_TM = 512
MERGE_TN = 1024


def _merge_kernel(y0_ref, y1_ref, y2_ref, g0_ref, g1_ref, g2_ref, bg_ref, w_ref, o_ref):
    acc = None
    for b, (y_ref, g_ref) in enumerate(((y0_ref, g0_ref), (y1_ref, g1_ref), (y2_ref, g2_ref))):
        gate = _sigmoid(g_ref[...].astype(F32) + bg_ref[b:b + 1, :])
        term = gate * _dot(y_ref[...], w_ref[b])
        acc = term if acc is None else acc + term
    o_ref[...] = acc.astype(o_ref.dtype)


def _merge(layer, y_pool, y_ssm, y_mlstm, proj_g, b_gate, w_branch_all):
    tm, tn = MERGE_TM, MERGE_TN
    per_seg = D_MODEL // tn
    y_spec = pl.BlockSpec((tm, D_MODEL), lambda i, j: (i, 0))
    gate = lambda b: pl.BlockSpec((tm, tn), lambda i, j: (i, b * per_seg + j))
    return pl.pallas_call(
        _merge_kernel,
        grid=(TOKENS // tm, D_MODEL // tn),
        in_specs=[y_spec, y_spec, y_spec, gate(0), gate(1), gate(2),
                  pl.BlockSpec((N_BRANCH, tn), lambda i, j: (0, j)),
                  pl.BlockSpec((None, N_BRANCH, D_MODEL, tn), lambda i, j: (layer, 0, 0, j))],
        out_specs=pl.BlockSpec((tm, tn), lambda i, j: (i, j)),
        out_shape=jax.ShapeDtypeStruct((TOKENS, D_MODEL), BF16),
        compiler_params=_params(2),
        name="branch_merge",
    )(y_pool, y_ssm, y_mlstm, proj_g, proj_g, proj_g, b_gate, w_branch_all)


OUT_TM = 512


def _out_kernel(m_ref, x_ref, w_ref, g_ref, b_ref, o_ref, obf_ref):
    half = OUT_TM // 2
    for r in range(2):
        rows = slice(r * half, (r + 1) * half)
        h = ALPHA * x_ref[rows, :] + _dot(m_ref[rows, :], w_ref[...])
        mu = jnp.mean(h, axis=-1, keepdims=True)
        var = jnp.mean(jnp.square(h - mu), axis=-1, keepdims=True)
        y = (h - mu) * lax.rsqrt(var + EPS) * g_ref[...] + b_ref[...]
        o_ref[rows, :] = y
        obf_ref[rows, :] = y.astype(BF16)


def _out_proj(layer, merged, x, w_out_all, ln_g, ln_b):
    tm = OUT_TM
    row = pl.BlockSpec((tm, D_MODEL), lambda i: (i, 0))
    vec = pl.BlockSpec((1, D_MODEL), lambda i: (0, 0))
    return pl.pallas_call(
        _out_kernel,
        grid=(TOKENS // tm,),
        in_specs=[row, row, pl.BlockSpec((None, D_MODEL, D_MODEL), lambda i: (layer, 0, 0)), vec, vec],
        out_specs=[row, row],
        out_shape=[jax.ShapeDtypeStruct((TOKENS, D_MODEL), F32),
                   jax.ShapeDtypeStruct((TOKENS, D_MODEL), BF16)],
        compiler_params=_params(1),
        name="out_proj_ln",
    )(merged, x, w_out_all, ln_g, ln_b)


def _lane_bcast(vec):
    return jnp.broadcast_to(vec.astype(F32)[:, None], (vec.shape[0], LANES))


def _layer(layer, x, x_mm, w_in_all, b_gate, w_pool, pool_scale, conv_w, conv_b, dt_bias, a_log,
           d_skip, ssm_norm_w, i_bias, f_bias, mlstm_norm_w, w_branch, w_out, ln_g, ln_b, expand01):
    w_s = _pack_gates(w_in_all, layer)
    small, x_bf = _gates_matmul(x_mm, w_s)
    proj_a = _in_proj(x_bf, w_in_all, layer, 0, OFF_SSM_DT, "in_proj_a")
    proj_m = _in_proj(x_bf, w_in_all, layer, OFF_MLSTM, 5 * D_MODEL, "in_proj_m")
    proj_g = _in_proj(x_bf, w_in_all, layer, OFF_MERGE_GATES, N_BRANCH * D_MODEL, "in_proj_g")

    ssd_cols, ssd_rows, ml_cols, ml_rows, ml_chunk = _gate_scan(
        small, _lane_bcast(dt_bias), _lane_bcast(a_log),
        _lane_bcast(jnp.concatenate([i_bias, f_bias])))

    y_pool = _pool_branch(proj_a, w_pool.astype(BF16), pool_scale.reshape(1, D_MODEL))
    y_ssm, y_mlstm = _recurrent_mixers(
        (proj_a, ssd_cols, ssd_rows, conv_w, conv_b.reshape(1, -1),
         jnp.repeat(d_skip, SSM_HEAD_DIM).reshape(1, D_MODEL),
         ssm_norm_w.reshape(1, D_MODEL), expand01),
        (proj_m, ml_cols, ml_rows, ml_chunk, mlstm_norm_w.reshape(1, D_MODEL)))
    merged = _merge(layer, y_pool, y_ssm, y_mlstm, proj_g, b_gate, w_branch)
    return _out_proj(layer, merged, x, w_out, ln_g.reshape(1, D_MODEL), ln_b.reshape(1, D_MODEL))


def kernel(x, w_in, b_gate, w_pool, pool_scale, conv_w, conv_b, dt_bias, a_log, d_skip,
           ssm_norm_w, i_bias, f_bias, mlstm_norm_w, w_branch, w_out, ln_g, ln_b):
    lane = lax.broadcasted_iota(jnp.int32, (3, 2 * LANES, D_MODEL), 1) % LANES
    chan = lax.broadcasted_iota(jnp.int32, (3, 2 * LANES, D_MODEL), 2)
    which = lax.broadcasted_iota(jnp.int32, (3, 2 * LANES, D_MODEL), 0)
    expand01 = (lane == which * SSM_HEADS + chan // SSM_HEAD_DIM).astype(BF16)

    w_in_t = jnp.swapaxes(w_in, 1, 2)
    w_branch_bf = w_branch.astype(BF16)
    w_out_bf = w_out.astype(BF16)
    h = x.reshape(TOKENS, D_MODEL)
    h_mm = h
    for l in range(DEPTH):
        h, h_mm = _layer(l, h, h_mm, w_in_t, b_gate[l], w_pool[l], pool_scale[l], conv_w[l],
                         conv_b[l], dt_bias[l], a_log[l], d_skip[l], ssm_norm_w[l], i_bias[l],
                         f_bias[l], mlstm_norm_w[l], w_branch_bf, w_out_bf, ln_g[l], ln_b[l],
                         expand01)
    return h.reshape(BATCH, SEQ, D_MODEL)
```

```python
import jax
import jax.numpy as jnp
from jax import lax
from jax.experimental import pallas as pl
from jax.experimental.pallas import tpu as pltpu

F32 = jnp.float32
BF16 = jnp.bfloat16

D_MODEL = 2048
BATCH = 8
SEQ = 2048
DEPTH = 2
TOKENS = BATCH * SEQ
CHUNK = 128
N_CHUNKS = SEQ // CHUNK
CHUNKS_PER_STEP = 2
STEP_ROWS = CHUNKS_PER_STEP * CHUNK
STEPS_PER_SEQ = N_CHUNKS // CHUNKS_PER_STEP
LANES = 128

POOL_GROUPS = 4
POOL_WINDOWS = (2, 4, 8, 16)
POOL_GDIM = D_MODEL // POOL_GROUPS
POOL_HALO = 16

SSM_HEAD_DIM = 64
SSM_HEADS = D_MODEL // SSM_HEAD_DIM
SSM_GROUPS = 4
SSM_HEADS_PER_GROUP = SSM_HEADS // SSM_GROUPS
SSM_STATE = 128
SSM_CONV = 4
SSM_BC = SSM_GROUPS * SSM_STATE
SSM_GROUP_WIDTH = D_MODEL // SSM_GROUPS
HEADS_PER_TILE = LANES // SSM_HEAD_DIM
CONV_HALO = 16

MLSTM_HEADS = 8
MLSTM_HEAD_DIM = D_MODEL // MLSTM_HEADS
MLSTM_AUG = MLSTM_HEAD_DIM + LANES

N_BRANCH = 3
ALPHA = (2 * DEPTH) ** 0.25
EPS = 1e-5

OFF_SSM_DT = 4 * D_MODEL + 2 * SSM_BC
OFF_MLSTM = OFF_SSM_DT + SSM_HEADS
OFF_MLSTM_GATES = OFF_MLSTM + 5 * D_MODEL
OFF_MERGE_GATES = OFF_MLSTM_GATES + 2 * MLSTM_HEADS
IN_DIM = OFF_MERGE_GATES + N_BRANCH * D_MODEL

SEG_PU, SEG_PZ, SEG_SX, SEG_SZ = range(4)
SEG_SB = 4 * D_MODEL // SSM_BC
SEG_SC = SEG_SB + 1
SEG_MQ, SEG_MK, SEG_MV, SEG_MO, SEG_MZ = range(5)
SMALL_WIDTH = LANES
LANE_DT = 0
LANE_IG = SSM_HEADS
LANE_FG = SSM_HEADS + MLSTM_HEADS

VMEM_LIMIT = 56 * 1024 * 1024


def _params(n_axes):
    return pltpu.CompilerParams(dimension_semantics=("arbitrary",) * n_axes,
                                vmem_limit_bytes=VMEM_LIMIT)


def _sigmoid(v):
    return 0.5 * jnp.tanh(0.5 * v) + 0.5


def _silu(v):
    h = 0.5 * v
    return h * jnp.tanh(h) + h


def _split3(v):
    hi = v.astype(BF16)
    r1 = v - hi.astype(F32)
    mid = r1.astype(BF16)
    lo = (r1 - mid.astype(F32)).astype(BF16)
    return hi, mid, lo


def _dot(a, b):
    return jnp.dot(a, b, preferred_element_type=F32)


def _exact_dot_right01(v, mat01):
    hi, mid, lo = _split3(v)
    return _dot(hi, mat01) + _dot(mid, mat01) + _dot(lo, mat01)


def _tril_mask(n):
    r = lax.broadcasted_iota(jnp.int32, (n, n), 0)
    c = lax.broadcasted_iota(jnp.int32, (n, n), 1)
    return c <= r


def _matmul_kernel(x_ref, w_ref, o_ref):
    o_ref[...] = _dot(x_ref[...], w_ref[...]).astype(o_ref.dtype)


def _matmul(x, w, out_dtype, tm, tn, name):
    m, k = x.shape
    n = w.shape[1]
    return pl.pallas_call(
        _matmul_kernel,
        grid=(m // tm, n // tn),
        in_specs=[pl.BlockSpec((tm, k), lambda i, j: (i, 0)),
                  pl.BlockSpec((k, tn), lambda i, j: (0, j))],
        out_specs=pl.BlockSpec((tm, tn), lambda i, j: (i, j)),
        out_shape=jax.ShapeDtypeStruct((m, n), out_dtype),
        compiler_params=_params(2),
        name=name,
    )(x, w)


PROJ_TM = 2048
PROJ_TN = 1024
GATES_TM = 1024


def _in_proj_kernel(x_ref, wt_ref, o_ref, w_ref):
    @pl.when(pl.program_id(1) == 0)
    def _():
        w_ref[...] = wt_ref[0].astype(BF16)

    o_ref[...] = lax.dot_general(x_ref[...], w_ref[...], (((1,), (1,)), ((), ())),
                                 preferred_element_type=F32).astype(o_ref.dtype)


def _in_proj(x_bf, w_in_t, layer, col_start, width, name):
    m, k = x_bf.shape
    return pl.pallas_call(
        _in_proj_kernel,
        grid=(width // PROJ_TN, m // PROJ_TM),
        in_specs=[pl.BlockSpec((PROJ_TM, k), lambda j, i: (i, 0)),
                  pl.BlockSpec((pl.Element(1), pl.Element(PROJ_TN), pl.Element(k)),
                               lambda j, i: (layer, pl.multiple_of(col_start + j * PROJ_TN, 16), 0))],
        out_specs=pl.BlockSpec((PROJ_TM, PROJ_TN), lambda j, i: (i, j)),
        out_shape=jax.ShapeDtypeStruct((m, width), BF16),
        scratch_shapes=[pltpu.VMEM((PROJ_TN, k), BF16)],
        compiler_params=_params(2),
        name=name,
    )(x_bf, w_in_t)


def _gates_cast_kernel(x_ref, w_ref, o_ref, xbf_ref):
    xb = x_ref[...].astype(BF16)
    xbf_ref[...] = xb
    o_ref[...] = _dot(xb, w_ref[...])


def _gates_matmul(x, w_s):
    if x.dtype == BF16:
        return _matmul(x, w_s, F32, GATES_TM, SMALL_WIDTH, "in_proj_gates"), x
    m, k = x.shape
    row = pl.BlockSpec((GATES_TM, k), lambda i: (i, 0))
    return pl.pallas_call(
        _gates_cast_kernel,
        grid=(m // GATES_TM,),
        in_specs=[row, pl.BlockSpec((k, SMALL_WIDTH), lambda i: (0, 0))],
        out_specs=[pl.BlockSpec((GATES_TM, SMALL_WIDTH), lambda i: (i, 0)), row],
        out_shape=[jax.ShapeDtypeStruct((m, SMALL_WIDTH), F32), jax.ShapeDtypeStruct((m, k), BF16)],
        compiler_params=_params(1),
        name="in_proj_gates_cast",
    )(x, w_s)


def _pack_gates_kernel(dt_ref, gate_ref, o_ref):
    pad = jnp.zeros((SMALL_WIDTH - SSM_HEADS - 2 * MLSTM_HEADS, D_MODEL), F32)
    rows = jnp.concatenate([dt_ref[0], gate_ref[0], pad], axis=0)
    o_ref[...] = rows.T.astype(BF16)


def _pack_gates(w_in_t, layer):
    rows = lambda n, start: pl.BlockSpec((pl.Element(1), pl.Element(n), pl.Element(D_MODEL)),
                                         lambda i: (layer, start, 0))
    return pl.pallas_call(
        _pack_gates_kernel,
        grid=(1,),
        in_specs=[rows(SSM_HEADS, OFF_SSM_DT), rows(2 * MLSTM_HEADS, OFF_MLSTM_GATES)],
        out_specs=pl.BlockSpec((D_MODEL, LANES), lambda i: (0, 0)),
        out_shape=jax.ShapeDtypeStruct((D_MODEL, SMALL_WIDTH), BF16),
        compiler_params=_params(1),
        name="pack_w_gates",
    )(w_in_t, w_in_t)


POOL_TM = 512


POOL_BAND = CHUNK


def _pool_kernel(u_ref, z_ref, w_ref, scale_ref, band_ref, o_ref, ext_ref):
    t = pl.program_id(1)
    tm = POOL_TM

    @pl.when(t == 0)
    def _():
        ext_ref[0:POOL_HALO, :] = jnp.zeros((POOL_HALO, D_MODEL), BF16)

    @pl.when(t != 0)
    def _():
        ext_ref[0:POOL_HALO, :] = ext_ref[tm:tm + POOL_HALO, :]

    ext_ref[POOL_HALO:POOL_HALO + tm, :] = u_ref[...]

    row = lax.broadcasted_iota(jnp.int32, (POOL_BAND, LANES), 0)
    for g, win in enumerate(POOL_WINDOWS):
        cols = slice(g * POOL_GDIM, (g + 1) * POOL_GDIM)
        pooled = []
        for r in range(tm // POOL_BAND):
            r0 = r * POOL_BAND
            win_sum = _dot(band_ref[g], ext_ref[r0:r0 + POOL_HALO + POOL_BAND, cols])
            cur = u_ref[r0:r0 + POOL_BAND, cols].astype(F32)
            inv_cnt = 1.0 / jnp.minimum(t * tm + r0 + row + 1, win).astype(F32)
            inv_cnt = jnp.concatenate([inv_cnt] * (POOL_GDIM // LANES), axis=1)
            pooled.append((win_sum * inv_cnt - cur).astype(BF16))
        mixed = _dot(jnp.concatenate(pooled, axis=0), w_ref[g])
        y = mixed * scale_ref[:, cols] * _silu(z_ref[:, cols].astype(F32))
        o_ref[:, cols] = y.astype(o_ref.dtype)


def _pool_branch(proj_a, w_pool, pool_scale):
    nt = SEQ // POOL_TM
    shape = (POOL_GROUPS, POOL_BAND, POOL_HALO + POOL_BAND)
    back = (POOL_HALO + lax.broadcasted_iota(jnp.int32, shape, 1)
            - lax.broadcasted_iota(jnp.int32, shape, 2))
    width = jnp.asarray(POOL_WINDOWS, jnp.int32)[:, None, None]
    band01 = ((back >= 0) & (back < width)).astype(BF16)
    return pl.pallas_call(
        _pool_kernel,
        grid=(BATCH, nt),
        in_specs=[pl.BlockSpec((POOL_TM, D_MODEL), lambda b, t: (b * nt + t, SEG_PU)),
                  pl.BlockSpec((POOL_TM, D_MODEL), lambda b, t: (b * nt + t, SEG_PZ)),
                  pl.BlockSpec((POOL_GROUPS, POOL_GDIM, POOL_GDIM), lambda b, t: (0, 0, 0)),
                  pl.BlockSpec((1, D_MODEL), lambda b, t: (0, 0)),
                  pl.BlockSpec(shape, lambda b, t: (0, 0, 0))],
        out_specs=pl.BlockSpec((POOL_TM, D_MODEL), lambda b, t: (b * nt + t, 0)),
        out_shape=jax.ShapeDtypeStruct((TOKENS, D_MODEL), BF16),
        scratch_shapes=[pltpu.VMEM((POOL_HALO + POOL_TM, D_MODEL), BF16)],
        compiler_params=_params(2),
        name="pool_mixer",
    )(proj_a, proj_a, w_pool, pool_scale, band01)


SSD_COL_DT, SSD_COL_DTDECAY, SSD_COL_EXPCS, SSD_COL_CS = (k * SSM_HEADS for k in range(4))
ML_COL_C1, ML_COL_CLAMP = 0, MLSTM_HEADS
ML_ROW_G, ML_ROW_WGT = 0, MLSTM_HEADS
ML_CHUNK_MPREV, ML_CHUNK_SOLD = 0, MLSTM_HEADS


def _gate_scan_kernel(small_ref, dtb_ref, alog_ref, gbias_ref,
                      ssd_cols_ref, ssd_rows_ref, ml_cols_ref, ml_rows_ref, ml_chunk_ref):
    nh = MLSTM_HEADS
    r = lax.broadcasted_iota(jnp.int32, (CHUNK, CHUNK), 0)
    cidx = lax.broadcasted_iota(jnp.int32, (CHUNK, CHUNK), 1)
    triu01 = (r <= cidx).astype(BF16)
    lane = lax.broadcasted_iota(jnp.int32, (nh, CHUNK), 1)
    a_coef = -jnp.exp(alog_ref[...])
    m_prev = jnp.zeros((nh, LANES), F32)
    ml_pad = jnp.zeros((CHUNK - 2 * nh, CHUNK), F32)

    for c in range(N_CHUNKS):
        tok = slice(c * CHUNK, (c + 1) * CHUNK)
        small_t = small_ref[tok, :].T

        dt = jax.nn.softplus(small_t[LANE_DT:LANE_DT + SSM_HEADS, :] + dtb_ref[...])
        a_cs = _exact_dot_right01(dt * a_coef, triu01)
        a_last = a_cs[:, CHUNK - 1:CHUNK]
        decay = jnp.exp(a_last - a_cs)
        ssd_rows_ref[0, :, tok] = a_cs
        ssd_cols_ref[tok, :] = jnp.concatenate([dt, dt * decay, jnp.exp(a_cs), a_cs], axis=0).T

        pre = small_t[LANE_IG:LANE_IG + 2 * nh, :] + gbias_ref[...]
        cum = _exact_dot_right01(jax.nn.log_sigmoid(pre), triu01)
        ig = pre[0:nh, :]
        bcum = cum[nh:2 * nh, :]
        g = ig - bcum
        b_last = bcum[:, CHUNK - 1:CHUNK]
        pmax = g
        shift = 1
        while shift < CHUNK:
            pmax = jnp.maximum(pmax, jnp.where(lane >= shift, pltpu.roll(pmax, shift, axis=1), -jnp.inf))
            shift *= 2
        m_t = jnp.maximum(bcum + pmax, bcum + m_prev)
        w_log = b_last + g
        m_loc = jnp.max(w_log, axis=1, keepdims=True)
        m_new = jnp.maximum(b_last + m_prev, m_loc)
        s_old = jnp.exp(b_last + m_prev - m_new)
        s_loc = jnp.exp(m_loc - m_new)
        wgt = jnp.exp(w_log - m_loc) * (s_loc * MLSTM_HEAD_DIM ** -0.5)
        ml_rows_ref[0, :, tok] = jnp.concatenate([g, wgt], axis=0)
        ml_chunk_ref[0, c] = jnp.concatenate([m_prev, s_old], axis=0)
        ml_cols_ref[tok, :] = jnp.concatenate([bcum - m_t, jnp.exp(-m_t), ml_pad], axis=0).T
        m_prev = m_new


def _gate_scan(small, dt_bias, a_log, gate_bias):
    const2 = lambda b: (0, 0)
    return pl.pallas_call(
        _gate_scan_kernel,
        grid=(BATCH,),
        in_specs=[pl.BlockSpec((SEQ, SMALL_WIDTH), lambda b: (b, 0)),
                  pl.BlockSpec((SSM_HEADS, LANES), const2),
                  pl.BlockSpec((SSM_HEADS, LANES), const2),
                  pl.BlockSpec((2 * MLSTM_HEADS, LANES), const2)],
        out_specs=[pl.BlockSpec((SEQ, LANES), lambda b: (b, 0)),
                   pl.BlockSpec((1, SSM_HEADS, SEQ), lambda b: (b, 0, 0)),
                   pl.BlockSpec((SEQ, LANES), lambda b: (b, 0)),
                   pl.BlockSpec((1, 2 * MLSTM_HEADS, SEQ), lambda b: (b, 0, 0)),
                   pl.BlockSpec((1, N_CHUNKS, 2 * MLSTM_HEADS, LANES), lambda b: (b, 0, 0, 0))],
        out_shape=[jax.ShapeDtypeStruct((TOKENS, LANES), F32),
                   jax.ShapeDtypeStruct((BATCH, SSM_HEADS, SEQ), F32),
                   jax.ShapeDtypeStruct((TOKENS, LANES), F32),
                   jax.ShapeDtypeStruct((BATCH, 2 * MLSTM_HEADS, SEQ), F32),
                   jax.ShapeDtypeStruct((BATCH, N_CHUNKS, 2 * MLSTM_HEADS, LANES), F32)],
        compiler_params=_params(1),
        name="gate_scan",
    )(small, dt_bias, a_log, gate_bias)


def _ssd_init(c, extx_ref, extb_ref, extc_ref, state_ref):
    @pl.when(c == 0)
    def _():
        state_ref[...] = jnp.zeros(state_ref.shape, F32)
        for ext_ref in (extx_ref, extb_ref, extc_ref):
            ext_ref[0:CONV_HALO, :] = jnp.zeros((CONV_HALO, ext_ref.shape[1]), BF16)

    @pl.when(c != 0)
    def _():
        for ext_ref in (extx_ref, extb_ref, extc_ref):
            ext_ref[0:CONV_HALO, :] = ext_ref[STEP_ROWS:STEP_ROWS + CONV_HALO, :]


def _conv_silu(sub, ext_ref, new_ref, w_ref, b_ref, shift_ref):
    window = ext_ref[sub * CHUNK:sub * CHUNK + CONV_HALO + CHUNK, :]
    taps = _dot(shift_ref[...], window)
    cur = new_ref[sub * CHUNK:(sub + 1) * CHUNK, :].astype(F32)
    w_half = 0.5 * w_ref[...]
    half = 0.5 * b_ref[...] + w_half[SSM_CONV - 1:SSM_CONV, :] * cur
    for j in range(1, SSM_CONV):
        half = half + w_half[SSM_CONV - 1 - j:SSM_CONV - j, :] * taps[(j - 1) * CHUNK:j * CHUNK, :]
    return half * jnp.tanh(half) + half


N_SSD_IN = 16
N_MLSTM_IN = 9


def _ssd_body(sub, xs_ref, z_ref, b_ref, c_ref, cols_ref, rows_ref,
              cwx_ref, cwb_ref, cwc_ref, cbx_ref, cbb_ref, cbc_ref,
              dskip_ref, normw_ref, expand_ref, shift_ref,
              o_ref, extx_ref, extb_ref, extc_ref, state_ref):
    tok = slice(sub * CHUNK, (sub + 1) * CHUNK)
    xc = _conv_silu(sub, extx_ref, xs_ref, cwx_ref, cbx_ref, shift_ref)
    bc = _conv_silu(sub, extb_ref, b_ref, cwb_ref, cbb_ref, shift_ref)
    cc = _conv_silu(sub, extc_ref, c_ref, cwc_ref, cbc_ref, shift_ref)

    cols = cols_ref[tok, :]
    cols_hi = cols.astype(BF16)
    cols_mid = (cols - cols_hi.astype(F32)).astype(BF16)
    cols_2 = jnp.concatenate([cols_hi, cols_mid], axis=1)
    dt_e = _dot(cols_2, expand_ref[0])
    dtdecay_e = _dot(cols_2, expand_ref[1])
    exp_cs_e = _dot(cols_2, expand_ref[2])
    chunk_decay_e = exp_cs_e[CHUNK - 1:CHUNK, :]
    a_cs_rows = rows_ref[0, :, tok]
    tril = _tril_mask(CHUNK)
    tile_head = lax.broadcasted_iota(jnp.int32, (CHUNK, LANES), 1) // SSM_HEAD_DIM

    xdt_bf = (xc * dt_e).astype(BF16)
    xdecay_bf = (xc * dtdecay_e).astype(BF16)
    cc_bf = cc.astype(BF16)

    for g in range(SSM_GROUPS):
        ncols = slice(g * SSM_STATE, (g + 1) * SSM_STATE)
        wcols = slice(g * SSM_GROUP_WIDTH, (g + 1) * SSM_GROUP_WIDTH)
        bg_t = bc[:, ncols].T.astype(BF16)
        cg = cc_bf[:, ncols]
        cb = _dot(cg, bg_t)
        prev = state_ref[g]
        y_off = _dot(cg, prev.astype(BF16)) * exp_cs_e[:, wcols]
        state_ref[g] = prev * chunk_decay_e[:, wcols] + _dot(bg_t, xdecay_bf[:, wcols])
        y_pairs = []
        for pair in range(SSM_HEADS_PER_GROUP // HEADS_PER_TILE):
            wts = []
            for r in range(HEADS_PER_TILE):
                h = g * SSM_HEADS_PER_GROUP + pair * HEADS_PER_TILE + r
                seg = cols[:, SSD_COL_CS + h:SSD_COL_CS + h + 1] - a_cs_rows[h:h + 1, :]
                lmat = jnp.exp(jnp.where(tril, seg, -jnp.inf))
                wts.append((cb * lmat).astype(BF16))
            tile0 = g * SSM_GROUP_WIDTH + pair * LANES
            x_tile = xdt_bf[:, tile0:tile0 + LANES]
            x_diag = jnp.concatenate(
                [jnp.where(tile_head == r, x_tile, jnp.zeros_like(x_tile)) for r in range(HEADS_PER_TILE)],
                axis=0)
            y_pairs.append(_dot(jnp.concatenate(wts, axis=1), x_diag))
        y = jnp.concatenate(y_pairs, axis=1) + y_off
        y = y + xc[:, wcols] * dskip_ref[:, wcols]
        y = y * _silu(z_ref[tok, wcols].astype(F32))
        y = y * lax.rsqrt(jnp.mean(y * y, axis=-1, keepdims=True) + EPS)
        o_ref[tok, wcols] = (y * normw_ref[:, wcols]).astype(o_ref.dtype)


def _ssd_operands(proj_a, ssd_cols, ssd_rows, conv_w, conv_b, d_skip, norm_w, expand01):
    row = lambda b, c: b * STEPS_PER_SEQ + c
    const2 = lambda b, c: (0, 0)
    n_shift = (SSM_CONV - 1) * CHUNK
    out_row = lax.broadcasted_iota(jnp.int32, (n_shift, CONV_HALO + CHUNK), 0)
    src_row = lax.broadcasted_iota(jnp.int32, (n_shift, CONV_HALO + CHUNK), 1)
    shift01 = (src_row == CONV_HALO + out_row % CHUNK - out_row // CHUNK - 1).astype(BF16)
    cwx, cwb, cwc = conv_w[:, :D_MODEL], conv_w[:, D_MODEL:D_MODEL + SSM_BC], conv_w[:, D_MODEL + SSM_BC:]
    cbx, cbb, cbc = conv_b[:, :D_MODEL], conv_b[:, D_MODEL:D_MODEL + SSM_BC], conv_b[:, D_MODEL + SSM_BC:]
    in_specs = [pl.BlockSpec((STEP_ROWS, D_MODEL), lambda b, c: (row(b, c), SEG_SX)),
                pl.BlockSpec((STEP_ROWS, D_MODEL), lambda b, c: (row(b, c), SEG_SZ)),
                pl.BlockSpec((STEP_ROWS, SSM_BC), lambda b, c: (row(b, c), SEG_SB)),
                pl.BlockSpec((STEP_ROWS, SSM_BC), lambda b, c: (row(b, c), SEG_SC)),
                pl.BlockSpec((STEP_ROWS, LANES), lambda b, c: (row(b, c), 0)),
                pl.BlockSpec((1, SSM_HEADS, STEP_ROWS), lambda b, c: (b, 0, c)),
                pl.BlockSpec((SSM_CONV, D_MODEL), const2),
                pl.BlockSpec((SSM_CONV, SSM_BC), const2),
                pl.BlockSpec((SSM_CONV, SSM_BC), const2),
                pl.BlockSpec((1, D_MODEL), const2),
                pl.BlockSpec((1, SSM_BC), const2),
                pl.BlockSpec((1, SSM_BC), const2),
                pl.BlockSpec((1, D_MODEL), const2),
                pl.BlockSpec((1, D_MODEL), const2),
                pl.BlockSpec((3, 2 * LANES, D_MODEL), lambda b, c: (0, 0, 0)),
                pl.BlockSpec((n_shift, CONV_HALO + CHUNK), const2)]
    operands = [proj_a, proj_a, proj_a, proj_a, ssd_cols, ssd_rows, cwx, cwb, cwc, cbx, cbb, cbc,
                d_skip, norm_w, expand01, shift01]
    scratch = [pltpu.VMEM((CONV_HALO + STEP_ROWS, D_MODEL), BF16),
               pltpu.VMEM((CONV_HALO + STEP_ROWS, SSM_BC), BF16),
               pltpu.VMEM((CONV_HALO + STEP_ROWS, SSM_BC), BF16),
               pltpu.VMEM((SSM_GROUPS, SSM_STATE, SSM_GROUP_WIDTH), F32)]
    assert len(in_specs) == len(operands) == N_SSD_IN
    return in_specs, operands, scratch


def _mlstm_body(sub, q_ref, k_ref, v_ref, og_ref, z_ref, cols_ref, rows_ref, chunk_ref, normw_ref,
                o_ref, cstate_ref):
    tok = slice(sub * CHUNK, (sub + 1) * CHUNK)
    nh = MLSTM_HEADS
    n_tiles = MLSTM_AUG // LANES
    v_tiles = MLSTM_HEAD_DIM // LANES

    cols_t = cols_ref[tok, :]
    rows = rows_ref[0, :, tok]
    chunk = chunk_ref[0, sub]
    tril = _tril_mask(CHUNK)
    ones_tile = jnp.ones((CHUNK, LANES), BF16)
    mean_mat = jnp.full((MLSTM_HEAD_DIM, LANES), 1.0 / MLSTM_HEAD_DIM, BF16)
    k_scale = MLSTM_HEAD_DIM ** -0.5

    for h in range(nh):
        cols = slice(h * MLSTM_HEAD_DIM, (h + 1) * MLSTM_HEAD_DIM)
        qh = q_ref[tok, cols]
        k_t = k_ref[tok, cols].T
        v_aug = jnp.concatenate([v_ref[tok, cols], ones_tile], axis=1)
        c_prev = cstate_ref[h]
        g_row = rows[ML_ROW_G + h:ML_ROW_G + h + 1, :]
        wgt_row = rows[ML_ROW_WGT + h:ML_ROW_WGT + h + 1, :]
        m_prev_row = chunk[ML_CHUNK_MPREV + h:ML_CHUNK_MPREV + h + 1, :]
        s_old_row = chunk[ML_CHUNK_SOLD + h:ML_CHUNK_SOLD + h + 1, :]

        c1_d = jnp.broadcast_to(cols_t[:, ML_COL_C1 + h:ML_COL_C1 + h + 1], (CHUNK, CHUNK))
        clamp_d = jnp.broadcast_to(cols_t[:, ML_COL_CLAMP + h:ML_COL_CLAMP + h + 1], (CHUNK, CHUNK))
        dw = jnp.exp(jnp.where(tril, c1_d + g_row, -jnp.inf))
        inter_w = jnp.exp(c1_d + m_prev_row)

        s = _dot(qh, k_t) * k_scale
        intra = _dot((s * dw).astype(BF16), v_aug)
        inter = _dot(qh, c_prev.astype(BF16))
        nd = [intra[:, j * LANES:(j + 1) * LANES] + inter_w * inter[:, j * LANES:(j + 1) * LANES]
              for j in range(n_tiles)]
        rden = 1.0 / jnp.maximum(jnp.abs(nd[-1]), clamp_d)

        wk_t = (k_t.astype(F32) * wgt_row).astype(BF16)
        cstate_ref[h] = (c_prev * jnp.concatenate([s_old_row] * n_tiles, axis=1)
                         + _dot(wk_t, v_aug))

        og = _sigmoid(og_ref[tok, cols].astype(F32))
        hc = jnp.concatenate([nd[j] * rden for j in range(v_tiles)], axis=1) * og
        mu = _dot(hc.astype(BF16), mean_mat)
        dev = hc - jnp.concatenate([mu] * v_tiles, axis=1)
        var = _dot((dev * dev).astype(BF16), mean_mat)
        y = dev * jnp.concatenate([lax.rsqrt(var + EPS)] * v_tiles, axis=1)
        y = y * normw_ref[:, cols] * _silu(z_ref[tok, cols].astype(F32))
        o_ref[tok, cols] = y.astype(o_ref.dtype)


def _mlstm_operands(proj_m, ml_cols, ml_rows, ml_chunk, norm_w):
    row = lambda b, c: b * STEPS_PER_SEQ + c
    seg = lambda s: pl.BlockSpec((STEP_ROWS, D_MODEL), lambda b, c: (row(b, c), s))
    in_specs = [seg(SEG_MQ), seg(SEG_MK), seg(SEG_MV), seg(SEG_MO), seg(SEG_MZ),
                pl.BlockSpec((STEP_ROWS, LANES), lambda b, c: (row(b, c), 0)),
                pl.BlockSpec((1, 2 * MLSTM_HEADS, STEP_ROWS), lambda b, c: (b, 0, c)),
                pl.BlockSpec((1, CHUNKS_PER_STEP, 2 * MLSTM_HEADS, LANES), lambda b, c: (b, c, 0, 0)),
                pl.BlockSpec((1, D_MODEL), lambda b, c: (0, 0))]
    operands = [proj_m, proj_m, proj_m, proj_m, proj_m, ml_cols, ml_rows, ml_chunk, norm_w]
    scratch = [pltpu.VMEM((MLSTM_HEADS, MLSTM_HEAD_DIM, MLSTM_AUG), F32)]
    assert len(in_specs) == len(operands) == N_MLSTM_IN
    return in_specs, operands, scratch


def _recurrent_mixers_kernel(*refs):
    ssd_in = refs[:N_SSD_IN]
    ml_in = refs[N_SSD_IN:N_SSD_IN + N_MLSTM_IN]
    o_ssd_ref, o_ml_ref = refs[N_SSD_IN + N_MLSTM_IN:N_SSD_IN + N_MLSTM_IN + 2]
    extx_ref, extb_ref, extc_ref, state_ref, cstate_ref = refs[N_SSD_IN + N_MLSTM_IN + 2:]
    c = pl.program_id(1)
    _ssd_init(c, extx_ref, extb_ref, extc_ref, state_ref)

    @pl.when(c == 0)
    def _():
        cstate_ref[...] = jnp.zeros(cstate_ref.shape, F32)

    xs_ref, _, b_ref, c_ref = ssd_in[:4]
    for ext_ref, new_ref in ((extx_ref, xs_ref), (extb_ref, b_ref), (extc_ref, c_ref)):
        ext_ref[CONV_HALO:CONV_HALO + STEP_ROWS, :] = new_ref[...]
    for sub in range(CHUNKS_PER_STEP):
        _ssd_body(sub, *ssd_in, o_ssd_ref, extx_ref, extb_ref, extc_ref, state_ref)
        _mlstm_body(sub, *ml_in, o_ml_ref, cstate_ref)


def _recurrent_mixers(ssd_args, mlstm_args):
    s_specs, s_ops, s_scratch = _ssd_operands(*ssd_args)
    m_specs, m_ops, m_scratch = _mlstm_operands(*mlstm_args)
    out_spec = pl.BlockSpec((STEP_ROWS, D_MODEL), lambda b, c: (b * STEPS_PER_SEQ + c, 0))
    out_shape = jax.ShapeDtypeStruct((TOKENS, D_MODEL), BF16)
    return pl.pallas_call(
        _recurrent_mixers_kernel,
        grid=(BATCH, STEPS_PER_SEQ),
        in_specs=s_specs + m_specs,
        out_specs=[out_spec, out_spec],
        out_shape=[out_shape, out_shape],
        scratch_shapes=s_scratch + m_scratch,
        compiler_params=_params(2),
        name="recurrent_mixers",
    )(*s_ops, *m_ops)


MERGE_TM = 512
MERGE_TN = 1024


def _merge_kernel(y0_ref, y1_ref, y2_ref, g0_ref, g1_ref, g2_ref, bg_ref, w_ref, o_ref):
    acc = None
    for b, (y_ref, g_ref) in enumerate(((y0_ref, g0_ref), (y1_ref, g1_ref), (y2_ref, g2_ref))):
        gate = _sigmoid(g_ref[...].astype(F32) + bg_ref[b:b + 1, :])
        term = gate * _dot(y_ref[...], w_ref[b])
        acc = term if acc is None else acc + term
    o_ref[...] = acc.astype(o_ref.dtype)


def _merge(layer, y_pool, y_ssm, y_mlstm, proj_g, b_gate, w_branch_all):
    tm, tn = MERGE_TM, MERGE_TN
    per_seg = D_MODEL // tn
    y_spec = pl.BlockSpec((tm, D_MODEL), lambda i, j: (i, 0))
    gate = lambda b: pl.BlockSpec((tm, tn), lambda i, j: (i, b * per_seg + j))
    return pl.pallas_call(
        _merge_kernel,
        grid=(TOKENS // tm, D_MODEL // tn),
        in_specs=[y_spec, y_spec, y_spec, gate(0), gate(1), gate(2),
                  pl.BlockSpec((N_BRANCH, tn), lambda i, j: (0, j)),
                  pl.BlockSpec((None, N_BRANCH, D_MODEL, tn), lambda i, j: (layer, 0, 0, j))],
        out_specs=pl.BlockSpec((tm, tn), lambda i, j: (i, j)),
        out_shape=jax.ShapeDtypeStruct((TOKENS, D_MODEL), BF16),
        compiler_params=_params(2),
        name="branch_merge",
    )(y_pool, y_ssm, y_mlstm, proj_g, proj_g, proj_g, b_gate, w_branch_all)


OUT_TM = 512


def _out_kernel(m_ref, x_ref, w_ref, g_ref, b_ref, o_ref, obf_ref):
    half = OUT_TM // 2
    for r in range(2):
        rows = slice(r * half, (r + 1) * half)
        h = ALPHA * x_ref[rows, :] + _dot(m_ref[rows, :], w_ref[...])
        mu = jnp.mean(h, axis=-1, keepdims=True)
        var = jnp.mean(jnp.square(h - mu), axis=-1, keepdims=True)
        y = (h - mu) * lax.rsqrt(var + EPS) * g_ref[...] + b_ref[...]
        o_ref[rows, :] = y
        obf_ref[rows, :] = y.astype(BF16)


def _out_proj(layer, merged, x, w_out_all, ln_g, ln_b):
    tm = OUT_TM
    row = pl.BlockSpec((tm, D_MODEL), lambda i: (i, 0))
    vec = pl.BlockSpec((1, D_MODEL), lambda i: (0, 0))
    return pl.pallas_call(
        _out_kernel,
        grid=(TOKENS // tm,),
        in_specs=[row, row, pl.BlockSpec((None, D_MODEL, D_MODEL), lambda i: (layer, 0, 0)), vec, vec],
        out_specs=[row, row],
        out_shape=[jax.ShapeDtypeStruct((TOKENS, D_MODEL), F32),
                   jax.ShapeDtypeStruct((TOKENS, D_MODEL), BF16)],
        compiler_params=_params(1),
        name="out_proj_ln",
    )(merged, x, w_out_all, ln_g, ln_b)


def _lane_bcast(vec):
    return jnp.broadcast_to(vec.astype(F32)[:, None], (vec.shape[0], LANES))


def _layer(layer, x, x_mm, w_in_all, b_gate, w_pool, pool_scale, conv_w, conv_b, dt_bias, a_log,
           d_skip, ssm_norm_w, i_bias, f_bias, mlstm_norm_w, w_branch, w_out, ln_g, ln_b, expand01):
    w_s = _pack_gates(w_in_all, layer)
    small, x_bf = _gates_matmul(x_mm, w_s)
    proj_a = _in_proj(x_bf, w_in_all, layer, 0, OFF_SSM_DT, "in_proj_a")
    proj_m = _in_proj(x_bf, w_in_all, layer, OFF_MLSTM, 5 * D_MODEL, "in_proj_m")
    proj_g = _in_proj(x_bf, w_in_all, layer, OFF_MERGE_GATES, N_BRANCH * D_MODEL, "in_proj_g")

    ssd_cols, ssd_rows, ml_cols, ml_rows, ml_chunk = _gate_scan(
        small, _lane_bcast(dt_bias), _lane_bcast(a_log),
        _lane_bcast(jnp.concatenate([i_bias, f_bias])))

    y_pool = _pool_branch(proj_a, w_pool.astype(BF16), pool_scale.reshape(1, D_MODEL))
    y_ssm, y_mlstm = _recurrent_mixers(
        (proj_a, ssd_cols, ssd_rows, conv_w, conv_b.reshape(1, -1),
         jnp.repeat(d_skip, SSM_HEAD_DIM).reshape(1, D_MODEL),
         ssm_norm_w.reshape(1, D_MODEL), expand01),
        (proj_m, ml_cols, ml_rows, ml_chunk, mlstm_norm_w.reshape(1, D_MODEL)))
    merged = _merge(layer, y_pool, y_ssm, y_mlstm, proj_g, b_gate, w_branch)
    return _out_proj(layer, merged, x, w_out, ln_g.reshape(1, D_MODEL), ln_b.reshape(1, D_MODEL))


def kernel(x, w_in, b_gate, w_pool, pool_scale, conv_w, conv_b, dt_bias, a_log, d_skip,
           ssm_norm_w, i_bias, f_bias, mlstm_norm_w, w_branch, w_out, ln_g, ln_b):
    lane = lax.broadcasted_iota(jnp.int32, (3, 2 * LANES, D_MODEL), 1) % LANES
    chan = lax.broadcasted_iota(jnp.int32, (3, 2 * LANES, D_MODEL), 2)
    which = lax.broadcasted_iota(jnp.int32, (3, 2 * LANES, D_MODEL), 0)
    expand01 = (lane == which * SSM_HEADS + chan // SSM_HEAD_DIM).astype(BF16)

    w_in_t = jnp.swapaxes(w_in, 1, 2)
    w_branch_bf = w_branch.astype(BF16)
    w_out_bf = w_out.astype(BF16)
    h = x.reshape(TOKENS, D_MODEL)
    h_mm = h
    for l in range(DEPTH):
        h, h_mm = _layer(l, h, h_mm, w_in_t, b_gate[l], w_pool[l], pool_scale[l], conv_w[l],
                         conv_b[l], dt_bias[l], a_log[l], d_skip[l], ssm_norm_w[l], i_bias[l],
                         f_bias[l], mlstm_norm_w[l], w_branch_bf, w_out_bf, ln_g[l], ln_b[l],
                         expand01)
    return h.reshape(BATCH, SEQ, D_MODEL)
```

```python
import jax
import jax.numpy as jnp
from jax import lax
from jax.experimental import pallas as pl
from jax.experimental.pallas import tpu as pltpu

F32 = jnp.float32
BF16 = jnp.bfloat16

D_MODEL = 2048
BATCH = 8
SEQ = 2048
DEPTH = 2
TOKENS = BATCH * SEQ
CHUNK = 128
N_CHUNKS = SEQ // CHUNK
CHUNKS_PER_STEP = 2
STEP_ROWS = CHUNKS_PER_STEP * CHUNK
STEPS_PER_SEQ = N_CHUNKS // CHUNKS_PER_STEP
LANES = 128

POOL_GROUPS = 4
POOL_WINDOWS = (2, 4, 8, 16)
POOL_GDIM = D_MODEL // POOL_GROUPS
POOL_HALO = 16

SSM_HEAD_DIM = 64
SSM_HEADS = D_MODEL // SSM_HEAD_DIM
SSM_GROUPS = 4
SSM_HEADS_PER_GROUP = SSM_HEADS // SSM_GROUPS
SSM_STATE = 128
SSM_CONV = 4
SSM_BC = SSM_GROUPS * SSM_STATE
SSM_GROUP_WIDTH = D_MODEL // SSM_GROUPS
HEADS_PER_TILE = LANES // SSM_HEAD_DIM
CONV_HALO = 16

MLSTM_HEADS = 8
MLSTM_HEAD_DIM = D_MODEL // MLSTM_HEADS
MLSTM_AUG = MLSTM_HEAD_DIM + LANES

N_BRANCH = 3
ALPHA = (2 * DEPTH) ** 0.25
EPS = 1e-5

OFF_SSM_DT = 4 * D_MODEL + 2 * SSM_BC
OFF_MLSTM = OFF_SSM_DT + SSM_HEADS
OFF_MLSTM_GATES = OFF_MLSTM + 5 * D_MODEL
OFF_MERGE_GATES = OFF_MLSTM_GATES + 2 * MLSTM_HEADS
IN_DIM = OFF_MERGE_GATES + N_BRANCH * D_MODEL

SEG_PU, SEG_PZ, SEG_SX, SEG_SZ = range(4)
SEG_SB = 4 * D_MODEL // SSM_BC
SEG_SC = SEG_SB + 1
SEG_MQ, SEG_MK, SEG_MV, SEG_MO, SEG_MZ = range(5)
SMALL_WIDTH = LANES
LANE_DT = 0
LANE_IG = SSM_HEADS
LANE_FG = SSM_HEADS + MLSTM_HEADS

VMEM_LIMIT = 56 * 1024 * 1024


def _params(n_axes):
    return pltpu.CompilerParams(dimension_semantics=("arbitrary",) * n_axes,
                                vmem_limit_bytes=VMEM_LIMIT)


def _sigmoid(v):
    return 0.5 * jnp.tanh(0.5 * v) + 0.5


def _silu(v):
    h = 0.5 * v
    return h * jnp.tanh(h) + h


def _split3(v):
    hi = v.astype(BF16)
    r1 = v - hi.astype(F32)
    mid = r1.astype(BF16)
    lo = (r1 - mid.astype(F32)).astype(BF16)
    return hi, mid, lo


def _dot(a, b):
    return jnp.dot(a, b, preferred_element_type=F32)


def _exact_dot_right01(v, mat01):
    hi, mid, lo = _split3(v)
    return _dot(hi, mat01) + _dot(mid, mat01) + _dot(lo, mat01)


def _tril_mask(n):
    r = lax.broadcasted_iota(jnp.int32, (n, n), 0)
    c = lax.broadcasted_iota(jnp.int32, (n, n), 1)
    return c <= r


def _matmul_kernel(x_ref, w_ref, o_ref):
    o_ref[...] = _dot(x_ref[...], w_ref[...]).astype(o_ref.dtype)


def _matmul(x, w, out_dtype, tm, tn, name):
    m, k = x.shape
    n = w.shape[1]
    return pl.pallas_call(
        _matmul_kernel,
        grid=(m // tm, n // tn),
        in_specs=[pl.BlockSpec((tm, k), lambda i, j: (i, 0)),
                  pl.BlockSpec((k, tn), lambda i, j: (0, j))],
        out_specs=pl.BlockSpec((tm, tn), lambda i, j: (i, j)),
        out_shape=jax.ShapeDtypeStruct((m, n), out_dtype),
        compiler_params=_params(2),
        name=name,
    )(x, w)


PROJ_TM = 2048
PROJ_TN = 1024
GATES_TM = 1024


def _in_proj_kernel(x_ref, wt_ref, o_ref, w_ref):
    @pl.when(pl.program_id(1) == 0)
    def _():
        w_ref[...] = wt_ref[0].astype(BF16)

    o_ref[...] = lax.dot_general(x_ref[...], w_ref[...], (((1,), (1,)), ((), ())),
                                 preferred_element_type=F32).astype(o_ref.dtype)


def _in_proj(x_bf, w_in_t, layer, col_start, width, name):
    m, k = x_bf.shape
    return pl.pallas_call(
        _in_proj_kernel,
        grid=(width // PROJ_TN, m // PROJ_TM),
        in_specs=[pl.BlockSpec((PROJ_TM, k), lambda j, i: (i, 0)),
                  pl.BlockSpec((pl.Element(1), pl.Element(PROJ_TN), pl.Element(k)),
                               lambda j, i: (layer, pl.multiple_of(col_start + j * PROJ_TN, 16), 0))],
        out_specs=pl.BlockSpec((PROJ_TM, PROJ_TN), lambda j, i: (i, j)),
        out_shape=jax.ShapeDtypeStruct((m, width), BF16),
        scratch_shapes=[pltpu.VMEM((PROJ_TN, k), BF16)],
        compiler_params=_params(2),
        name=name,
    )(x_bf, w_in_t)


def _gates_cast_kernel(x_ref, w_ref, o_ref, xbf_ref):
    xb = x_ref[...].astype(BF16)
    xbf_ref[...] = xb
    o_ref[...] = _dot(xb, w_ref[...])


def _gates_matmul(x, w_s):
    if x.dtype == BF16:
        return _matmul(x, w_s, F32, GATES_TM, SMALL_WIDTH, "in_proj_gates"), x
    m, k = x.shape
    row = pl.BlockSpec((GATES_TM, k), lambda i: (i, 0))
    return pl.pallas_call(
        _gates_cast_kernel,
        grid=(m // GATES_TM,),
        in_specs=[row, pl.BlockSpec((k, SMALL_WIDTH), lambda i: (0, 0))],
        out_specs=[pl.BlockSpec((GATES_TM, SMALL_WIDTH), lambda i: (i, 0)), row],
        out_shape=[jax.ShapeDtypeStruct((m, SMALL_WIDTH), F32), jax.ShapeDtypeStruct((m, k), BF16)],
        compiler_params=_params(1),
        name="in_proj_gates_cast",
    )(x, w_s)


def _pack_gates_kernel(dt_ref, gate_ref, o_ref):
    pad = jnp.zeros((SMALL_WIDTH - SSM_HEADS - 2 * MLSTM_HEADS, D_MODEL), F32)
    rows = jnp.concatenate([dt_ref[0], gate_ref[0], pad], axis=0)
    o_ref[...] = rows.T.astype(BF16)


def _pack_gates(w_in_t, layer):
    rows = lambda n, start: pl.BlockSpec((pl.Element(1), pl.Element(n), pl.Element(D_MODEL)),
                                         lambda i: (layer, start, 0))
    return pl.pallas_call(
        _pack_gates_kernel,
        grid=(1,),
        in_specs=[rows(SSM_HEADS, OFF_SSM_DT), rows(2 * MLSTM_HEADS, OFF_MLSTM_GATES)],
        out_specs=pl.BlockSpec((D_MODEL, LANES), lambda i: (0, 0)),
        out_shape=jax.ShapeDtypeStruct((D_MODEL, SMALL_WIDTH), BF16),
        compiler_params=_params(1),
        name="pack_w_gates",
    )(w_in_t, w_in_t)


POOL_TM = 512


POOL_BAND = CHUNK


def _pool_kernel(u_ref, z_ref, w_ref, scale_ref, band_ref, o_ref, ext_ref):
    t = pl.program_id(1)
    tm = POOL_TM

    @pl.when(t == 0)
    def _():
        ext_ref[0:POOL_HALO, :] = jnp.zeros((POOL_HALO, D_MODEL), BF16)

    @pl.when(t != 0)
    def _():
        ext_ref[0:POOL_HALO, :] = ext_ref[tm:tm + POOL_HALO, :]

    ext_ref[POOL_HALO:POOL_HALO + tm, :] = u_ref[...]

    row = lax.broadcasted_iota(jnp.int32, (POOL_BAND, LANES), 0)
    for g, win in enumerate(POOL_WINDOWS):
        cols = slice(g * POOL_GDIM, (g + 1) * POOL_GDIM)
        pooled = []
        for r in range(tm // POOL_BAND):
            r0 = r * POOL_BAND
            win_sum = _dot(band_ref[g], ext_ref[r0:r0 + POOL_HALO + POOL_BAND, cols])
            cur = u_ref[r0:r0 + POOL_BAND, cols].astype(F32)
            inv_cnt = 1.0 / jnp.minimum(t * tm + r0 + row + 1, win).astype(F32)
            inv_cnt = jnp.concatenate([inv_cnt] * (POOL_GDIM // LANES), axis=1)
            pooled.append((win_sum * inv_cnt - cur).astype(BF16))
        mixed = _dot(jnp.concatenate(pooled, axis=0), w_ref[g])
        y = mixed * scale_ref[:, cols] * _silu(z_ref[:, cols].astype(F32))
        o_ref[:, cols] = y.astype(o_ref.dtype)


def _pool_branch(proj_a, w_pool, pool_scale):
    nt = SEQ // POOL_TM
    shape = (POOL_GROUPS, POOL_BAND, POOL_HALO + POOL_BAND)
    back = (POOL_HALO + lax.broadcasted_iota(jnp.int32, shape, 1)
            - lax.broadcasted_iota(jnp.int32, shape, 2))
    width = jnp.asarray(POOL_WINDOWS, jnp.int32)[:, None, None]
    band01 = ((back >= 0) & (back < width)).astype(BF16)
    return pl.pallas_call(
        _pool_kernel,
        grid=(BATCH, nt),
        in_specs=[pl.BlockSpec((POOL_TM, D_MODEL), lambda b, t: (b * nt + t, SEG_PU)),
                  pl.BlockSpec((POOL_TM, D_MODEL), lambda b, t: (b * nt + t, SEG_PZ)),
                  pl.BlockSpec((POOL_GROUPS, POOL_GDIM, POOL_GDIM), lambda b, t: (0, 0, 0)),
                  pl.BlockSpec((1, D_MODEL), lambda b, t: (0, 0)),
                  pl.BlockSpec(shape, lambda b, t: (0, 0, 0))],
        out_specs=pl.BlockSpec((POOL_TM, D_MODEL), lambda b, t: (b * nt + t, 0)),
        out_shape=jax.ShapeDtypeStruct((TOKENS, D_MODEL), BF16),
        scratch_shapes=[pltpu.VMEM((POOL_HALO + POOL_TM, D_MODEL), BF16)],
        compiler_params=_params(2),
        name="pool_mixer",
    )(proj_a, proj_a, w_pool, pool_scale, band01)


SSD_COL_DT, SSD_COL_DTDECAY, SSD_COL_EXPCS, SSD_COL_CS = (k * SSM_HEADS for k in range(4))
ML_COL_C1, ML_COL_CLAMP = 0, MLSTM_HEADS
ML_ROW_G, ML_ROW_WGT = 0, MLSTM_HEADS
ML_CHUNK_MPREV, ML_CHUNK_SOLD = 0, MLSTM_HEADS


def _gate_scan_kernel(small_ref, dtb_ref, alog_ref, gbias_ref,
                      ssd_cols_ref, ssd_rows_ref, ml_cols_ref, ml_rows_ref, ml_chunk_ref):
    nh = MLSTM_HEADS
    r = lax.broadcasted_iota(jnp.int32, (CHUNK, CHUNK), 0)
    cidx = lax.broadcasted_iota(jnp.int32, (CHUNK, CHUNK), 1)
    triu01 = (r <= cidx).astype(BF16)
    lane = lax.broadcasted_iota(jnp.int32, (nh, CHUNK), 1)
    a_coef = -jnp.exp(alog_ref[...])
    m_prev = jnp.zeros((nh, LANES), F32)
    ml_pad = jnp.zeros((CHUNK - 2 * nh, CHUNK), F32)

    for c in range(N_CHUNKS):
        tok = slice(c * CHUNK, (c + 1) * CHUNK)
        small_t = small_ref[tok, :].T

        dt = jax.nn.softplus(small_t[LANE_DT:LANE_DT + SSM_HEADS, :] + dtb_ref[...])
        a_cs = _exact_dot_right01(dt * a_coef, triu01)
        a_last = a_cs[:, CHUNK - 1:CHUNK]
        decay = jnp.exp(a_last - a_cs)
        ssd_rows_ref[0, :, tok] = a_cs
        ssd_cols_ref[tok, :] = jnp.concatenate([dt, dt * decay, jnp.exp(a_cs), a_cs], axis=0).T

        pre = small_t[LANE_IG:LANE_IG + 2 * nh, :] + gbias_ref[...]
        cum = _exact_dot_right01(jax.nn.log_sigmoid(pre), triu01)
        ig = pre[0:nh, :]
        bcum = cum[nh:2 * nh, :]
        g = ig - bcum
        b_last = bcum[:, CHUNK - 1:CHUNK]
        pmax = g
        shift = 1
        while shift < CHUNK:
            pmax = jnp.maximum(pmax, jnp.where(lane >= shift, pltpu.roll(pmax, shift, axis=1), -jnp.inf))
            shift *= 2
        m_t = jnp.maximum(bcum + pmax, bcum + m_prev)
        w_log = b_last + g
        m_loc = jnp.max(w_log, axis=1, keepdims=True)
        m_new = jnp.maximum(b_last + m_prev, m_loc)
        s_old = jnp.exp(b_last + m_prev - m_new)
        s_loc = jnp.exp(m_loc - m_new)
        wgt = jnp.exp(w_log - m_loc) * (s_loc * MLSTM_HEAD_DIM ** -0.5)
        ml_rows_ref[0, :, tok] = jnp.concatenate([g, wgt], axis=0)
        ml_chunk_ref[0, c] = jnp.concatenate([m_prev, s_old], axis=0)
        ml_cols_ref[tok, :] = jnp.concatenate([bcum - m_t, jnp.exp(-m_t), ml_pad], axis=0).T
        m_prev = m_new


def _gate_scan(small, dt_bias, a_log, gate_bias):
    const2 = lambda b: (0, 0)
    return pl.pallas_call(
        _gate_scan_kernel,
        grid=(BATCH,),
        in_specs=[pl.BlockSpec((SEQ, SMALL_WIDTH), lambda b: (b, 0)),
                  pl.BlockSpec((SSM_HEADS, LANES), const2),
                  pl.BlockSpec((SSM_HEADS, LANES), const2),
                  pl.BlockSpec((2 * MLSTM_HEADS, LANES), const2)],
        out_specs=[pl.BlockSpec((SEQ, LANES), lambda b: (b, 0)),
                   pl.BlockSpec((1, SSM_HEADS, SEQ), lambda b: (b, 0, 0)),
                   pl.BlockSpec((SEQ, LANES), lambda b: (b, 0)),
                   pl.BlockSpec((1, 2 * MLSTM_HEADS, SEQ), lambda b: (b, 0, 0)),
                   pl.BlockSpec((1, N_CHUNKS, 2 * MLSTM_HEADS, LANES), lambda b: (b, 0, 0, 0))],
        out_shape=[jax.ShapeDtypeStruct((TOKENS, LANES), F32),
                   jax.ShapeDtypeStruct((BATCH, SSM_HEADS, SEQ), F32),
                   jax.ShapeDtypeStruct((TOKENS, LANES), F32),
                   jax.ShapeDtypeStruct((BATCH, 2 * MLSTM_HEADS, SEQ), F32),
                   jax.ShapeDtypeStruct((BATCH, N_CHUNKS, 2 * MLSTM_HEADS, LANES), F32)],
        compiler_params=_params(1),
        name="gate_scan",
    )(small, dt_bias, a_log, gate_bias)


def _ssd_init(c, extx_ref, extb_ref, extc_ref, state_ref):
    @pl.when(c == 0)
    def _():
        state_ref[...] = jnp.zeros(state_ref.shape, F32)
        for ext_ref in (extx_ref, extb_ref, extc_ref):
            ext_ref[0:CONV_HALO, :] = jnp.zeros((CONV_HALO, ext_ref.shape[1]), BF16)

    @pl.when(c != 0)
    def _():
        for ext_ref in (extx_ref, extb_ref, extc_ref):
            ext_ref[0:CONV_HALO, :] = ext_ref[STEP_ROWS:STEP_ROWS + CONV_HALO, :]


def _conv_silu(sub, ext_ref, new_ref, w_ref, b_ref, shift_ref, cols=slice(None)):
    window = ext_ref[sub * CHUNK:sub * CHUNK + CONV_HALO + CHUNK, cols]
    taps = _dot(shift_ref[...], window)
    cur = new_ref[sub * CHUNK:(sub + 1) * CHUNK, cols].astype(F32)
    w_half = 0.5 * w_ref[:, cols]
    half = 0.5 * b_ref[:, cols] + w_half[SSM_CONV - 1:SSM_CONV, :] * cur
    for j in range(1, SSM_CONV):
        half = half + w_half[SSM_CONV - 1 - j:SSM_CONV - j, :] * taps[(j - 1) * CHUNK:j * CHUNK, :]
    return half * jnp.tanh(half) + half


N_SSD_IN = 16
N_MLSTM_IN = 9
MLSTM_HEADS_PER_SLICE = 2
_DONE = object()


def _ssd_body(sub, xs_ref, z_ref, b_ref, c_ref, cols_ref, rows_ref,
              cwx_ref, cwb_ref, cwc_ref, cbx_ref, cbb_ref, cbc_ref,
              dskip_ref, normw_ref, expand_ref, shift_ref,
              o_ref, extx_ref, extb_ref, extc_ref, state_ref):
    tok = slice(sub * CHUNK, (sub + 1) * CHUNK)
    bc = _conv_silu(sub, extb_ref, b_ref, cwb_ref, cbb_ref, shift_ref)
    cc = _conv_silu(sub, extc_ref, c_ref, cwc_ref, cbc_ref, shift_ref)

    cols = cols_ref[tok, :]
    cols_hi = cols.astype(BF16)
    cols_mid = (cols - cols_hi.astype(F32)).astype(BF16)
    cols_2 = jnp.concatenate([cols_hi, cols_mid], axis=1)
    a_cs_rows = rows_ref[0, :, tok]
    tril = _tril_mask(CHUNK)
    tile_head = lax.broadcasted_iota(jnp.int32, (CHUNK, LANES), 1) // SSM_HEAD_DIM
    cc_bf = cc.astype(BF16)
    yield

    for g in range(SSM_GROUPS):
        ncols = slice(g * SSM_STATE, (g + 1) * SSM_STATE)
        wcols = slice(g * SSM_GROUP_WIDTH, (g + 1) * SSM_GROUP_WIDTH)
        xc = _conv_silu(sub, extx_ref, xs_ref, cwx_ref, cbx_ref, shift_ref, wcols)
        dt_e = _dot(cols_2, expand_ref[0, :, wcols])
        dtdecay_e = _dot(cols_2, expand_ref[1, :, wcols])
        exp_cs_e = _dot(cols_2, expand_ref[2, :, wcols])
        xdt_bf = (xc * dt_e).astype(BF16)
        xdecay_bf = (xc * dtdecay_e).astype(BF16)
        bg_t = bc[:, ncols].T.astype(BF16)
        cg = cc_bf[:, ncols]
        cb = _dot(cg, bg_t)
        prev = state_ref[g]
        y_off = _dot(cg, prev.astype(BF16)) * exp_cs_e
        state_ref[g] = prev * exp_cs_e[CHUNK - 1:CHUNK, :] + _dot(bg_t, xdecay_bf)
        y_pairs = []
        for pair in range(SSM_HEADS_PER_GROUP // HEADS_PER_TILE):
            wts = []
            for r in range(HEADS_PER_TILE):
                h = g * SSM_HEADS_PER_GROUP + pair * HEADS_PER_TILE + r
                seg = cols[:, SSD_COL_CS + h:SSD_COL_CS + h + 1] - a_cs_rows[h:h + 1, :]
                lmat = jnp.exp(jnp.where(tril, seg, -jnp.inf))
                wts.append((cb * lmat).astype(BF16))
            x_tile = xdt_bf[:, pair * LANES:(pair + 1) * LANES]
            x_diag = jnp.concatenate(
                [jnp.where(tile_head == r, x_tile, jnp.zeros_like(x_tile)) for r in range(HEADS_PER_TILE)],
                axis=0)
            y_pairs.append(_dot(jnp.concatenate(wts, axis=1), x_diag))
        y = jnp.concatenate(y_pairs, axis=1) + y_off
        y = y + xc * dskip_ref[:, wcols]
        y = y * _silu(z_ref[tok, wcols].astype(F32))
        y = y * lax.rsqrt(jnp.mean(y * y, axis=-1, keepdims=True) + EPS)
        o_ref[tok, wcols] = (y * normw_ref[:, wcols]).astype(o_ref.dtype)
        yield


def _ssd_operands(proj_a, ssd_cols, ssd_rows, conv_w, conv_b, d_skip, norm_w, expand01):
    row = lambda b, c: b * STEPS_PER_SEQ + c
    const2 = lambda b, c: (0, 0)
    n_shift = (SSM_CONV - 1) * CHUNK
    out_row = lax.broadcasted_iota(jnp.int32, (n_shift, CONV_HALO + CHUNK), 0)
    src_row = lax.broadcasted_iota(jnp.int32, (n_shift, CONV_HALO + CHUNK), 1)
    shift01 = (src_row == CONV_HALO + out_row % CHUNK - out_row // CHUNK - 1).astype(BF16)
    cwx, cwb, cwc = conv_w[:, :D_MODEL], conv_w[:, D_MODEL:D_MODEL + SSM_BC], conv_w[:, D_MODEL + SSM_BC:]
    cbx, cbb, cbc = conv_b[:, :D_MODEL], conv_b[:, D_MODEL:D_MODEL + SSM_BC], conv_b[:, D_MODEL + SSM_BC:]
    in_specs = [pl.BlockSpec((STEP_ROWS, D_MODEL), lambda b, c: (row(b, c), SEG_SX)),
                pl.BlockSpec((STEP_ROWS, D_MODEL), lambda b, c: (row(b, c), SEG_SZ)),
                pl.BlockSpec((STEP_ROWS, SSM_BC), lambda b, c: (row(b, c), SEG_SB)),
                pl.BlockSpec((STEP_ROWS, SSM_BC), lambda b, c: (row(b, c), SEG_SC)),
                pl.BlockSpec((STEP_ROWS, LANES), lambda b, c: (row(b, c), 0)),
                pl.BlockSpec((1, SSM_HEADS, STEP_ROWS), lambda b, c: (b, 0, c)),
                pl.BlockSpec((SSM_CONV, D_MODEL), const2),
                pl.BlockSpec((SSM_CONV, SSM_BC), const2),
                pl.BlockSpec((SSM_CONV, SSM_BC), const2),
                pl.BlockSpec((1, D_MODEL), const2),
                pl.BlockSpec((1, SSM_BC), const2),
                pl.BlockSpec((1, SSM_BC), const2),
                pl.BlockSpec((1, D_MODEL), const2),
                pl.BlockSpec((1, D_MODEL), const2),
                pl.BlockSpec((3, 2 * LANES, D_MODEL), lambda b, c: (0, 0, 0)),
                pl.BlockSpec((n_shift, CONV_HALO + CHUNK), const2)]
    operands = [proj_a, proj_a, proj_a, proj_a, ssd_cols, ssd_rows, cwx, cwb, cwc, cbx, cbb, cbc,
                d_skip, norm_w, expand01, shift01]
    scratch = [pltpu.VMEM((CONV_HALO + STEP_ROWS, D_MODEL), BF16),
               pltpu.VMEM((CONV_HALO + STEP_ROWS, SSM_BC), BF16),
               pltpu.VMEM((CONV_HALO + STEP_ROWS, SSM_BC), BF16),
               pltpu.VMEM((SSM_GROUPS, SSM_STATE, SSM_GROUP_WIDTH), F32)]
    assert len(in_specs) == len(operands) == N_SSD_IN
    return in_specs, operands, scratch


def _mlstm_body(sub, q_ref, k_ref, v_ref, og_ref, z_ref, cols_ref, rows_ref, chunk_ref, normw_ref,
                o_ref, cstate_ref):
    tok = slice(sub * CHUNK, (sub + 1) * CHUNK)
    nh = MLSTM_HEADS
    n_tiles = MLSTM_AUG // LANES
    v_tiles = MLSTM_HEAD_DIM // LANES

    cols_t = cols_ref[tok, :]
    rows = rows_ref[0, :, tok]
    chunk = chunk_ref[0, sub]
    tril = _tril_mask(CHUNK)
    ones_tile = jnp.ones((CHUNK, LANES), BF16)
    mean_mat = jnp.full((MLSTM_HEAD_DIM, LANES), 1.0 / MLSTM_HEAD_DIM, BF16)
    k_scale = MLSTM_HEAD_DIM ** -0.5

    for h in range(nh):
        cols = slice(h * MLSTM_HEAD_DIM, (h + 1) * MLSTM_HEAD_DIM)
        qh = q_ref[tok, cols]
        k_t = k_ref[tok, cols].T
        v_aug = jnp.concatenate([v_ref[tok, cols], ones_tile], axis=1)
        c_prev = cstate_ref[h]
        g_row = rows[ML_ROW_G + h:ML_ROW_G + h + 1, :]
        wgt_row = rows[ML_ROW_WGT + h:ML_ROW_WGT + h + 1, :]
        m_prev_row = chunk[ML_CHUNK_MPREV + h:ML_CHUNK_MPREV + h + 1, :]
        s_old_row = chunk[ML_CHUNK_SOLD + h:ML_CHUNK_SOLD + h + 1, :]

        c1_d = jnp.broadcast_to(cols_t[:, ML_COL_C1 + h:ML_COL_C1 + h + 1], (CHUNK, CHUNK))
        clamp_d = jnp.broadcast_to(cols_t[:, ML_COL_CLAMP + h:ML_COL_CLAMP + h + 1], (CHUNK, CHUNK))
        dw = jnp.exp(jnp.where(tril, c1_d + g_row, -jnp.inf))
        inter_w = jnp.exp(c1_d + m_prev_row)

        s = _dot(qh, k_t) * k_scale
        intra = _dot((s * dw).astype(BF16), v_aug)
        inter = _dot(qh, c_prev.astype(BF16))
        nd = [intra[:, j * LANES:(j + 1) * LANES] + inter_w * inter[:, j * LANES:(j + 1) * LANES]
              for j in range(n_tiles)]
        rden = 1.0 / jnp.maximum(jnp.abs(nd[-1]), clamp_d)

        wk_t = (k_t.astype(F32) * wgt_row).astype(BF16)
        cstate_ref[h] = (c_prev * jnp.concatenate([s_old_row] * n_tiles, axis=1)
                         + _dot(wk_t, v_aug))

        og = _sigmoid(og_ref[tok, cols].astype(F32))
        hc = jnp.concatenate([nd[j] * rden for j in range(v_tiles)], axis=1) * og
        mu = _dot(hc.astype(BF16), mean_mat)
        dev = hc - jnp.concatenate([mu] * v_tiles, axis=1)
        var = _dot((dev * dev).astype(BF16), mean_mat)
        y = dev * jnp.concatenate([lax.rsqrt(var + EPS)] * v_tiles, axis=1)
        y = y * normw_ref[:, cols] * _silu(z_ref[tok, cols].astype(F32))
        o_ref[tok, cols] = y.astype(o_ref.dtype)
        if h % MLSTM_HEADS_PER_SLICE == MLSTM_HEADS_PER_SLICE - 1:
            yield


def _mlstm_operands(proj_m, ml_cols, ml_rows, ml_chunk, norm_w):
    row = lambda b, c: b * STEPS_PER_SEQ + c
    seg = lambda s: pl.BlockSpec((STEP_ROWS, D_MODEL), lambda b, c: (row(b, c), s))
    in_specs = [seg(SEG_MQ), seg(SEG_MK), seg(SEG_MV), seg(SEG_MO), seg(SEG_MZ),
                pl.BlockSpec((STEP_ROWS, LANES), lambda b, c: (row(b, c), 0)),
                pl.BlockSpec((1, 2 * MLSTM_HEADS, STEP_ROWS), lambda b, c: (b, 0, c)),
                pl.BlockSpec((1, CHUNKS_PER_STEP, 2 * MLSTM_HEADS, LANES), lambda b, c: (b, c, 0, 0)),
                pl.BlockSpec((1, D_MODEL), lambda b, c: (0, 0))]
    operands = [proj_m, proj_m, proj_m, proj_m, proj_m, ml_cols, ml_rows, ml_chunk, norm_w]
    scratch = [pltpu.VMEM((MLSTM_HEADS, MLSTM_HEAD_DIM, MLSTM_AUG), F32)]
    assert len(in_specs) == len(operands) == N_MLSTM_IN
    return in_specs, operands, scratch


def _recurrent_mixers_kernel(*refs):
    ssd_in = refs[:N_SSD_IN]
    ml_in = refs[N_SSD_IN:N_SSD_IN + N_MLSTM_IN]
    o_ssd_ref, o_ml_ref = refs[N_SSD_IN + N_MLSTM_IN:N_SSD_IN + N_MLSTM_IN + 2]
    extx_ref, extb_ref, extc_ref, state_ref, cstate_ref = refs[N_SSD_IN + N_MLSTM_IN + 2:]
    c = pl.program_id(1)
    _ssd_init(c, extx_ref, extb_ref, extc_ref, state_ref)

    @pl.when(c == 0)
    def _():
        cstate_ref[...] = jnp.zeros(cstate_ref.shape, F32)

    xs_ref, _, b_ref, c_ref = ssd_in[:4]
    for ext_ref, new_ref in ((extx_ref, xs_ref), (extb_ref, b_ref), (extc_ref, c_ref)):
        ext_ref[CONV_HALO:CONV_HALO + STEP_ROWS, :] = new_ref[...]
    for sub in range(CHUNKS_PER_STEP):
        slices = [_ssd_body(sub, *ssd_in, o_ssd_ref, extx_ref, extb_ref, extc_ref, state_ref),
                  _mlstm_body(sub, *ml_in, o_ml_ref, cstate_ref)]
        while slices:
            for gen in list(slices):
                if next(gen, _DONE) is _DONE:
                    slices.remove(gen)


def _recurrent_mixers(ssd_args, mlstm_args):
    s_specs, s_ops, s_scratch = _ssd_operands(*ssd_args)
    m_specs, m_ops, m_scratch = _mlstm_operands(*mlstm_args)
    out_spec = pl.BlockSpec((STEP_ROWS, D_MODEL), lambda b, c: (b * STEPS_PER_SEQ + c, 0))
    out_shape = jax.ShapeDtypeStruct((TOKENS, D_MODEL), BF16)
    return pl.pallas_call(
        _recurrent_mixers_kernel,
        grid=(BATCH, STEPS_PER_SEQ),
        in_specs=s_specs + m_specs,
        out_specs=[out_spec, out_spec],
        out_shape=[out_shape, out_shape],
        scratch_shapes=s_scratch + m_scratch,
        compiler_params=_params(2),
        name="recurrent_mixers",
    )(*s_ops, *m_ops)


MERGE_TM = 512
MERGE_TN = 1024


def _merge_kernel(y0_ref, y1_ref, y2_ref, g0_ref, g1_ref, g2_ref, bg_ref, w_ref, o_ref):
    acc = None
    for b, (y_ref, g_ref) in enumerate(((y0_ref, g0_ref), (y1_ref, g1_ref), (y2_ref, g2_ref))):
        gate = _sigmoid(g_ref[...].astype(F32) + bg_ref[b:b + 1, :])
        term = gate * _dot(y_ref[...], w_ref[b])
        acc = term if acc is None else acc + term
    o_ref[...] = acc.astype(o_ref.dtype)


def _merge(layer, y_pool, y_ssm, y_mlstm, proj_g, b_gate, w_branch_all):
    tm, tn = MERGE_TM, MERGE_TN
    per_seg = D_MODEL // tn
    y_spec = pl.BlockSpec((tm, D_MODEL), lambda i, j: (i, 0))
    gate = lambda b: pl.BlockSpec((tm, tn), lambda i, j: (i, b * per_seg + j))
    return pl.pallas_call(
        _merge_kernel,
        grid=(TOKENS // tm, D_MODEL // tn),
        in_specs=[y_spec, y_spec, y_spec, gate(0), gate(1), gate(2),
                  pl.BlockSpec((N_BRANCH, tn), lambda i, j: (0, j)),
                  pl.BlockSpec((None, N_BRANCH, D_MODEL, tn), lambda i, j: (layer, 0, 0, j))],
        out_specs=pl.BlockSpec((tm, tn), lambda i, j: (i, j)),
        out_shape=jax.ShapeDtypeStruct((TOKENS, D_MODEL), BF16),
        compiler_params=_params(2),
        name="branch_merge",
    )(y_pool, y_ssm, y_mlstm, proj_g, proj_g, proj_g, b_gate, w_branch_all)


OUT_TM = 512


def _out_kernel(m_ref, x_ref, w_ref, g_ref, b_ref, o_ref, obf_ref):
    half = OUT_TM // 2
    for r in range(2):
        rows = slice(r * half, (r + 1) * half)
        h = ALPHA * x_ref[rows, :] + _dot(m_ref[rows, :], w_ref[...])
        mu = jnp.mean(h, axis=-1, keepdims=True)
        var = jnp.mean(jnp.square(h - mu), axis=-1, keepdims=True)
        y = (h - mu) * lax.rsqrt(var + EPS) * g_ref[...] + b_ref[...]
        o_ref[rows, :] = y
        obf_ref[rows, :] = y.astype(BF16)


def _out_proj(layer, merged, x, w_out_all, ln_g, ln_b):
    tm = OUT_TM
    row = pl.BlockSpec((tm, D_MODEL), lambda i: (i, 0))
    vec = pl.BlockSpec((1, D_MODEL), lambda i: (0, 0))
    return pl.pallas_call(
        _out_kernel,
        grid=(TOKENS // tm,),
        in_specs=[row, row, pl.BlockSpec((None, D_MODEL, D_MODEL), lambda i: (layer, 0, 0)), vec, vec],
        out_specs=[row, row],
        out_shape=[jax.ShapeDtypeStruct((TOKENS, D_MODEL), F32),
                   jax.ShapeDtypeStruct((TOKENS, D_MODEL), BF16)],
        compiler_params=_params(1),
        name="out_proj_ln",
    )(merged, x, w_out_all, ln_g, ln_b)


def _lane_bcast(vec):
    return jnp.broadcast_to(vec.astype(F32)[:, None], (vec.shape[0], LANES))


def _layer(layer, x, x_mm, w_in_all, b_gate, w_pool, pool_scale, conv_w, conv_b, dt_bias, a_log,
           d_skip, ssm_norm_w, i_bias, f_bias, mlstm_norm_w, w_branch, w_out, ln_g, ln_b, expand01):
    w_s = _pack_gates(w_in_all, layer)
    small, x_bf = _gates_matmul(x_mm, w_s)
    proj_a = _in_proj(x_bf, w_in_all, layer, 0, OFF_SSM_DT, "in_proj_a")
    proj_m = _in_proj(x_bf, w_in_all, layer, OFF_MLSTM, 5 * D_MODEL, "in_proj_m")
    proj_g = _in_proj(x_bf, w_in_all, layer, OFF_MERGE_GATES, N_BRANCH * D_MODEL, "in_proj_g")

    ssd_cols, ssd_rows, ml_cols, ml_rows, ml_chunk = _gate_scan(
        small, _lane_bcast(dt_bias), _lane_bcast(a_log),
        _lane_bcast(jnp.concatenate([i_bias, f_bias])))

    y_pool = _pool_branch(proj_a, w_pool.astype(BF16), pool_scale.reshape(1, D_MODEL))
    y_ssm, y_mlstm = _recurrent_mixers(
        (proj_a, ssd_cols, ssd_rows, conv_w, conv_b.reshape(1, -1),
         jnp.repeat(d_skip, SSM_HEAD_DIM).reshape(1, D_MODEL),
         ssm_norm_w.reshape(1, D_MODEL), expand01),
        (proj_m, ml_cols, ml_rows, ml_chunk, mlstm_norm_w.reshape(1, D_MODEL)))
    merged = _merge(layer, y_pool, y_ssm, y_mlstm, proj_g, b_gate, w_branch)
    return _out_proj(layer, merged, x, w_out, ln_g.reshape(1, D_MODEL), ln_b.reshape(1, D_MODEL))


def kernel(x, w_in, b_gate, w_pool, pool_scale, conv_w, conv_b, dt_bias, a_log, d_skip,
           ssm_norm_w, i_bias, f_bias, mlstm_norm_w, w_branch, w_out, ln_g, ln_b):
    lane = lax.broadcasted_iota(jnp.int32, (3, 2 * LANES, D_MODEL), 1) % LANES
    chan = lax.broadcasted_iota(jnp.int32, (3, 2 * LANES, D_MODEL), 2)
    which = lax.broadcasted_iota(jnp.int32, (3, 2 * LANES, D_MODEL), 0)
    expand01 = (lane == which * SSM_HEADS + chan // SSM_HEAD_DIM).astype(BF16)

    w_in_t = jnp.swapaxes(w_in, 1, 2)
    w_branch_bf = w_branch.astype(BF16)
    w_out_bf = w_out.astype(BF16)
    h = x.reshape(TOKENS, D_MODEL)
    h_mm = h
    for l in range(DEPTH):
        h, h_mm = _layer(l, h, h_mm, w_in_t, b_gate[l], w_pool[l], pool_scale[l], conv_w[l],
                         conv_b[l], dt_bias[l], a_log[l], d_skip[l], ssm_norm_w[l], i_bias[l],
                         f_bias[l], mlstm_norm_w[l], w_branch_bf, w_out_bf, ln_g[l], ln_b[l],
                         expand01)
    return h.reshape(BATCH, SEQ, D_MODEL)
```
